```python
import math
import jax
import jax.numpy as jnp
from jax import lax
import numpy as np

D_MODEL = 2048
BATCH = 4
SEQ = 2048
DEPTH = 4
DEC_BATCH = 128
DEC_SEQ = 4
PAST_LEN = 16384
PAGE_SIZE = 128

N_MIXERS = 2
N_A = (DEPTH + 1) // 2
N_B = DEPTH // 2
M_HEADS = 8
M_DV = D_MODEL // M_HEADS
M_DK = M_DV // 2
M_CHUNK = 64
HK = M_HEADS * M_DK
HV = M_HEADS * M_DV
A_IN = 2 * HK + 2 * HV + 2 * M_HEADS
S_GROUPS = 8
S_CHUNK = 128
S_WIDTH = D_MODEL
S_DG = S_WIDTH // S_GROUPS
P_HEADS = 8
P_NKEYS = 128
P_NEXP = P_NKEYS * P_NKEYS
P_DQ = 256
P_DH = P_DQ // 2
P_TOPK = 16
P_BLOCK = 128
PLE_DIM = 256
ALPHA = (2 * DEPTH) ** 0.25
BETA = (8 * DEPTH) ** -0.25
LN_EPS = 1e-5

kernel_name = 'mlstm_chunkmlp_peer_hybrid_step'


def layer_norm(x, g, b):
    xf = x.astype(jnp.float32)
    mu = jnp.mean(xf, -1, keepdims=True)
    var = jnp.mean(jnp.square(xf - mu), -1, keepdims=True)
    y = (xf - mu) * lax.rsqrt(var + LN_EPS)
    return (y * g.astype(jnp.float32) + b.astype(jnp.float32)).astype(x.dtype)


def gain_norm(x, g):
    xf = x.astype(jnp.float32)
    mu = jnp.mean(xf, -1, keepdims=True)
    var = jnp.mean(jnp.square(xf - mu), -1, keepdims=True)
    return (xf - mu) * lax.rsqrt(var + LN_EPS) * g.astype(jnp.float32)


def _to_chunks(a, nc, L):
    B = a.shape[0]
    a = a.reshape((B, nc, L) + a.shape[2:])
    return jnp.swapaxes(jnp.moveaxis(a, 1, 0), 2, 3)


def mlstm_scan(q, k, v, ig, lf, C0, n0, m0):
    B, S = q.shape[:2]
    L = math.gcd(S, M_CHUNK)
    nc = S // L
    causal = jnp.tril(jnp.ones((L, L), dtype=bool))

    def step(carry, xs):
        C, n, m = carry
        qc, kc, vc, ic, fc = xs
        b = jnp.cumsum(fc, axis=-1)
        logd = jnp.where(causal, b[..., :, None] - b[..., None, :] + ic[..., None, :], -jnp.inf)
        inter = b + m[..., None]
        m_row = jnp.maximum(inter, jnp.max(logd, axis=-1))
        s = jnp.einsum('bhld,bhsd->bhls', qc, kc) * jnp.exp(logd - m_row[..., None])
        w_inter = jnp.exp(inter - m_row)
        num = w_inter[..., None] * jnp.einsum('bhld,bhde->bhle', qc, C) + jnp.einsum('bhls,bhse->bhle', s, vc)
        den = w_inter * jnp.einsum('bhld,bhd->bhl', qc, n) + jnp.sum(s, axis=-1)
        h = num / jnp.maximum(jnp.abs(den), jnp.exp(-m_row))[..., None]
        m_new = m_row[..., -1]
        g_state = jnp.exp(b[..., -1] + m - m_new)
        g_tok = jnp.exp(b[..., -1:] - b + ic - m_new[..., None])
        C_new = g_state[..., None, None] * C + jnp.einsum('bhl,bhld,bhle->bhde', g_tok, kc, vc)
        n_new = g_state[..., None] * n + jnp.einsum('bhl,bhld->bhd', g_tok, kc)
        return (C_new, n_new, m_new), h

    xs = tuple(_to_chunks(a, nc, L) for a in (q, k, v, ig, lf))
    (C, n, m), h = lax.scan(step, (C0, n0, m0), xs)
    h = jnp.transpose(h, (1, 0, 3, 2, 4)).reshape(B, S, M_HEADS, M_DV)
    return h, C, n, m


def mlstm_mixer(x, C0, n0, m0, w_in, b_gate, norm_w, w_out):
    B, S, _ = x.shape
    proj = x @ w_in
    cuts = [HK, 2 * HK, 2 * HK + HV, 2 * HK + 2 * HV, 2 * HK + 2 * HV + M_HEADS]
    q, k, v, o, gi, gf = jnp.split(proj, cuts, axis=-1)
    q = q.reshape(B, S, M_HEADS, M_DK).astype(jnp.float32) * (M_DK ** -0.5)
    k = k.reshape(B, S, M_HEADS, M_DK).astype(jnp.float32)
    v = v.reshape(B, S, M_HEADS, M_DV).astype(jnp.float32)
    ig = gi.astype(jnp.float32) + b_gate[:M_HEADS].astype(jnp.float32)
    lf = jax.nn.log_sigmoid(gf.astype(jnp.float32) + b_gate[M_HEADS:].astype(jnp.float32))
    h, C, n, m = mlstm_scan(q, k, v, ig, lf, C0.astype(jnp.float32), n0.astype(jnp.float32), m0.astype(jnp.float32))
    h = gain_norm(h, norm_w.reshape(M_HEADS, M_DV))
    h = h.reshape(B, S, HV).astype(x.dtype) * jax.nn.sigmoid(o)
    return h @ w_out, C, n, m


def chunk_mlp_mixer(x, w_in, b_in, vnorm_w, w_s, b_s, w_out):
    B, S, _ = x.shape
    L = min(S, S_CHUNK)
    nc = -(-S // L)
    hdn = jax.nn.gelu(x @ w_in + b_in)
    u, v = jnp.split(hdn, 2, axis=-1)
    v = gain_norm(v, vnorm_w).astype(x.dtype)
    vg = jnp.pad(v, ((0, 0), (0, nc * L - S), (0, 0))).reshape(B, nc, L, S_GROUPS, S_DG)
    ws = jnp.tril(w_s[:, :L, :L])
    mixed = jnp.einsum('gts,bcsgd->bctgd', ws, vg) + b_s[:, :L].T[None, None, :, :, None]
    mixed = mixed.reshape(B, nc * L, S_WIDTH)[:, :S]
    return (u * mixed) @ w_out, v


def peer_ffn(x, w_q, sub_keys, u_tab, v_tab):
    B, S, D = x.shape
    T = B * S
    xt = x.reshape(T, D)
    q = (xt @ w_q).reshape(T, P_HEADS, 2, P_DH).astype(jnp.float32)
    sc = jnp.einsum('thcd,ckd->thck', q, sub_keys.astype(jnp.float32))
    sv, si = lax.top_k(sc, P_TOPK)
    cand = (sv[:, :, 0, :, None] + sv[:, :, 1, None, :]).reshape(T, P_HEADS, P_TOPK * P_TOPK)
    cidx = (si[:, :, 0, :, None] * P_NKEYS + si[:, :, 1, None, :]).reshape(T, P_HEADS, P_TOPK * P_TOPK)
    top_s, pos = lax.top_k(cand, P_TOPK)
    eidx = jnp.take_along_axis(cidx, pos, axis=-1)
    gate = jax.nn.softmax(top_s, axis=-1)
    blk = min(P_BLOCK, T)
    nb = -(-T // blk)
    pad = nb * blk - T
    xb = jnp.pad(xt, ((0, pad), (0, 0))).reshape(nb, blk, D)
    eb = jnp.pad(eidx, ((0, pad), (0, 0), (0, 0))).reshape(nb, blk, P_HEADS, P_TOPK)
    gb = jnp.pad(gate, ((0, pad), (0, 0), (0, 0))).reshape(nb, blk, P_HEADS, P_TOPK)

    def block(args):
        xs, es, gs = args
        act = jax.nn.gelu(jnp.einsum('td,thkd->thk', xs, u_tab[es]).astype(jnp.float32)) * gs
        return jnp.einsum('thk,thkd->td', act.astype(xs.dtype), v_tab[es])

    out = lax.map(block, (xb, eb, gb)).reshape(nb * blk, D)[:T]
    return out.reshape(B, S, D)


def trunk(x, p, C_in, n_in, m_in, w_a_in, b_a_gate, a_norm_w, w_a_out, w_b_in, b_b_in, b_norm_w, w_b_s, b_b_s, w_b_out, ln_g, ln_b, peer_wq, peer_keys, peer_u, peer_v, ple_w, ple_gate_w):
    Cs, ns, ms, vs = [], [], [], []
    for i in range(DEPTH):
        j = i // N_MIXERS
        if i % N_MIXERS == 0:
            y, C, n, m = mlstm_mixer(x, C_in[j], n_in[j], m_in[j], w_a_in[j], b_a_gate[j], a_norm_w[j], w_a_out[j])
            Cs.append(C)
            ns.append(n)
            ms.append(m)
        else:
            y, v = chunk_mlp_mixer(x, w_b_in[j], b_b_in[j], b_norm_w[j], w_b_s[j], b_b_s[j], w_b_out[j])
            vs.append(v)
        x = layer_norm(ALPHA * x + y, ln_g[i, 0], ln_b[i, 0])
        x = layer_norm(ALPHA * x + peer_ffn(x, peer_wq[i], peer_keys[i], peer_u[i], peer_v[i]), ln_g[i, 1], ln_b[i, 1])
        x = x + jax.nn.sigmoid(x @ ple_gate_w[i]) * (p[i].astype(x.dtype) @ ple_w[i])
    return x, jnp.stack(Cs), jnp.stack(ns), jnp.stack(ms), jnp.stack(vs)


def setup_inputs(seed: int = 0) -> dict:
    key = jax.random.key(seed)
    ks = iter(jax.random.split(key, 40))

    def nrm(shape, scale):
        return jax.random.normal(next(ks), shape, jnp.float32) * scale

    f_bias = jax.random.uniform(next(ks), (N_A, M_HEADS), jnp.float32, 3.0, 6.0)
    i_bias = nrm((N_A, M_HEADS), 0.1)
    return {
        'x_prompt': nrm((BATCH, SEQ, D_MODEL), 1.0),
        'x_sample': nrm((DEC_BATCH, DEC_SEQ, D_MODEL), 1.0),
        'state_C': nrm((N_A, DEC_BATCH, M_HEADS, M_DK, M_DV), 0.1),
        'state_n': nrm((N_A, DEC_BATCH, M_HEADS, M_DK), 0.5),
        'state_m': nrm((N_A, DEC_BATCH, M_HEADS), 1.0),
        'p_prompt': nrm((DEPTH, BATCH, SEQ, PLE_DIM), 1.0),
        'p_sample': nrm((DEPTH, DEC_BATCH, DEC_SEQ, PLE_DIM), 1.0),
        'w_a_in': nrm((N_A, D_MODEL, A_IN), D_MODEL ** -0.5),
        'b_a_gate': jnp.concatenate([i_bias, f_bias], axis=-1),
        'a_norm_w': 1.0 + nrm((N_A, HV), 0.02),
        'w_a_out': nrm((N_A, HV, D_MODEL), BETA * HV ** -0.5),
        'w_b_in': nrm((N_B, D_MODEL, 2 * S_WIDTH), D_MODEL ** -0.5),
        'b_b_in': nrm((N_B, 2 * S_WIDTH), 0.02),
        'b_norm_w': 1.0 + nrm((N_B, S_WIDTH), 0.02),
        'w_b_s': nrm((N_B, S_GROUPS, S_CHUNK, S_CHUNK), S_CHUNK ** -0.5),
        'b_b_s': 1.0 + nrm((N_B, S_GROUPS, S_CHUNK), 0.1),
        'w_b_out': nrm((N_B, S_WIDTH, D_MODEL), BETA * S_WIDTH ** -0.5),
        'ln_g': 1.0 + nrm((DEPTH, 2, D_MODEL), 0.02),
        'ln_b': nrm((DEPTH, 2, D_MODEL), 0.02),
        'peer_wq': nrm((DEPTH, D_MODEL, P_HEADS * P_DQ), D_MODEL ** -0.5),
        'peer_keys': nrm((DEPTH, 2, P_NKEYS, P_DH), P_DH ** -0.5),
        'peer_u': nrm((DEPTH, P_NEXP, D_MODEL), D_MODEL ** -0.5),
        'peer_v': nrm((DEPTH, P_NEXP, D_MODEL), BETA * P_HEADS ** -0.5),
        'ple_w': nrm((DEPTH, PLE_DIM, D_MODEL), PLE_DIM ** -0.5),
        'ple_gate_w': nrm((DEPTH, D_MODEL, D_MODEL), D_MODEL ** -0.5),
    }


def reference(x_prompt, x_sample, state_C, state_n, state_m, p_prompt, p_sample, w_a_in, b_a_gate, a_norm_w, w_a_out, w_b_in, b_b_in, b_norm_w, w_b_s, b_b_s, w_b_out, ln_g, ln_b, peer_wq, peer_keys, peer_u, peer_v, ple_w, ple_gate_w):
    B = x_prompt.shape[0]
    C0 = jnp.zeros((N_A, B, M_HEADS, M_DK, M_DV), jnp.float32)
    n0 = jnp.zeros((N_A, B, M_HEADS, M_DK), jnp.float32)
    m0 = jnp.zeros((N_A, B, M_HEADS), jnp.float32)
    y_prompt, prompt_C, prompt_n, prompt_m, _ = trunk(
        x_prompt, p_prompt, C0, n0, m0, w_a_in, b_a_gate, a_norm_w, w_a_out, w_b_in, b_b_in, b_norm_w, w_b_s, b_b_s, w_b_out,
        ln_g, ln_b, peer_wq, peer_keys, peer_u, peer_v, ple_w, ple_gate_w)
    y_sample, sample_C, sample_n, sample_m, sample_v = trunk(
        x_sample, p_sample, state_C, state_n, state_m, w_a_in, b_a_gate, a_norm_w, w_a_out, w_b_in, b_b_in, b_norm_w, w_b_s, b_b_s, w_b_out,
        ln_g, ln_b, peer_wq, peer_keys, peer_u, peer_v, ple_w, ple_gate_w)
    return (y_prompt, y_sample, prompt_C, prompt_n, prompt_m, sample_C, sample_n, sample_m, sample_v)
```

```python
import functools

import jax
import jax.numpy as jnp
from jax import lax
from jax.experimental import pallas as pl
from jax.experimental.pallas import tpu as pltpu

F32 = jnp.float32
BF16 = jnp.bfloat16

D_MODEL = 2048
DEPTH = 4
N_MIXERS = 2
M_HEADS = 8
M_DV = D_MODEL // M_HEADS
M_DK = M_DV // 2
HK = M_HEADS * M_DK
HV = M_HEADS * M_DV
S_GROUPS = 8
S_CHUNK = 128
S_WIDTH = D_MODEL
S_DG = S_WIDTH // S_GROUPS
P_HEADS = 8
P_NKEYS = 128
P_NEXP = P_NKEYS * P_NKEYS
P_DH = 128
P_TOPK = 16
ALPHA = (2 * DEPTH) ** 0.25
LN_EPS = 1e-5

LANES = 128
VMEM_LIMIT = 56 * 1024 * 1024
NEG_INF = float("-inf")

MLSTM_CHUNK = 256
MLSTM_PAD = 16
PEER_TB = 512
PEER_TE = 512


def _params(sem):
    return pltpu.CompilerParams(dimension_semantics=sem, vmem_limit_bytes=VMEM_LIMIT)


def _layer_norm_rows(z, g, b):
    mu = jnp.mean(z, axis=-1, keepdims=True)
    zc = z - mu
    var = jnp.mean(zc * zc, axis=-1, keepdims=True)
    return zc * lax.rsqrt(var + LN_EPS) * g + b


def _mm_kernel(x_ref, w_ref, b_ref, o_ref, *, act):
    acc = jnp.dot(x_ref[...].astype(BF16), w_ref[...], preferred_element_type=F32)
    acc = acc + b_ref[...]
    if act == "gelu":
        acc = jax.nn.gelu(acc)
    o_ref[...] = acc.astype(o_ref.dtype)


def _mm(x, w, bias, *, act=None, out_dtype=F32, tm=512, tn=512):
    m, k = x.shape
    n = w.shape[1]
    tn = min(tn, n)
    return pl.pallas_call(
        functools.partial(_mm_kernel, act=act),
        grid=(m // tm, n // tn),
        in_specs=[
            pl.BlockSpec((tm, k), lambda i, j: (i, 0)),
            pl.BlockSpec((k, tn), lambda i, j: (0, j)),
            pl.BlockSpec((1, tn), lambda i, j: (0, j)),
        ],
        out_specs=pl.BlockSpec((tm, tn), lambda i, j: (i, j)),
        out_shape=jax.ShapeDtypeStruct((m, n), out_dtype),
        compiler_params=_params(("parallel", "parallel")),
        name="mm",
    )(x, w, bias)


def _mm_ln_kernel(a_ref, w_ref, r_ref, g_ref, b_ref, of_ref, ob_ref):
    y = jnp.dot(a_ref[...].astype(BF16), w_ref[...], preferred_element_type=F32)
    o = _layer_norm_rows(ALPHA * r_ref[...] + y, g_ref[...], b_ref[...])
    of_ref[...] = o
    ob_ref[...] = o.astype(BF16)


def _mm_ln(a, w, res, g, b, *, tm=256):
    m, k = a.shape
    n = w.shape[1]
    return pl.pallas_call(
        _mm_ln_kernel,
        grid=(m // tm,),
        in_specs=[
            pl.BlockSpec((tm, k), lambda i: (i, 0)),
            pl.BlockSpec((k, n), lambda i: (0, 0)),
            pl.BlockSpec((tm, n), lambda i: (i, 0)),
            pl.BlockSpec((1, n), lambda i: (0, 0)),
            pl.BlockSpec((1, n), lambda i: (0, 0)),
        ],
        out_specs=[pl.BlockSpec((tm, n), lambda i: (i, 0)), pl.BlockSpec((tm, n), lambda i: (i, 0))],
        out_shape=[jax.ShapeDtypeStruct((m, n), F32), jax.ShapeDtypeStruct((m, n), BF16)],
        compiler_params=_params(("parallel",)),
        name="mm_ln",
    )(a, w, res, g, b)


def _ple_kernel(xf_ref, xb_ref, gw_ref, p_ref, pw_ref, of_ref, ob_ref):
    gate = jax.nn.sigmoid(jnp.dot(xb_ref[...], gw_ref[...], preferred_element_type=F32))
    pe = jnp.dot(p_ref[...].astype(BF16), pw_ref[...], preferred_element_type=F32)
    o = xf_ref[...] + gate * pe
    of_ref[...] = o
    ob_ref[...] = o.astype(BF16)


def _ple(xf, xb, gw, p, pw, *, tm=256):
    m, n = xf.shape
    kp = p.shape[1]
    return pl.pallas_call(
        _ple_kernel,
        grid=(m // tm,),
        in_specs=[
            pl.BlockSpec((tm, n), lambda i: (i, 0)),
            pl.BlockSpec((tm, n), lambda i: (i, 0)),
            pl.BlockSpec((n, n), lambda i: (0, 0)),
            pl.BlockSpec((tm, kp), lambda i: (i, 0)),
            pl.BlockSpec((kp, n), lambda i: (0, 0)),
        ],
        out_specs=[pl.BlockSpec((tm, n), lambda i: (i, 0)), pl.BlockSpec((tm, n), lambda i: (i, 0))],
        out_shape=[jax.ShapeDtypeStruct((m, n), F32), jax.ShapeDtypeStruct((m, n), BF16)],
        compiler_params=_params(("parallel",)),
        name="ple",
    )(xf, xb, gw, p, pw)


def _group_causal(rr, cc, group):
    shift = group.bit_length() - 1
    same = (rr >> shift) == (cc >> shift)
    return jnp.where(rr >= cc, jnp.where(same, 1.0, 0.0), 0.0)


def _split3(x):
    hi = x.astype(BF16)
    r1 = x - hi.astype(F32)
    mid = r1.astype(BF16)
    lo = (r1 - mid.astype(F32)).astype(BF16)
    return hi, mid, lo


def _gate_prep_kernel(g_ref, gb_ref, x_ref, xt_ref, *, seg, valid):
    rows = g_ref.shape[0]
    g = g_ref[...] + gb_ref[...]
    lane = lax.broadcasted_iota(jnp.int32, (rows, LANES), 1)
    r = lax.broadcasted_iota(jnp.int32, (rows, LANES), 0)
    pad = (r & (seg - 1)) >= valid
    lf = jnp.where(pad, 0.0, jax.nn.log_sigmoid(g))
    ig = jnp.where(pad, NEG_INF, g)
    rr = lax.broadcasted_iota(jnp.int32, (rows, rows), 0)
    cc = lax.broadcasted_iota(jnp.int32, (rows, rows), 1)
    tri = _group_causal(rr, cc, seg).astype(BF16)
    hi, mid, lo = _split3(lf)
    bcum = (jnp.dot(tri, hi, preferred_element_type=F32)
            + jnp.dot(tri, mid, preferred_element_type=F32)
            + jnp.dot(tri, lo, preferred_element_type=F32))
    x = jnp.where(lane >= M_HEADS, bcum, ig)
    x_ref[...] = x
    xt_ref[...] = x.T


def _gate_prep(gates, gbias, *, seg, valid, rows):
    m = gates.shape[0]
    return pl.pallas_call(
        functools.partial(_gate_prep_kernel, seg=seg, valid=valid),
        grid=(m // rows,),
        in_specs=[pl.BlockSpec((rows, LANES), lambda i: (i, 0)), pl.BlockSpec((1, LANES), lambda i: (0, 0))],
        out_specs=[pl.BlockSpec((rows, LANES), lambda i: (i, 0)), pl.BlockSpec((LANES, rows), lambda i: (0, i))],
        out_shape=[jax.ShapeDtypeStruct((m, LANES), F32), jax.ShapeDtypeStruct((LANES, m), F32)],
        compiler_params=_params(("parallel",)),
        name="gate_prep",
    )(gates, gbias)


def _mlstm_kernel(q_ref, k_ref, v_ref, o_ref, x_ref, irow_ref, brow_ref, nw_ref, c0_ref, n0_ref, m0_ref,
                  hg_ref, c_ref, n_ref, m_ref, cs, ns, ms, *, bb, seg):
    h = pl.program_id(1)
    c = pl.program_id(2)
    rows = bb * seg
    lane = lax.broadcasted_iota(jnp.int32, (rows, LANES), 1)

    @pl.when(c == 0)
    def _():
        cs[...] = c0_ref[...]
        ns[...] = n0_ref[...]
        ms[...] = jnp.broadcast_to(m0_ref[...], (bb, 1, LANES))

    x = x_ref[...]
    icol_all = jnp.sum(jnp.where(lane == h, x, 0.0), axis=1, keepdims=True)
    bcol_all = jnp.sum(jnp.where(lane == h + M_HEADS, x, 0.0), axis=1, keepdims=True)
    irow_all = irow_ref[...]
    brow_all = brow_ref[...]
    rr = lax.broadcasted_iota(jnp.int32, (seg, seg), 0)
    cc = lax.broadcasted_iota(jnp.int32, (seg, seg), 1)
    causal = rr >= cc
    scale = M_DK ** -0.5
    nw = nw_ref[...]

    for bi in range(bb):
        lo_r, hi_r = bi * seg, (bi + 1) * seg
        icol = icol_all[lo_r:hi_r]
        bcol = bcol_all[lo_r:hi_r]
        irow = irow_all[:, lo_r:hi_r]
        brow = brow_all[:, lo_r:hi_r]
        c_prev = cs[bi]
        n_prev = ns[bi]
        m_prev = ms[bi][:, :1]
        q = q_ref[lo_r:hi_r, :] * scale
        k = k_ref[lo_r:hi_r, :]
        v = v_ref[lo_r:hi_r, :]
        qb = q.astype(BF16)
        kb = k.astype(BF16)
        vb = v.astype(BF16)

        logd = jnp.where(causal, bcol - brow + irow, NEG_INF)
        inter = bcol + m_prev
        m_row = jnp.maximum(inter, jnp.max(logd, axis=1, keepdims=True))
        dmat = jnp.exp(logd - m_row)
        s = lax.dot_general(qb, kb, (((1,), (1,)), ((), ())), preferred_element_type=F32) * dmat
        w_inter = jnp.exp(inter - m_row)
        num = (w_inter * jnp.dot(qb, c_prev.astype(BF16), preferred_element_type=F32)
               + jnp.dot(s.astype(BF16), vb, preferred_element_type=F32))
        den = w_inter * jnp.sum(q * n_prev, axis=1, keepdims=True) + jnp.sum(s, axis=1, keepdims=True)
        hh = num / jnp.maximum(jnp.abs(den), jnp.exp(-m_row))
        mu = jnp.mean(hh, axis=1, keepdims=True)
        hc = hh - mu
        var = jnp.mean(hc * hc, axis=1, keepdims=True)
        hn = hc * lax.rsqrt(var + LN_EPS) * nw
        hg_ref[lo_r:hi_r, :] = (hn * jax.nn.sigmoid(o_ref[lo_r:hi_r, :])).astype(hg_ref.dtype)

        m_new = m_row[seg - 1:seg, :]
        b_last = bcol[seg - 1:seg, :]
        g_state = jnp.exp(b_last + m_prev - m_new)
        g_tok = jnp.exp(b_last - bcol + icol - m_new)
        kg = g_tok * k
        c_new = g_state * c_prev + lax.dot_general(
            kg.astype(BF16), vb, (((0,), (0,)), ((), ())), preferred_element_type=F32)
        n_new = g_state * n_prev + jnp.sum(kg, axis=0, keepdims=True)
        m_new_b = jnp.broadcast_to(m_new, (1, LANES))
        cs[bi] = c_new
        ns[bi] = n_new
        ms[bi] = m_new_b
        c_ref[bi] = c_new
        n_ref[bi] = n_new
        m_ref[bi] = m_new_b


def _mlstm_scan(proj, gates, gbias, norm_w, c0, n0, m0, *, batch, nc, seg, valid, bb):
    rows = bb * seg
    nb = batch // bb
    kq, kv = M_DK, M_DV
    xg, xgt = _gate_prep(gates, gbias, seg=seg, valid=valid, rows=rows)
    xgt = xgt[:2 * M_HEADS].reshape(2 * M_HEADS, 1, -1)
    row_map = lambda col: (lambda i, h, c: (i * nc + c, col(h)))
    st_map = lambda i, h, c: (i, h, 0, 0)
    n0r = n0.reshape(batch, M_HEADS, 1, M_DK)
    m0r = m0.reshape(batch, M_HEADS, 1, 1)
    hg, c_out, n_out, m_out = pl.pallas_call(
        functools.partial(_mlstm_kernel, bb=bb, seg=seg),
        grid=(nb, M_HEADS, nc),
        in_specs=[
            pl.BlockSpec((rows, kq), row_map(lambda h: h)),
            pl.BlockSpec((rows, kq), row_map(lambda h: HK // kq + h)),
            pl.BlockSpec((rows, kv), row_map(lambda h: 2 * HK // kv + h)),
            pl.BlockSpec((rows, kv), row_map(lambda h: (2 * HK + HV) // kv + h)),
            pl.BlockSpec((rows, LANES), lambda i, h, c: (i * nc + c, 0)),
            pl.BlockSpec((None, 1, rows), lambda i, h, c: (h, 0, i * nc + c)),
            pl.BlockSpec((None, 1, rows), lambda i, h, c: (h + M_HEADS, 0, i * nc + c)),
            pl.BlockSpec((1, kv), lambda i, h, c: (0, h)),
            pl.BlockSpec((bb, None, kq, kv), st_map),
            pl.BlockSpec((bb, None, 1, kq), st_map),
            pl.BlockSpec((bb, None, 1, 1), st_map),
        ],
        out_specs=[
            pl.BlockSpec((rows, kv), lambda i, h, c: (i * nc + c, h)),
            pl.BlockSpec((bb, None, kq, kv), st_map),
            pl.BlockSpec((bb, None, 1, kq), st_map),
            pl.BlockSpec((bb, None, 1, LANES), st_map),
        ],
        out_shape=[
            jax.ShapeDtypeStruct((batch * nc * seg, HV), BF16),
            jax.ShapeDtypeStruct((batch, M_HEADS, kq, kv), F32),
            jax.ShapeDtypeStruct((batch, M_HEADS, 1, kq), F32),
            jax.ShapeDtypeStruct((batch, M_HEADS, 1, LANES), F32),
        ],
        scratch_shapes=[
            pltpu.VMEM((bb, kq, kv), F32),
            pltpu.VMEM((bb, 1, kq), F32),
            pltpu.VMEM((bb, 1, LANES), F32),
        ],
        compiler_params=_params(("parallel", "parallel", "arbitrary")),
        name="mlstm_scan",
    )(proj, proj, proj, proj, xg, xgt, xgt, norm_w, c0, n0r, m0r)
    return hg, c_out, n_out[:, :, 0, :], m_out[:, :, 0, 0]


def _mix_kernel(h_ref, w_ref, bias_ref, vw_ref, um_ref, v_ref, *, lc):
    t = h_ref.shape[0]
    vraw = h_ref[:, S_WIDTH:]
    mu = jnp.mean(vraw, axis=-1, keepdims=True)
    vc = vraw - mu
    var = jnp.mean(vc * vc, axis=-1, keepdims=True)
    v = vc * lax.rsqrt(var + LN_EPS) * vw_ref[...]
    v_ref[...] = v
    vb = v.astype(BF16)
    rr = lax.broadcasted_iota(jnp.int32, (t, t), 0)
    cc = lax.broadcasted_iota(jnp.int32, (t, t), 1)
    keep = _group_causal(rr, cc, lc)
    for g in range(S_GROUPS):
        sl = slice(g * S_DG, (g + 1) * S_DG)
        wg = (w_ref[g] * keep).astype(BF16)
        mixed = jnp.dot(wg, vb[:, sl], preferred_element_type=F32) + bias_ref[:, g:g + 1]
        um_ref[:, sl] = (h_ref[:, sl] * mixed).astype(BF16)


def _mix(hdn, wmix, bias_t, vnorm_w, *, lc):
    m = hdn.shape[0]
    t = S_CHUNK
    return pl.pallas_call(
        functools.partial(_mix_kernel, lc=lc),
        grid=(m // t,),
        in_specs=[
            pl.BlockSpec((t, 2 * S_WIDTH), lambda i: (i, 0)),
            pl.BlockSpec((S_GROUPS, t, t), lambda i: (0, 0, 0)),
            pl.BlockSpec((t, S_GROUPS), lambda i: (0, 0)),
            pl.BlockSpec((1, S_WIDTH), lambda i: (0, 0)),
        ],
        out_specs=[pl.BlockSpec((t, S_WIDTH), lambda i: (i, 0)), pl.BlockSpec((t, S_WIDTH), lambda i: (i, 0))],
        out_shape=[jax.ShapeDtypeStruct((m, S_WIDTH), BF16), jax.ShapeDtypeStruct((m, S_WIDTH), F32)],
        compiler_params=_params(("parallel",)),
        name="chunk_mix",
    )(hdn, wmix, bias_t, vnorm_w)


def _route_kernel(xb_ref, wq_ref, keys_ref, r1_ref, e1_ref, nn_ref, c0_ref, sc_s, wk_s, rk_s, sv_s):
    tm = xb_ref.shape[0]
    q = jnp.dot(xb_ref[...], wq_ref[...], preferred_element_type=F32)
    kidx = lax.broadcasted_iota(jnp.int32, (P_NKEYS, tm), 0).astype(F32)
    for hc in range(2 * P_HEADS):
        h, c = divmod(hc, 2)
        qhc = q[:, hc * P_DH:(hc + 1) * P_DH].astype(BF16)
        sc = lax.dot_general(keys_ref[c], qhc, (((1,), (1,)), ((), ())), preferred_element_type=F32)
        sc_s[hc] = sc
        wk_s[...] = sc
        rk_s[hc] = jnp.full((P_NKEYS, tm), float(P_NKEYS), F32)

        def body(kk, carry, hc=hc, h=h, c=c):
            s = wk_s[...]
            mx = jnp.max(s, axis=0, keepdims=True)
            idx = jnp.min(jnp.where(s == mx, kidx, float(P_NKEYS)), axis=0, keepdims=True)
            sel = kidx == idx
            wk_s[...] = jnp.where(sel, NEG_INF, s)
            rk_s[hc] = jnp.where(sel, kk.astype(F32), rk_s[hc])
            sv_s[c, kk, h:h + 1, :] = mx
            return carry

        lax.fori_loop(0, P_TOPK, body, 0)

    sv0 = [sv_s[0, k] for k in range(P_TOPK)]
    sv1 = [sv_s[1, k] for k in range(P_TOPK)]
    cnt = [jnp.zeros((P_HEADS, tm), F32) for _ in range(P_TOPK)]
    front = [sv0[k] + sv1[0] for k in range(P_TOPK)]
    top = front[0]
    z = jnp.zeros((P_HEADS, tm), F32)
    for r in range(P_TOPK):
        live = min(r + 1, P_TOPK)
        mx = front[0]
        for k in range(1, live):
            mx = jnp.maximum(mx, front[k])
        pick = jnp.full((P_HEADS, tm), float(P_TOPK), F32)
        for k in reversed(range(live)):
            pick = jnp.where(front[k] == mx, float(k), pick)
        z = z + jnp.exp(mx - top)
        hits = [pick == float(k) for k in range(live)]
        newcnt = jnp.zeros((P_HEADS, tm), F32)
        for k in range(live):
            cnt[k] = cnt[k] + jnp.where(hits[k], 1.0, 0.0)
            newcnt = jnp.where(hits[k], cnt[k], newcnt)
        nxt = jnp.full((P_HEADS, tm), NEG_INF, F32)
        for j in range(1, min(r + 2, P_TOPK)):
            nxt = jnp.where(newcnt == float(j), sv1[j], nxt)
        for k in range(live):
            front[k] = jnp.where(hits[k], sv0[k] + nxt, front[k])
    zinv = 1.0 / z

    for h in range(P_HEADS):
        rank0 = rk_s[2 * h]
        nn = jnp.zeros((P_NKEYS, tm), F32)
        for k in range(P_TOPK):
            nn = jnp.where(rank0 == float(k), cnt[k][h:h + 1, :], nn)
        nn_ref[h] = nn
        c0_ref[h] = jnp.exp(sc_s[2 * h] - sv0[0][h:h + 1, :]) * zinv[h:h + 1, :]
        r1_ref[h] = rk_s[2 * h + 1]
        e1_ref[h] = jnp.exp(sc_s[2 * h + 1] - sv1[0][h:h + 1, :])


def _route(xb, wq, keys, *, tm=256):
    m, k = xb.shape
    tab = jax.ShapeDtypeStruct((P_HEADS, P_NKEYS, m), F32)
    tab_spec = pl.BlockSpec((P_HEADS, P_NKEYS, tm), lambda i: (0, 0, i))
    return pl.pallas_call(
        _route_kernel,
        grid=(m // tm,),
        in_specs=[
            pl.BlockSpec((tm, k), lambda i: (i, 0)),
            pl.BlockSpec((k, P_HEADS * 2 * P_DH), lambda i: (0, 0)),
            pl.BlockSpec((2, P_NKEYS, P_DH), lambda i: (0, 0, 0)),
        ],
        out_specs=[tab_spec, tab_spec, tab_spec, tab_spec],
        out_shape=[tab, tab, tab, tab],
        scratch_shapes=[
            pltpu.VMEM((2 * P_HEADS, P_NKEYS, tm), F32),
            pltpu.VMEM((P_NKEYS, tm), F32),
            pltpu.VMEM((2 * P_HEADS, P_NKEYS, tm), F32),
            pltpu.VMEM((2, P_TOPK, P_HEADS, tm), F32),
        ],
        compiler_params=_params(("parallel",)),
        name="peer_route",
    )(xb, wq, keys)


def _peer_kernel(xb_ref, u_ref, v_ref, r1_ref, e1_ref, nn_ref, c0_ref, xf_ref, g_ref, b_ref,
                 of_ref, ob_ref, acc_s, st_s, act_s, *, te):
    e = pl.program_id(1)
    ne = pl.num_programs(1)
    n_i1 = te // P_NKEYS

    @pl.when(e == 0)
    def _():
        acc_s[...] = jnp.zeros_like(acc_s)

    st_s[...] = lax.dot_general(u_ref[...], xb_ref[...], (((1,), (1,)), ((), ())),
                                preferred_element_type=F32)

    def slab(j, carry):
        i1 = e * n_i1 + j
        row0 = pl.multiple_of(j * P_NKEYS, P_NKEYS)
        gate = jnp.zeros((P_NKEYS, st_s.shape[1]), F32)
        for h in range(P_HEADS):
            nn = nn_ref[h, pl.ds(i1, 1), :]
            c0 = c0_ref[h, pl.ds(i1, 1), :]
            gate = gate + jnp.where(r1_ref[h] < nn, e1_ref[h], 0.0) * c0
        act = jax.nn.gelu(st_s[pl.ds(row0, P_NKEYS), :]) * gate
        act_s[pl.ds(row0, P_NKEYS), :] = act.astype(BF16)
        return carry

    lax.fori_loop(0, n_i1, slab, 0)
    acc_s[...] += lax.dot_general(act_s[...], v_ref[...], (((0,), (0,)), ((), ())),
                                  preferred_element_type=F32)

    @pl.when(e == ne - 1)
    def _():
        o = _layer_norm_rows(ALPHA * xf_ref[...] + acc_s[...], g_ref[...], b_ref[...])
        of_ref[...] = o
        ob_ref[...] = o.astype(BF16)


def _peer(xb, xf, u, v, r1, e1, nn, c0, g, b, *, tb=PEER_TB, te=PEER_TE):
    m, d = xf.shape
    once = pl.Buffered(1)
    tab_spec = pl.BlockSpec((P_HEADS, P_NKEYS, tb), lambda i, e: (0, 0, i), pipeline_mode=once)
    return pl.pallas_call(
        functools.partial(_peer_kernel, te=te),
        grid=(m // tb, P_NEXP // te),
        in_specs=[
            pl.BlockSpec((tb, d), lambda i, e: (i, 0), pipeline_mode=once),
            pl.BlockSpec((te, d), lambda i, e: (e, 0)),
            pl.BlockSpec((te, d), lambda i, e: (e, 0)),
            tab_spec, tab_spec, tab_spec, tab_spec,
            pl.BlockSpec((tb, d), lambda i, e: (i, 0), pipeline_mode=once),
            pl.BlockSpec((1, d), lambda i, e: (0, 0)),
            pl.BlockSpec((1, d), lambda i, e: (0, 0)),
        ],
        out_specs=[pl.BlockSpec((tb, d), lambda i, e: (i, 0)), pl.BlockSpec((tb, d), lambda i, e: (i, 0))],
        out_shape=[jax.ShapeDtypeStruct((m, d), F32), jax.ShapeDtypeStruct((m, d), BF16)],
        scratch_shapes=[
            pltpu.VMEM((tb, d), F32),
            pltpu.VMEM((te, tb), F32),
            pltpu.VMEM((te, tb), BF16),
        ],
        compiler_params=_params(("parallel", "arbitrary")),
        name="peer_dense",
    )(xb, u, v, r1, e1, nn, c0, xf, g, b)


def _row(vec):
    return vec.reshape(1, -1).astype(F32)


def _trunk(x, p, c_in, n_in, m_in, *, batch, seq, wts):
    m = batch * seq
    xf = x
    xb = x.astype(BF16)
    decode = seq < MLSTM_PAD
    cs, ns, ms, vs = [], [], [], []
    zero_bias = {}

    def zeros_row(n):
        if n not in zero_bias:
            zero_bias[n] = jnp.zeros((1, n), F32)
        return zero_bias[n]

    for i in range(DEPTH):
        j = i // N_MIXERS
        if i % N_MIXERS == 0:
            w_main, w_gate, gbias = wts["a_in"][j]
            if decode:
                seg, valid, nc, bb = MLSTM_PAD, seq, 1, 8
                xin = jnp.pad(xb.reshape(batch, seq, D_MODEL), ((0, 0), (0, seg - seq), (0, 0)))
                xin = xin.reshape(batch * seg, D_MODEL)
            else:
                seg, valid, nc, bb = MLSTM_CHUNK, MLSTM_CHUNK, seq // MLSTM_CHUNK, 1
                xin = xb
            proj = _mm(xin, w_main, zeros_row(w_main.shape[1]))
            gates = _mm(xin, w_gate, zeros_row(LANES))
            hg, c_new, n_new, m_new = _mlstm_scan(
                proj, gates, gbias, wts["a_norm"][j], c_in[j], n_in[j], m_in[j],
                batch=batch, nc=nc, seg=seg, valid=valid, bb=bb)
            if decode:
                hg = hg.reshape(batch, seg, HV)[:, :seq].reshape(m, HV)
            cs.append(c_new)
            ns.append(n_new)
            ms.append(m_new)
            w_out = wts["a_out"][j]
        else:
            w_in, b_in = wts["b_in"][j]
            hdn = _mm(xb, w_in, b_in, act="gelu")
            lc = min(seq, S_CHUNK)
            wmix, bias_t = wts["b_mix"][j][lc]
            hg, v = _mix(hdn, wmix, bias_t, wts["b_norm"][j], lc=lc)
            vs.append(v)
            w_out = wts["b_out"][j]
        x1f, x1b = _mm_ln(hg, w_out, xf, wts["ln_g"][i][0], wts["ln_b"][i][0])
        r1, e1, nn, c0 = _route(x1b, wts["peer_wq"][i], wts["peer_keys"][i])
        x2f, x2b = _peer(x1b, x1f, wts["peer_u"][i], wts["peer_v"][i], r1, e1, nn, c0,
                         wts["ln_g"][i][1], wts["ln_b"][i][1])
        xf, xb = _ple(x2f, x2b, wts["ple_gate"][i], p[i], wts["ple_w"][i])
    return xf, cs, ns, ms, vs


def _prep_weights(w_a_in, b_a_gate, a_norm_w, w_a_out, w_b_in, b_b_in, b_norm_w, w_b_s, b_b_s, w_b_out,
                  ln_g, ln_b, peer_wq, peer_keys, peer_u, peer_v, ple_w, ple_gate_w, chunk_lens):
    n_a = w_a_in.shape[0]
    n_b = w_b_in.shape[0]
    cut = 2 * HK + 2 * HV
    wts = {"a_in": [], "a_norm": [], "a_out": [], "b_in": [], "b_mix": [], "b_norm": [], "b_out": []}
    for j in range(n_a):
        w_main = w_a_in[j, :, :cut].astype(BF16)
        w_gate = jnp.pad(w_a_in[j, :, cut:], ((0, 0), (0, LANES - 2 * M_HEADS))).astype(BF16)
        gbias = jnp.pad(b_a_gate[j], (0, LANES - 2 * M_HEADS)).reshape(1, LANES).astype(F32)
        wts["a_in"].append((w_main, w_gate, gbias))
        wts["a_norm"].append(_row(a_norm_w[j]))
        wts["a_out"].append(w_a_out[j].astype(BF16))
    for j in range(n_b):
        wts["b_in"].append((w_b_in[j].astype(BF16), _row(b_b_in[j])))
        mixes = {}
        for lc in chunk_lens:
            rep = S_CHUNK // lc
            wmix = jnp.tile(w_b_s[j][:, :lc, :lc], (1, rep, rep)).astype(F32)
            bias_t = jnp.tile(b_b_s[j][:, :lc], (1, rep)).T.astype(F32)
            mixes[lc] = (wmix, bias_t)
        wts["b_mix"].append(mixes)
        wts["b_norm"].append(_row(b_norm_w[j]))
        wts["b_out"].append(w_b_out[j].astype(BF16))
    wts["ln_g"] = [[_row(ln_g[i, s]) for s in range(2)] for i in range(DEPTH)]
    wts["ln_b"] = [[_row(ln_b[i, s]) for s in range(2)] for i in range(DEPTH)]
    wts["peer_wq"] = [peer_wq[i].astype(BF16) for i in range(DEPTH)]
    wts["peer_keys"] = [peer_keys[i].astype(BF16) for i in range(DEPTH)]
    wts["peer_u"] = [peer_u[i].astype(BF16) for i in range(DEPTH)]
    wts["peer_v"] = [peer_v[i].astype(BF16) for i in range(DEPTH)]
    wts["ple_w"] = [ple_w[i].astype(BF16) for i in range(DEPTH)]
    wts["ple_gate"] = [ple_gate_w[i].astype(BF16) for i in range(DEPTH)]
    return wts


def kernel(x_prompt, x_sample, state_C, state_n, state_m, p_prompt, p_sample, w_a_in, b_a_gate, a_norm_w, w_a_out, w_b_in, b_b_in, b_norm_w, w_b_s, b_b_s, w_b_out, ln_g, ln_b, peer_wq, peer_keys, peer_u, peer_v, ple_w, ple_gate_w):
    bp, sp, d = x_prompt.shape
    bs, ss, _ = x_sample.shape
    n_a = state_C.shape[0]
    chunk_lens = sorted({min(sp, S_CHUNK), min(ss, S_CHUNK)})
    wts = _prep_weights(w_a_in, b_a_gate, a_norm_w, w_a_out, w_b_in, b_b_in, b_norm_w, w_b_s, b_b_s, w_b_out,
                        ln_g, ln_b, peer_wq, peer_keys, peer_u, peer_v, ple_w, ple_gate_w, chunk_lens)

    c0 = jnp.zeros((n_a, bp, M_HEADS, M_DK, M_DV), F32)
    n0 = jnp.zeros((n_a, bp, M_HEADS, M_DK), F32)
    m0 = jnp.zeros((n_a, bp, M_HEADS), F32)
    yp, pc, pn, pm, _ = _trunk(x_prompt.reshape(bp * sp, d), p_prompt.reshape(DEPTH, bp * sp, -1),
                               c0, n0, m0, batch=bp, seq=sp, wts=wts)
    ys, sc, sn, sm, sv = _trunk(x_sample.reshape(bs * ss, d), p_sample.reshape(DEPTH, bs * ss, -1),
                                state_C, state_n, state_m, batch=bs, seq=ss, wts=wts)
    return (yp.reshape(bp, sp, d), ys.reshape(bs, ss, d),
            jnp.stack(pc), jnp.stack(pn), jnp.stack(pm),
            jnp.stack(sc), jnp.stack(sn), jnp.stack(sm),
            jnp.stack([v.reshape(bs, ss, S_WIDTH) for v in sv]))
```

```python
import functools

import jax
import jax.numpy as jnp
from jax import lax
from jax.experimental import pallas as pl
from jax.experimental.pallas import tpu as pltpu

F32 = jnp.float32
BF16 = jnp.bfloat16

D_MODEL = 2048
DEPTH = 4
N_MIXERS = 2
M_HEADS = 8
M_DV = D_MODEL // M_HEADS
M_DK = M_DV // 2
HK = M_HEADS * M_DK
HV = M_HEADS * M_DV
A_MAIN = 2 * HK + 2 * HV
S_GROUPS = 8
S_CHUNK = 128
S_WIDTH = D_MODEL
S_DG = S_WIDTH // S_GROUPS
P_HEADS = 8
P_NKEYS = 128
P_NEXP = P_NKEYS * P_NKEYS
P_DH = 128
P_TOPK = 16
ALPHA = (2 * DEPTH) ** 0.25
LN_EPS = 1e-5

LANES = 128
BF16_ROWS = 16
VMEM_LIMIT = 56 * 1024 * 1024
NEG_INF = float("-inf")

MLSTM_CHUNK = 256
MLSTM_PAD = 16
PEER_TB = 512
PEER_TE = 512


def _params(sem):
    return pltpu.CompilerParams(dimension_semantics=sem, vmem_limit_bytes=VMEM_LIMIT)


def _layer_norm_rows(z, g, b):
    mu = jnp.mean(z, axis=-1, keepdims=True)
    zc = z - mu
    var = jnp.mean(zc * zc, axis=-1, keepdims=True)
    return zc * lax.rsqrt(var + LN_EPS) * g + b


def _mm_kernel(*refs, act, has_bias):
    if has_bias:
        x_ref, w_ref, b_ref, o_ref = refs
    else:
        x_ref, w_ref, o_ref = refs
    acc = jnp.dot(x_ref[...].astype(BF16), w_ref[...], preferred_element_type=F32)
    if has_bias:
        acc = acc + b_ref[...]
    if act == "gelu":
        acc = jax.nn.gelu(acc)
    o_ref[...] = acc.astype(o_ref.dtype)


def _mm(x, w, layer, bias=None, *, n=None, act=None, out_dtype=F32, tm=512, tn=512):
    m, k = x.shape
    n = w.shape[2] if n is None else n
    tn = min(tn, n)
    in_specs = [
        pl.BlockSpec((tm, k), lambda i, j: (i, 0)),
        pl.BlockSpec((None, k, tn), lambda i, j: (layer, 0, j)),
    ]
    args = [x, w]
    if bias is not None:
        in_specs.append(pl.BlockSpec((None, 1, tn), lambda i, j: (layer, 0, j)))
        args.append(bias)
    return pl.pallas_call(
        functools.partial(_mm_kernel, act=act, has_bias=bias is not None),
        grid=(m // tm, n // tn),
        in_specs=in_specs,
        out_specs=pl.BlockSpec((tm, tn), lambda i, j: (i, j)),
        out_shape=jax.ShapeDtypeStruct((m, n), out_dtype),
        compiler_params=_params(("parallel", "parallel")),
        name="mm",
    )(*args)


def _mm_ln_kernel(a_ref, w_ref, r_ref, g_ref, b_ref, of_ref, ob_ref):
    y = jnp.dot(a_ref[...].astype(BF16), w_ref[...], preferred_element_type=F32)
    o = _layer_norm_rows(ALPHA * r_ref[...] + y, g_ref[...], b_ref[...])
    of_ref[...] = o
    ob_ref[...] = o.astype(BF16)


def _ln_spec(n, li, s):
    return pl.BlockSpec((None, None, 1, n), lambda *_: (li, s, 0, 0))


def _mm_ln(a, w, layer, res, ln_g, ln_b, li, *, tm=256):
    m, k = a.shape
    n = w.shape[2]
    return pl.pallas_call(
        _mm_ln_kernel,
        grid=(m // tm,),
        in_specs=[
            pl.BlockSpec((tm, k), lambda i: (i, 0)),
            pl.BlockSpec((None, k, n), lambda i: (layer, 0, 0)),
            pl.BlockSpec((tm, n), lambda i: (i, 0)),
            _ln_spec(n, li, 0),
            _ln_spec(n, li, 0),
        ],
        out_specs=[pl.BlockSpec((tm, n), lambda i: (i, 0)), pl.BlockSpec((tm, n), lambda i: (i, 0))],
        out_shape=[jax.ShapeDtypeStruct((m, n), F32), jax.ShapeDtypeStruct((m, n), BF16)],
        compiler_params=_params(("parallel",)),
        name="mm_ln",
    )(a, w, res, ln_g, ln_b)


def _ple_kernel(xf_ref, xb_ref, gw_ref, p_ref, pw_ref, of_ref, ob_ref):
    gate = jax.nn.sigmoid(jnp.dot(xb_ref[...], gw_ref[...], preferred_element_type=F32))
    pe = jnp.dot(p_ref[...].astype(BF16), pw_ref[...], preferred_element_type=F32)
    o = xf_ref[...] + gate * pe
    of_ref[...] = o
    ob_ref[...] = o.astype(BF16)


def _ple(xf, xb, gw, p, pw, li, *, tm=256):
    m, n = xf.shape
    kp = p.shape[2]
    return pl.pallas_call(
        _ple_kernel,
        grid=(m // tm,),
        in_specs=[
            pl.BlockSpec((tm, n), lambda i: (i, 0)),
            pl.BlockSpec((tm, n), lambda i: (i, 0)),
            pl.BlockSpec((None, n, n), lambda i: (li, 0, 0)),
            pl.BlockSpec((None, tm, kp), lambda i: (li, i, 0)),
            pl.BlockSpec((None, kp, n), lambda i: (li, 0, 0)),
        ],
        out_specs=[pl.BlockSpec((tm, n), lambda i: (i, 0)), pl.BlockSpec((tm, n), lambda i: (i, 0))],
        out_shape=[jax.ShapeDtypeStruct((m, n), F32), jax.ShapeDtypeStruct((m, n), BF16)],
        compiler_params=_params(("parallel",)),
        name="ple",
    )(xf, xb, gw, p, pw)


def _group_causal(rr, cc, group):
    shift = group.bit_length() - 1
    same = (rr >> shift) == (cc >> shift)
    return jnp.where(rr >= cc, jnp.where(same, 1.0, 0.0), 0.0)


def _split3(x):
    hi = x.astype(BF16)
    r1 = x - hi.astype(F32)
    mid = r1.astype(BF16)
    lo = (r1 - mid.astype(F32)).astype(BF16)
    return hi, mid, lo


def _gate_prep_kernel(g_ref, gb_ref, x_ref, xt_ref, *, seg, valid):
    rows = g_ref.shape[0]
    g = g_ref[...] + gb_ref[...]
    lane = lax.broadcasted_iota(jnp.int32, (rows, LANES), 1)
    r = lax.broadcasted_iota(jnp.int32, (rows, LANES), 0)
    pad = (r & (seg - 1)) >= valid
    lf = jnp.where(pad, 0.0, jax.nn.log_sigmoid(g))
    ig = jnp.where(pad, NEG_INF, g)
    rr = lax.broadcasted_iota(jnp.int32, (rows, rows), 0)
    cc = lax.broadcasted_iota(jnp.int32, (rows, rows), 1)
    tri = _group_causal(rr, cc, seg).astype(BF16)
    hi, mid, lo = _split3(lf)
    bcum = (jnp.dot(tri, hi, preferred_element_type=F32)
            + jnp.dot(tri, mid, preferred_element_type=F32)
            + jnp.dot(tri, lo, preferred_element_type=F32))
    x = jnp.where(lane >= M_HEADS, bcum, ig)
    x_ref[...] = x
    xt_ref[...] = x.T


def _gate_prep(gates, gbias, layer, *, seg, valid, rows):
    m = gates.shape[0]
    return pl.pallas_call(
        functools.partial(_gate_prep_kernel, seg=seg, valid=valid),
        grid=(m // rows,),
        in_specs=[pl.BlockSpec((rows, LANES), lambda i: (i, 0)),
                  pl.BlockSpec((None, 1, LANES), lambda i: (layer, 0, 0))],
        out_specs=[pl.BlockSpec((rows, LANES), lambda i: (i, 0)), pl.BlockSpec((LANES, rows), lambda i: (0, i))],
        out_shape=[jax.ShapeDtypeStruct((m, LANES), F32), jax.ShapeDtypeStruct((LANES, m), F32)],
        compiler_params=_params(("parallel",)),
        name="gate_prep",
    )(gates, gbias)


def _mlstm_kernel(q_ref, k_ref, v_ref, o_ref, x_ref, irow_ref, brow_ref, nw_ref, c0_ref, n0_ref, m0_ref,
                  *rest, bb, seg, n_prev):
    hg_ref, c_ref, n_ref, m_ref, cs, ns, ms = rest[n_prev:]
    h = pl.program_id(1)
    c = pl.program_id(2)
    rows = bb * seg
    lane = lax.broadcasted_iota(jnp.int32, (rows, LANES), 1)

    @pl.when(c == 0)
    def _():
        cs[...] = c0_ref[...]
        ns[...] = n0_ref[...]
        ms[...] = jnp.broadcast_to(m0_ref[...], (bb, 1, LANES))

    x = x_ref[...]
    icol_all = jnp.sum(jnp.where(lane == h, x, 0.0), axis=1, keepdims=True)
    bcol_all = jnp.sum(jnp.where(lane == h + M_HEADS, x, 0.0), axis=1, keepdims=True)
    irow_all = irow_ref[...]
    brow_all = brow_ref[...]
    rr = lax.broadcasted_iota(jnp.int32, (seg, seg), 0)
    cc = lax.broadcasted_iota(jnp.int32, (seg, seg), 1)
    causal = rr >= cc
    scale = M_DK ** -0.5
    nw = nw_ref[...]

    for bi in range(bb):
        lo_r, hi_r = bi * seg, (bi + 1) * seg
        icol = icol_all[lo_r:hi_r]
        bcol = bcol_all[lo_r:hi_r]
        irow = irow_all[:, lo_r:hi_r]
        brow = brow_all[:, lo_r:hi_r]
        c_prev = cs[bi]
        n_prev_state = ns[bi]
        m_prev = ms[bi][:, :1]
        q = q_ref[lo_r:hi_r, :] * scale
        k = k_ref[lo_r:hi_r, :]
        v = v_ref[lo_r:hi_r, :]
        qb = q.astype(BF16)
        kb = k.astype(BF16)
        vb = v.astype(BF16)

        logd = jnp.where(causal, bcol - brow + irow, NEG_INF)
        inter = bcol + m_prev
        m_row = jnp.maximum(inter, jnp.max(logd, axis=1, keepdims=True))
        dmat = jnp.exp(logd - m_row)
        s = lax.dot_general(qb, kb, (((1,), (1,)), ((), ())), preferred_element_type=F32) * dmat
        w_inter = jnp.exp(inter - m_row)
        num = (w_inter * jnp.dot(qb, c_prev.astype(BF16), preferred_element_type=F32)
               + jnp.dot(s.astype(BF16), vb, preferred_element_type=F32))
        den = (w_inter * jnp.sum(q * n_prev_state, axis=1, keepdims=True)
               + jnp.sum(s, axis=1, keepdims=True))
        hh = num / jnp.maximum(jnp.abs(den), jnp.exp(-m_row))
        mu = jnp.mean(hh, axis=1, keepdims=True)
        hc = hh - mu
        var = jnp.mean(hc * hc, axis=1, keepdims=True)
        hn = hc * lax.rsqrt(var + LN_EPS) * nw
        hg_ref[lo_r:hi_r, :] = (hn * jax.nn.sigmoid(o_ref[lo_r:hi_r, :])).astype(hg_ref.dtype)

        m_new = m_row[seg - 1:seg, :]
        b_last = bcol[seg - 1:seg, :]
        g_state = jnp.exp(b_last + m_prev - m_new)
        g_tok = jnp.exp(b_last - bcol + icol - m_new)
        kg = g_tok * k
        c_new = g_state * c_prev + lax.dot_general(
            kg.astype(BF16), vb, (((0,), (0,)), ((), ())), preferred_element_type=F32)
        n_new = g_state * n_prev_state + jnp.sum(kg, axis=0, keepdims=True)
        m_new_b = jnp.broadcast_to(m_new, (1, LANES))
        cs[bi] = c_new
        ns[bi] = n_new
        ms[bi] = m_new_b
        c_ref[bi] = c_new
        n_ref[bi] = n_new
        m_ref[bi] = m_new_b


def _mlstm_scan(proj, gates, gbias, norm_w, c0, n0, m0, layer, prev, *, batch, nc, seg, valid, bb):
    rows = bb * seg
    nb = batch // bb
    n_a = c0.shape[0]
    kq, kv = M_DK, M_DV
    xg, xgt = _gate_prep(gates, gbias, layer, seg=seg, valid=valid, rows=rows)
    xgt = xgt[:2 * M_HEADS].reshape(2 * M_HEADS, 1, -1)
    row_map = lambda col: (lambda i, h, c: (i * nc + c, col(h)))
    st_map = lambda i, h, c: (layer, i, h, 0, 0)
    in_specs = [
        pl.BlockSpec((rows, kq), row_map(lambda h: h)),
        pl.BlockSpec((rows, kq), row_map(lambda h: HK // kq + h)),
        pl.BlockSpec((rows, kv), row_map(lambda h: 2 * HK // kv + h)),
        pl.BlockSpec((rows, kv), row_map(lambda h: (2 * HK + HV) // kv + h)),
        pl.BlockSpec((rows, LANES), lambda i, h, c: (i * nc + c, 0)),
        pl.BlockSpec((None, 1, rows), lambda i, h, c: (h, 0, i * nc + c)),
        pl.BlockSpec((None, 1, rows), lambda i, h, c: (h + M_HEADS, 0, i * nc + c)),
        pl.BlockSpec((None, 1, kv), lambda i, h, c: (layer, 0, h)),
        pl.BlockSpec((None, bb, None, kq, kv), st_map),
        pl.BlockSpec((None, bb, None, 1, kq), st_map),
        pl.BlockSpec((None, bb, None, 1, 1), st_map),
    ]
    args = [proj, proj, proj, proj, xg, xgt, xgt, norm_w, c0,
            n0.reshape(n_a, batch, M_HEADS, 1, kq), m0.reshape(n_a, batch, M_HEADS, 1, 1)]
    aliases = {}
    n_prev = 0
    if prev is not None:
        n_prev = len(prev)
        for t, arr in enumerate(prev):
            aliases[len(args)] = 1 + t
            in_specs.append(pl.BlockSpec(memory_space=pl.ANY))
            args.append(arr)
    hg, c_out, n_out, m_out = pl.pallas_call(
        functools.partial(_mlstm_kernel, bb=bb, seg=seg, n_prev=n_prev),
        grid=(nb, M_HEADS, nc),
        in_specs=in_specs,
        out_specs=[
            pl.BlockSpec((rows, kv), lambda i, h, c: (i * nc + c, h)),
            pl.BlockSpec((None, bb, None, kq, kv), st_map),
            pl.BlockSpec((None, bb, None, 1, kq), st_map),
            pl.BlockSpec((None, bb, None, 1, LANES), st_map),
        ],
        out_shape=[
            jax.ShapeDtypeStruct((batch * nc * seg, HV), BF16),
            jax.ShapeDtypeStruct((n_a, batch, M_HEADS, kq, kv), F32),
            jax.ShapeDtypeStruct((n_a, batch, M_HEADS, 1, kq), F32),
            jax.ShapeDtypeStruct((n_a, batch, M_HEADS, 1, LANES), F32),
        ],
        scratch_shapes=[
            pltpu.VMEM((bb, kq, kv), F32),
            pltpu.VMEM((bb, 1, kq), F32),
            pltpu.VMEM((bb, 1, LANES), F32),
        ],
        input_output_aliases=aliases,
        compiler_params=_params(("parallel", "parallel", "arbitrary")),
        name="mlstm_scan",
    )(*args)
    return hg, (c_out, n_out, m_out)


def _mix_kernel(h_ref, w_ref, bias_ref, vw_ref, um_ref, v_ref, *, lc):
    t = h_ref.shape[0]
    vraw = h_ref[:, S_WIDTH:]
    mu = jnp.mean(vraw, axis=-1, keepdims=True)
    vc = vraw - mu
    var = jnp.mean(vc * vc, axis=-1, keepdims=True)
    v = vc * lax.rsqrt(var + LN_EPS) * vw_ref[...]
    v_ref[...] = v
    vb = v.astype(BF16)
    rr = lax.broadcasted_iota(jnp.int32, (t, t), 0)
    cc = lax.broadcasted_iota(jnp.int32, (t, t), 1)
    keep = _group_causal(rr, cc, lc)
    for g in range(S_GROUPS):
        sl = slice(g * S_DG, (g + 1) * S_DG)
        wg = (w_ref[g] * keep).astype(BF16)
        mixed = jnp.dot(wg, vb[:, sl], preferred_element_type=F32) + bias_ref[:, g:g + 1]
        um_ref[:, sl] = (h_ref[:, sl] * mixed).astype(BF16)


def _mix(hdn, wmix, bias_t, vnorm_w, layer, *, lc):
    m = hdn.shape[0]
    t = S_CHUNK
    return pl.pallas_call(
        functools.partial(_mix_kernel, lc=lc),
        grid=(m // t,),
        in_specs=[
            pl.BlockSpec((t, 2 * S_WIDTH), lambda i: (i, 0)),
            pl.BlockSpec((None, S_GROUPS, t, t), lambda i: (layer, 0, 0, 0)),
            pl.BlockSpec((None, t, S_GROUPS), lambda i: (layer, 0, 0)),
            pl.BlockSpec((None, 1, S_WIDTH), lambda i: (layer, 0, 0)),
        ],
        out_specs=[pl.BlockSpec((t, S_WIDTH), lambda i: (i, 0)), pl.BlockSpec((t, S_WIDTH), lambda i: (i, 0))],
        out_shape=[jax.ShapeDtypeStruct((m, S_WIDTH), BF16), jax.ShapeDtypeStruct((m, S_WIDTH), F32)],
        compiler_params=_params(("parallel",)),
        name="chunk_mix",
    )(hdn, wmix, bias_t, vnorm_w)


def _route_kernel(xb_ref, wq_ref, keys_ref, r1_ref, e1_ref, nn_ref, c0_ref, q_s, sc_s, wk_s, rk_s, sv_s):
    tm = xb_ref.shape[0]
    q_s[...] = jnp.dot(xb_ref[...], wq_ref[...], preferred_element_type=F32)
    kidx = lax.broadcasted_iota(jnp.int32, (P_NKEYS, tm), 0).astype(F32)
    for h in range(P_HEADS):
        for c in range(2):
            hc = 2 * h + c
            qhc = q_s[:, hc * P_DH:(hc + 1) * P_DH].astype(BF16)
            sc = lax.dot_general(keys_ref[c], qhc, (((1,), (1,)), ((), ())), preferred_element_type=F32)
            sc_s[hc] = sc
            wk_s[c] = sc
            rk_s[hc] = jnp.full((P_NKEYS, tm), float(P_NKEYS), F32)

        def body(kk, carry, h=h):
            for c in range(2):
                hc = 2 * h + c
                s = wk_s[c]
                mx = jnp.max(s, axis=0, keepdims=True)
                idx = jnp.min(jnp.where(s == mx, kidx, float(P_NKEYS)), axis=0, keepdims=True)
                sel = kidx == idx
                wk_s[c] = jnp.where(sel, NEG_INF, s)
                rk_s[hc] = jnp.where(sel, kk.astype(F32), rk_s[hc])
                sv_s[c, kk, h:h + 1, :] = mx
            return carry

        lax.fori_loop(0, P_TOPK, body, 0)

    sv0 = [sv_s[0, k] for k in range(P_TOPK)]
    sv1 = [sv_s[1, k] for k in range(P_TOPK)]
    cnt = [jnp.zeros((P_HEADS, tm), F32) for _ in range(P_TOPK)]
    front = [sv0[k] + sv1[0] for k in range(P_TOPK)]
    top = front[0]
    z = jnp.zeros((P_HEADS, tm), F32)
    for r in range(P_TOPK):
        live = min(r + 1, P_TOPK)
        mx = front[0]
        for k in range(1, live):
            mx = jnp.maximum(mx, front[k])
        pick = jnp.full((P_HEADS, tm), float(P_TOPK), F32)
        for k in reversed(range(live)):
            pick = jnp.where(front[k] == mx, float(k), pick)
        z = z + jnp.exp(mx - top)
        hits = [pick == float(k) for k in range(live)]
        newcnt = jnp.zeros((P_HEADS, tm), F32)
        for k in range(live):
            cnt[k] = cnt[k] + jnp.where(hits[k], 1.0, 0.0)
            newcnt = jnp.where(hits[k], cnt[k], newcnt)
        nxt = jnp.full((P_HEADS, tm), NEG_INF, F32)
        for j in range(1, min(r + 2, P_TOPK)):
            nxt = jnp.where(newcnt == float(j), sv1[j], nxt)
        for k in range(live):
            front[k] = jnp.where(hits[k], sv0[k] + nxt, front[k])
    zinv = 1.0 / z

    for h in range(P_HEADS):
        rank0 = rk_s[2 * h]
        nn = jnp.zeros((P_NKEYS, tm), F32)
        for k in range(P_TOPK):
            nn = jnp.where(rank0 == float(k), cnt[k][h:h + 1, :], nn)
        nn_ref[h] = nn
        c0_ref[h] = jnp.exp(sc_s[2 * h] - sv0[0][h:h + 1, :]) * zinv[h:h + 1, :]
        r1_ref[h] = pltpu.bitcast(rk_s[2 * h + 1].astype(BF16), jnp.int32)
        e1_ref[h] = pltpu.bitcast(jnp.exp(sc_s[2 * h + 1] - sv1[0][h:h + 1, :]).astype(BF16), jnp.int32)


def _route(xb, wq, keys, li, *, tm=256):
    m, k = xb.shape
    tab = jax.ShapeDtypeStruct((P_HEADS, P_NKEYS, m), F32)
    tab16 = jax.ShapeDtypeStruct((P_HEADS, P_NKEYS // 2, m), jnp.int32)
    tab_spec = pl.BlockSpec((P_HEADS, P_NKEYS, tm), lambda i: (0, 0, i))
    tab16_spec = pl.BlockSpec((P_HEADS, P_NKEYS // 2, tm), lambda i: (0, 0, i))
    return pl.pallas_call(
        _route_kernel,
        grid=(m // tm,),
        in_specs=[
            pl.BlockSpec((tm, k), lambda i: (i, 0)),
            pl.BlockSpec((None, k, P_HEADS * 2 * P_DH), lambda i: (li, 0, 0)),
            pl.BlockSpec((None, 2, P_NKEYS, P_DH), lambda i: (li, 0, 0, 0)),
        ],
        out_specs=[tab16_spec, tab16_spec, tab_spec, tab_spec],
        out_shape=[tab16, tab16, tab, tab],
        scratch_shapes=[
            pltpu.VMEM((tm, P_HEADS * 2 * P_DH), F32),
            pltpu.VMEM((2 * P_HEADS, P_NKEYS, tm), F32),
            pltpu.VMEM((2, P_NKEYS, tm), F32),
            pltpu.VMEM((2 * P_HEADS, P_NKEYS, tm), F32),
            pltpu.VMEM((2, P_TOPK, P_HEADS, tm), F32),
        ],
        compiler_params=_params(("parallel",)),
        name="peer_route",
    )(xb, wq, keys)


def _peer_kernel(xb_ref, u_ref, v_ref, r1_ref, e1_ref, nn_ref, c0_ref, xf_ref, g_ref, b_ref,
                 of_ref, ob_ref, acc_s, st_s, stn_s, act_s, *, te):
    e = pl.program_id(1)
    last = pl.num_programs(1) - 1
    n_i1 = te // P_NKEYS
    tb = st_s.shape[1]

    @pl.when(e == 0)
    def _():
        acc_s[...] = jnp.zeros_like(acc_s)
        st_s[...] = jnp.zeros_like(st_s)

    stn_s[...] = lax.dot_general(u_ref[...], xb_ref[...], (((1,), (1,)), ((), ())),
                                 preferred_element_type=F32)
    base = jnp.maximum(e - 1, 0) * n_i1
    for j in range(n_i1):
        rows = slice(j * P_NKEYS, (j + 1) * P_NKEYS)
        for t in range(tb // LANES):
            cols = slice(t * LANES, (t + 1) * LANES)
            shape3 = (P_NKEYS // BF16_ROWS, BF16_ROWS, LANES)
            gate = jnp.zeros(shape3, BF16)
            zero = jnp.zeros(shape3, BF16)
            for h in range(P_HEADS):
                nn = nn_ref[h, pl.ds(base + j, 1), :][:, cols]
                c0 = c0_ref[h, pl.ds(base + j, 1), :][:, cols]
                nn = jnp.broadcast_to(nn, (BF16_ROWS, LANES)).astype(BF16)[None]
                c0 = jnp.broadcast_to(c0, (BF16_ROWS, LANES)).astype(BF16)[None]
                r1 = pltpu.bitcast(r1_ref[h, :, cols], BF16).reshape(shape3)
                e1 = pltpu.bitcast(e1_ref[h, :, cols], BF16).reshape(shape3)
                gate = gate + jnp.where(r1 < nn, e1, zero) * c0
            act = jax.nn.gelu(st_s[rows, cols]).astype(BF16).reshape(shape3) * gate
            act_s[rows, cols] = act.reshape(P_NKEYS, LANES)
    acc_s[...] += lax.dot_general(act_s[...], v_ref[...], (((0,), (0,)), ((), ())),
                                  preferred_element_type=F32)
    st_s[...] = stn_s[...]

    @pl.when(e == last)
    def _():
        o = _layer_norm_rows(ALPHA * xf_ref[...] + acc_s[...], g_ref[...], b_ref[...])
        of_ref[...] = o
        ob_ref[...] = o.astype(BF16)


def _peer(xb, xf, u, v, r1, e1, nn, c0, ln_g, ln_b, li, *, tb=PEER_TB, te=PEER_TE):
    m, d = xf.shape
    ne = P_NEXP // te
    once = pl.Buffered(1)
    tab_spec = pl.BlockSpec((P_HEADS, P_NKEYS, tb), lambda i, e: (0, 0, i), pipeline_mode=once)
    tab16_spec = pl.BlockSpec((P_HEADS, P_NKEYS // 2, tb), lambda i, e: (0, 0, i), pipeline_mode=once)
    return pl.pallas_call(
        functools.partial(_peer_kernel, te=te),
        grid=(m // tb, ne + 1),
        in_specs=[
            pl.BlockSpec((tb, d), lambda i, e: (i, 0), pipeline_mode=once),
            pl.BlockSpec((None, te, d), lambda i, e: (li, jnp.minimum(e, ne - 1), 0)),
            pl.BlockSpec((None, te, d), lambda i, e: (li, jnp.maximum(e - 1, 0), 0)),
            tab16_spec, tab16_spec, tab_spec, tab_spec,
            pl.BlockSpec((tb, d), lambda i, e: (i, 0), pipeline_mode=once),
            _ln_spec(d, li, 1),
            _ln_spec(d, li, 1),
        ],
        out_specs=[pl.BlockSpec((tb, d), lambda i, e: (i, 0)), pl.BlockSpec((tb, d), lambda i, e: (i, 0))],
        out_shape=[jax.ShapeDtypeStruct((m, d), F32), jax.ShapeDtypeStruct((m, d), BF16)],
        scratch_shapes=[
            pltpu.VMEM((tb, d), F32),
            pltpu.VMEM((te, tb), F32),
            pltpu.VMEM((te, tb), F32),
            pltpu.VMEM((te, tb), BF16),
        ],
        compiler_params=_params(("parallel", "arbitrary")),
        name="peer_dense",
    )(xb, u, v, r1, e1, nn, c0, xf, ln_g, ln_b)


def _trunk(x, p, c_in, n_in, m_in, *, batch, seq, wts):
    m = batch * seq
    xf = x
    xb = x.astype(BF16)
    decode = seq < MLSTM_PAD
    lc = min(seq, S_CHUNK)
    states = None
    vs = []
    for i in range(DEPTH):
        j = i // N_MIXERS
        if i % N_MIXERS == 0:
            if decode:
                seg, valid, nc, bb = MLSTM_PAD, seq, 1, 8
                xin = jnp.pad(xb.reshape(batch, seq, D_MODEL), ((0, 0), (0, seg - seq), (0, 0)))
                xin = xin.reshape(batch * seg, D_MODEL)
            else:
                seg, valid, nc, bb = MLSTM_CHUNK, MLSTM_CHUNK, seq // MLSTM_CHUNK, 1
                xin = xb
            proj = _mm(xin, wts["a_in"], j, n=A_MAIN)
            gates = _mm(xin, wts["a_gate"], j)
            hg, states = _mlstm_scan(proj, gates, wts["a_gbias"], wts["a_norm"], c_in, n_in, m_in, j, states,
                                     batch=batch, nc=nc, seg=seg, valid=valid, bb=bb)
            if decode:
                hg = hg.reshape(batch, seg, HV)[:, :seq].reshape(m, HV)
            w_out = wts["a_out"]
        else:
            hdn = _mm(xb, wts["b_in"], j, wts["b_in_bias"], act="gelu")
            hg, v = _mix(hdn, wts["b_mix"][lc], wts["b_mix_bias"][lc], wts["b_norm"], j, lc=lc)
            vs.append(v)
            w_out = wts["b_out"]
        x1f, x1b = _mm_ln(hg, w_out, j, xf, wts["ln_g"], wts["ln_b"], i)
        r1, e1, nn, c0 = _route(x1b, wts["peer_wq"], wts["peer_keys"], i)
        x2f, x2b = _peer(x1b, x1f, wts["peer_u"], wts["peer_v"], r1, e1, nn, c0, wts["ln_g"], wts["ln_b"], i)
        xf, xb = _ple(x2f, x2b, wts["ple_gate"], p, wts["ple_w"], i)
    c_out, n_out, m_out = states
    return xf, c_out, n_out[:, :, :, 0, :], m_out[:, :, :, 0, 0], vs


def _prep_weights(w_a_in, b_a_gate, a_norm_w, w_a_out, w_b_in, b_b_in, b_norm_w, w_b_s, b_b_s, w_b_out,
                  ln_g, ln_b, peer_wq, peer_keys, peer_u, peer_v, ple_w, ple_gate_w, chunk_lens):
    n_a = w_a_in.shape[0]
    n_b = w_b_in.shape[0]
    gpad = LANES - 2 * M_HEADS
    wts = {
        "a_in": w_a_in.astype(BF16),
        "a_gate": jnp.pad(w_a_in[:, :, A_MAIN:], ((0, 0), (0, 0), (0, gpad))).astype(BF16),
        "a_gbias": jnp.pad(b_a_gate, ((0, 0), (0, gpad))).reshape(n_a, 1, LANES).astype(F32),
        "a_norm": a_norm_w.reshape(n_a, 1, HV).astype(F32),
        "a_out": w_a_out.astype(BF16),
        "b_in": w_b_in.astype(BF16),
        "b_in_bias": b_b_in.reshape(n_b, 1, -1).astype(F32),
        "b_norm": b_norm_w.reshape(n_b, 1, S_WIDTH).astype(F32),
        "b_out": w_b_out.astype(BF16),
        "b_mix": {},
        "b_mix_bias": {},
        "ln_g": ln_g.reshape(DEPTH, 2, 1, D_MODEL).astype(F32),
        "ln_b": ln_b.reshape(DEPTH, 2, 1, D_MODEL).astype(F32),
        "peer_wq": peer_wq.astype(BF16),
        "peer_keys": peer_keys.astype(BF16),
        "peer_u": peer_u.astype(BF16),
        "peer_v": peer_v.astype(BF16),
        "ple_w": ple_w.astype(BF16),
        "ple_gate": ple_gate_w.astype(BF16),
    }
    for lc in chunk_lens:
        rep = S_CHUNK // lc
        if rep == 1:
            wts["b_mix"][lc] = w_b_s.astype(F32)
        else:
            wts["b_mix"][lc] = jnp.tile(w_b_s[:, :, :lc, :lc], (1, 1, rep, rep)).astype(F32)
        wts["b_mix_bias"][lc] = jnp.swapaxes(jnp.tile(b_b_s[:, :, :lc], (1, 1, rep)), 1, 2).astype(F32)
    return wts


def kernel(x_prompt, x_sample, state_C, state_n, state_m, p_prompt, p_sample, w_a_in, b_a_gate, a_norm_w, w_a_out, w_b_in, b_b_in, b_norm_w, w_b_s, b_b_s, w_b_out, ln_g, ln_b, peer_wq, peer_keys, peer_u, peer_v, ple_w, ple_gate_w):
    bp, sp, d = x_prompt.shape
    bs, ss, _ = x_sample.shape
    n_a = state_C.shape[0]
    chunk_lens = sorted({min(sp, S_CHUNK), min(ss, S_CHUNK)})
    wts = _prep_weights(w_a_in, b_a_gate, a_norm_w, w_a_out, w_b_in, b_b_in, b_norm_w, w_b_s, b_b_s, w_b_out,
                        ln_g, ln_b, peer_wq, peer_keys, peer_u, peer_v, ple_w, ple_gate_w, chunk_lens)

    c0 = jnp.zeros((n_a, bp, M_HEADS, M_DK, M_DV), F32)
    n0 = jnp.zeros((n_a, bp, M_HEADS, M_DK), F32)
    m0 = jnp.zeros((n_a, bp, M_HEADS), F32)
    yp, pc, pn, pm, _ = _trunk(x_prompt.reshape(bp * sp, d), p_prompt.reshape(DEPTH, bp * sp, -1),
                               c0, n0, m0, batch=bp, seq=sp, wts=wts)
    ys, sc, sn, sm, sv = _trunk(x_sample.reshape(bs * ss, d), p_sample.reshape(DEPTH, bs * ss, -1),
                                state_C, state_n, state_m, batch=bs, seq=ss, wts=wts)
    return (yp.reshape(bp, sp, d), ys.reshape(bs, ss, d), pc, pn, pm, sc, sn, sm,
            jnp.stack([v.reshape(bs, ss, S_WIDTH) for v in sv]))
```

```python
import functools

import jax
import jax.numpy as jnp
from jax import lax
from jax.experimental import pallas as pl
from jax.experimental.pallas import tpu as pltpu

F32 = jnp.float32
BF16 = jnp.bfloat16

D_MODEL = 2048
DEPTH = 4
N_MIXERS = 2
M_HEADS = 8
M_DV = D_MODEL // M_HEADS
M_DK = M_DV // 2
HK = M_HEADS * M_DK
HV = M_HEADS * M_DV
A_MAIN = 2 * HK + 2 * HV
S_GROUPS = 8
S_CHUNK = 128
S_WIDTH = D_MODEL
S_DG = S_WIDTH // S_GROUPS
P_HEADS = 8
P_NKEYS = 128
P_NEXP = P_NKEYS * P_NKEYS
P_DH = 128
P_TOPK = 16
ALPHA = (2 * DEPTH) ** 0.25
LN_EPS = 1e-5

LANES = 128
BF16_ROWS = 16
VMEM_LIMIT = 56 * 1024 * 1024
NEG_INF = float("-inf")

MLSTM_CHUNK = 256
MLSTM_PAD = 16
PEER_TB = 512
PEER_TE = 1024
MM_ROWS = 1024
RES_ROWS = 512


def _params(sem, flags=None):
    return pltpu.CompilerParams(dimension_semantics=sem, vmem_limit_bytes=VMEM_LIMIT, flags=flags)


def _layer_norm_rows(z, g, b):
    mu = jnp.mean(z, axis=-1, keepdims=True)
    zc = z - mu
    var = jnp.mean(zc * zc, axis=-1, keepdims=True)
    return zc * lax.rsqrt(var + LN_EPS) * g + b


def _mm_kernel(*refs, act, has_bias):
    if has_bias:
        x_ref, w_ref, b_ref, o_ref = refs
    else:
        x_ref, w_ref, o_ref = refs
    acc = jnp.dot(x_ref[...].astype(BF16), w_ref[...], preferred_element_type=F32)
    if has_bias:
        acc = acc + b_ref[...]
    if act == "gelu":
        acc = jax.nn.gelu(acc)
    o_ref[...] = acc.astype(o_ref.dtype)


def _mm(x, w, layer, bias=None, *, n=None, act=None, out_dtype=F32, tm=None, tn=512):
    m, k = x.shape
    n = w.shape[2] if n is None else n
    tn = min(tn, n)
    if tm is None:
        tm = MM_ROWS if m % MM_ROWS == 0 else m
    in_specs = [
        pl.BlockSpec((tm, k), lambda i, j: (i, 0)),
        pl.BlockSpec((None, k, tn), lambda i, j: (layer, 0, j)),
    ]
    args = [x, w]
    if bias is not None:
        in_specs.append(pl.BlockSpec((None, 1, tn), lambda i, j: (layer, 0, j)))
        args.append(bias)
    return pl.pallas_call(
        functools.partial(_mm_kernel, act=act, has_bias=bias is not None),
        grid=(m // tm, n // tn),
        in_specs=in_specs,
        out_specs=pl.BlockSpec((tm, tn), lambda i, j: (i, j)),
        out_shape=jax.ShapeDtypeStruct((m, n), out_dtype),
        compiler_params=_params(("parallel", "parallel")),
        name="mm",
    )(*args)


def _mm_ln_kernel(a_ref, w_ref, r_ref, g_ref, b_ref, of_ref, ob_ref):
    y = jnp.dot(a_ref[...].astype(BF16), w_ref[...], preferred_element_type=F32)
    o = _layer_norm_rows(ALPHA * r_ref[...] + y, g_ref[...], b_ref[...])
    of_ref[...] = o
    ob_ref[...] = o.astype(BF16)


def _ln_spec(n, li, s):
    return pl.BlockSpec((None, None, 1, n), lambda *_: (li, s, 0, 0))


def _mm_ln(a, w, layer, res, ln_g, ln_b, li, *, tm=RES_ROWS):
    m, k = a.shape
    n = w.shape[2]
    return pl.pallas_call(
        _mm_ln_kernel,
        grid=(m // tm,),
        in_specs=[
            pl.BlockSpec((tm, k), lambda i: (i, 0)),
            pl.BlockSpec((None, k, n), lambda i: (layer, 0, 0)),
            pl.BlockSpec((tm, n), lambda i: (i, 0)),
            _ln_spec(n, li, 0),
            _ln_spec(n, li, 0),
        ],
        out_specs=[pl.BlockSpec((tm, n), lambda i: (i, 0)), pl.BlockSpec((tm, n), lambda i: (i, 0))],
        out_shape=[jax.ShapeDtypeStruct((m, n), F32), jax.ShapeDtypeStruct((m, n), BF16)],
        compiler_params=_params(("parallel",)),
        name="mm_ln",
    )(a, w, res, ln_g, ln_b)


def _ple_kernel(xf_ref, xb_ref, gw_ref, p_ref, pw_ref, of_ref, ob_ref):
    gate = jax.nn.sigmoid(jnp.dot(xb_ref[...], gw_ref[...], preferred_element_type=F32))
    pe = jnp.dot(p_ref[...].astype(BF16), pw_ref[...], preferred_element_type=F32)
    o = xf_ref[...] + gate * pe
    of_ref[...] = o
    ob_ref[...] = o.astype(BF16)


def _ple(xf, xb, gw, p, pw, li, *, tm=RES_ROWS):
    m, n = xf.shape
    kp = p.shape[2]
    return pl.pallas_call(
        _ple_kernel,
        grid=(m // tm,),
        in_specs=[
            pl.BlockSpec((tm, n), lambda i: (i, 0)),
            pl.BlockSpec((tm, n), lambda i: (i, 0)),
            pl.BlockSpec((None, n, n), lambda i: (li, 0, 0)),
            pl.BlockSpec((None, tm, kp), lambda i: (li, i, 0)),
            pl.BlockSpec((None, kp, n), lambda i: (li, 0, 0)),
        ],
        out_specs=[pl.BlockSpec((tm, n), lambda i: (i, 0)), pl.BlockSpec((tm, n), lambda i: (i, 0))],
        out_shape=[jax.ShapeDtypeStruct((m, n), F32), jax.ShapeDtypeStruct((m, n), BF16)],
        compiler_params=_params(("parallel",)),
        name="ple",
    )(xf, xb, gw, p, pw)


def _group_causal(rr, cc, group):
    shift = group.bit_length() - 1
    same = (rr >> shift) == (cc >> shift)
    return jnp.where(rr >= cc, jnp.where(same, 1.0, 0.0), 0.0)


def _split3(x):
    hi = x.astype(BF16)
    r1 = x - hi.astype(F32)
    mid = r1.astype(BF16)
    lo = (r1 - mid.astype(F32)).astype(BF16)
    return hi, mid, lo


def _gate_prep_kernel(g_ref, gb_ref, x_ref, xt_ref, *, seg, valid):
    rows = g_ref.shape[0]
    g = g_ref[...] + gb_ref[...]
    lane = lax.broadcasted_iota(jnp.int32, (rows, LANES), 1)
    r = lax.broadcasted_iota(jnp.int32, (rows, LANES), 0)
    pad = (r & (seg - 1)) >= valid
    lf = jnp.where(pad, 0.0, jax.nn.log_sigmoid(g))
    ig = jnp.where(pad, NEG_INF, g)
    rr = lax.broadcasted_iota(jnp.int32, (rows, rows), 0)
    cc = lax.broadcasted_iota(jnp.int32, (rows, rows), 1)
    tri = _group_causal(rr, cc, seg).astype(BF16)
    hi, mid, lo = _split3(lf)
    bcum = (jnp.dot(tri, hi, preferred_element_type=F32)
            + jnp.dot(tri, mid, preferred_element_type=F32)
            + jnp.dot(tri, lo, preferred_element_type=F32))
    x = jnp.where(lane >= M_HEADS, bcum, ig)
    x_ref[...] = x
    xt_ref[...] = x.T


def _gate_prep(gates, gbias, layer, *, seg, valid, rows):
    m = gates.shape[0]
    return pl.pallas_call(
        functools.partial(_gate_prep_kernel, seg=seg, valid=valid),
        grid=(m // rows,),
        in_specs=[pl.BlockSpec((rows, LANES), lambda i: (i, 0)),
                  pl.BlockSpec((None, 1, LANES), lambda i: (layer, 0, 0))],
        out_specs=[pl.BlockSpec((rows, LANES), lambda i: (i, 0)), pl.BlockSpec((LANES, rows), lambda i: (0, i))],
        out_shape=[jax.ShapeDtypeStruct((m, LANES), F32), jax.ShapeDtypeStruct((LANES, m), F32)],
        compiler_params=_params(("parallel",)),
        name="gate_prep",
    )(gates, gbias)


def _mlstm_kernel(q_ref, k_ref, v_ref, o_ref, x_ref, irow_ref, brow_ref, nw_ref, c0_ref, n0_ref, m0_ref,
                  *rest, bb, seg, n_prev):
    hg_ref, c_ref, n_ref, m_ref, cs, ns, ms = rest[n_prev:]
    h = pl.program_id(1)
    c = pl.program_id(2)
    rows = bb * seg
    lane = lax.broadcasted_iota(jnp.int32, (rows, LANES), 1)

    @pl.when(c == 0)
    def _():
        cs[...] = c0_ref[...]
        ns[...] = n0_ref[...]
        ms[...] = jnp.broadcast_to(m0_ref[...], (bb, 1, LANES))

    x = x_ref[...]
    icol_all = jnp.sum(jnp.where(lane == h, x, 0.0), axis=1, keepdims=True)
    bcol_all = jnp.sum(jnp.where(lane == h + M_HEADS, x, 0.0), axis=1, keepdims=True)
    irow_all = irow_ref[...]
    brow_all = brow_ref[...]
    rr = lax.broadcasted_iota(jnp.int32, (seg, seg), 0)
    cc = lax.broadcasted_iota(jnp.int32, (seg, seg), 1)
    causal = rr >= cc
    scale = M_DK ** -0.5
    nw = nw_ref[...]

    for bi in range(bb):
        lo_r, hi_r = bi * seg, (bi + 1) * seg
        icol = icol_all[lo_r:hi_r]
        bcol = bcol_all[lo_r:hi_r]
        irow = irow_all[:, lo_r:hi_r]
        brow = brow_all[:, lo_r:hi_r]
        c_prev = cs[bi]
        n_prev_state = ns[bi]
        m_prev = ms[bi][:, :1]
        q = q_ref[lo_r:hi_r, :] * scale
        k = k_ref[lo_r:hi_r, :]
        v = v_ref[lo_r:hi_r, :]
        qb = q.astype(BF16)
        kb = k.astype(BF16)
        vb = v.astype(BF16)

        logd = jnp.where(causal, bcol - brow + irow, NEG_INF)
        inter = bcol + m_prev
        m_row = jnp.maximum(inter, jnp.max(logd, axis=1, keepdims=True))
        dmat = jnp.exp(logd - m_row)
        s = lax.dot_general(qb, kb, (((1,), (1,)), ((), ())), preferred_element_type=F32) * dmat
        w_inter = jnp.exp(inter - m_row)
        num = (w_inter * jnp.dot(qb, c_prev.astype(BF16), preferred_element_type=F32)
               + jnp.dot(s.astype(BF16), vb, preferred_element_type=F32))
        den = (w_inter * jnp.sum(q * n_prev_state, axis=1, keepdims=True)
               + jnp.sum(s, axis=1, keepdims=True))
        hh = num / jnp.maximum(jnp.abs(den), jnp.exp(-m_row))
        mu = jnp.mean(hh, axis=1, keepdims=True)
        hc = hh - mu
        var = jnp.mean(hc * hc, axis=1, keepdims=True)
        hn = hc * lax.rsqrt(var + LN_EPS) * nw
        hg_ref[lo_r:hi_r, :] = (hn * jax.nn.sigmoid(o_ref[lo_r:hi_r, :])).astype(hg_ref.dtype)

        m_new = m_row[seg - 1:seg, :]
        b_last = bcol[seg - 1:seg, :]
        g_state = jnp.exp(b_last + m_prev - m_new)
        g_tok = jnp.exp(b_last - bcol + icol - m_new)
        kg = g_tok * k
        c_new = g_state * c_prev + lax.dot_general(
            kg.astype(BF16), vb, (((0,), (0,)), ((), ())), preferred_element_type=F32)
        n_new = g_state * n_prev_state + jnp.sum(kg, axis=0, keepdims=True)
        m_new_b = jnp.broadcast_to(m_new, (1, LANES))
        cs[bi] = c_new
        ns[bi] = n_new
        ms[bi] = m_new_b
        c_ref[bi] = c_new
        n_ref[bi] = n_new
        m_ref[bi] = m_new_b


def _mlstm_scan(proj, gates, gbias, norm_w, c0, n0, m0, layer, prev, *, batch, nc, seg, valid, bb):
    rows = bb * seg
    nb = batch // bb
    n_a = c0.shape[0]
    kq, kv = M_DK, M_DV
    xg, xgt = _gate_prep(gates, gbias, layer, seg=seg, valid=valid, rows=rows)
    xgt = xgt[:2 * M_HEADS].reshape(2 * M_HEADS, 1, -1)
    row_map = lambda col: (lambda i, h, c: (i * nc + c, col(h)))
    st_map = lambda i, h, c: (layer, i, h, 0, 0)
    in_specs = [
        pl.BlockSpec((rows, kq), row_map(lambda h: h)),
        pl.BlockSpec((rows, kq), row_map(lambda h: HK // kq + h)),
        pl.BlockSpec((rows, kv), row_map(lambda h: 2 * HK // kv + h)),
        pl.BlockSpec((rows, kv), row_map(lambda h: (2 * HK + HV) // kv + h)),
        pl.BlockSpec((rows, LANES), lambda i, h, c: (i * nc + c, 0)),
        pl.BlockSpec((None, 1, rows), lambda i, h, c: (h, 0, i * nc + c)),
        pl.BlockSpec((None, 1, rows), lambda i, h, c: (h + M_HEADS, 0, i * nc + c)),
        pl.BlockSpec((None, 1, kv), lambda i, h, c: (layer, 0, h)),
        pl.BlockSpec((None, bb, None, kq, kv), st_map),
        pl.BlockSpec((None, bb, None, 1, kq), st_map),
        pl.BlockSpec((None, bb, None, 1, 1), st_map),
    ]
    args = [proj, proj, proj, proj, xg, xgt, xgt, norm_w, c0,
            n0.reshape(n_a, batch, M_HEADS, 1, kq), m0.reshape(n_a, batch, M_HEADS, 1, 1)]
    aliases = {}
    n_prev = 0
    if prev is not None:
        n_prev = len(prev)
        for t, arr in enumerate(prev):
            aliases[len(args)] = 1 + t
            in_specs.append(pl.BlockSpec(memory_space=pl.ANY))
            args.append(arr)
    hg, c_out, n_out, m_out = pl.pallas_call(
        functools.partial(_mlstm_kernel, bb=bb, seg=seg, n_prev=n_prev),
        grid=(nb, M_HEADS, nc),
        in_specs=in_specs,
        out_specs=[
            pl.BlockSpec((rows, kv), lambda i, h, c: (i * nc + c, h)),
            pl.BlockSpec((None, bb, None, kq, kv), st_map),
            pl.BlockSpec((None, bb, None, 1, kq), st_map),
            pl.BlockSpec((None, bb, None, 1, LANES), st_map),
        ],
        out_shape=[
            jax.ShapeDtypeStruct((batch * nc * seg, HV), BF16),
            jax.ShapeDtypeStruct((n_a, batch, M_HEADS, kq, kv), F32),
            jax.ShapeDtypeStruct((n_a, batch, M_HEADS, 1, kq), F32),
            jax.ShapeDtypeStruct((n_a, batch, M_HEADS, 1, LANES), F32),
        ],
        scratch_shapes=[
            pltpu.VMEM((bb, kq, kv), F32),
            pltpu.VMEM((bb, 1, kq), F32),
            pltpu.VMEM((bb, 1, LANES), F32),
        ],
        input_output_aliases=aliases,
        compiler_params=_params(("parallel", "parallel", "arbitrary")),
        name="mlstm_scan",
    )(*args)
    return hg, (c_out, n_out, m_out)


def _mix_kernel(h_ref, w_ref, bias_ref, vw_ref, um_ref, v_ref, *, lc):
    t = h_ref.shape[0]
    vraw = h_ref[:, S_WIDTH:]
    mu = jnp.mean(vraw, axis=-1, keepdims=True)
    vc = vraw - mu
    var = jnp.mean(vc * vc, axis=-1, keepdims=True)
    v = vc * lax.rsqrt(var + LN_EPS) * vw_ref[...]
    v_ref[...] = v
    vb = v.astype(BF16)
    rr = lax.broadcasted_iota(jnp.int32, (t, t), 0)
    cc = lax.broadcasted_iota(jnp.int32, (t, t), 1)
    keep = _group_causal(rr, cc, lc)
    for g in range(S_GROUPS):
        sl = slice(g * S_DG, (g + 1) * S_DG)
        wg = (w_ref[g] * keep).astype(BF16)
        mixed = jnp.dot(wg, vb[:, sl], preferred_element_type=F32) + bias_ref[:, g:g + 1]
        um_ref[:, sl] = (h_ref[:, sl] * mixed).astype(BF16)


def _mix(hdn, wmix, bias_t, vnorm_w, layer, *, lc):
    m = hdn.shape[0]
    t = S_CHUNK
    return pl.pallas_call(
        functools.partial(_mix_kernel, lc=lc),
        grid=(m // t,),
        in_specs=[
            pl.BlockSpec((t, 2 * S_WIDTH), lambda i: (i, 0)),
            pl.BlockSpec((None, S_GROUPS, t, t), lambda i: (layer, 0, 0, 0)),
            pl.BlockSpec((None, t, S_GROUPS), lambda i: (layer, 0, 0)),
            pl.BlockSpec((None, 1, S_WIDTH), lambda i: (layer, 0, 0)),
        ],
        out_specs=[pl.BlockSpec((t, S_WIDTH), lambda i: (i, 0)), pl.BlockSpec((t, S_WIDTH), lambda i: (i, 0))],
        out_shape=[jax.ShapeDtypeStruct((m, S_WIDTH), BF16), jax.ShapeDtypeStruct((m, S_WIDTH), F32)],
        compiler_params=_params(("parallel",)),
        name="chunk_mix",
    )(hdn, wmix, bias_t, vnorm_w)


def _route_kernel(xb_ref, wq_ref, keys_ref, r1_ref, e1_ref, nn_ref, c0_ref, q_s, sc_s, wk_s, rk_s, sv_s):
    tm = xb_ref.shape[0]
    q_s[...] = jnp.dot(xb_ref[...], wq_ref[...], preferred_element_type=F32)
    kidx = lax.broadcasted_iota(jnp.int32, (P_NKEYS, tm), 0).astype(F32)
    for h in range(P_HEADS):
        for c in range(2):
            hc = 2 * h + c
            qhc = q_s[:, hc * P_DH:(hc + 1) * P_DH].astype(BF16)
            sc = lax.dot_general(keys_ref[c], qhc, (((1,), (1,)), ((), ())), preferred_element_type=F32)
            sc_s[hc] = sc
            wk_s[c] = sc
            rk_s[hc] = jnp.full((P_NKEYS, tm), float(P_NKEYS), F32)

        def body(kk, carry, h=h):
            for c in range(2):
                hc = 2 * h + c
                s = wk_s[c]
                mx = jnp.max(s, axis=0, keepdims=True)
                idx = jnp.min(jnp.where(s == mx, kidx, float(P_NKEYS)), axis=0, keepdims=True)
                sel = kidx == idx
                wk_s[c] = jnp.where(sel, NEG_INF, s)
                rk_s[hc] = jnp.where(sel, kk.astype(F32), rk_s[hc])
                sv_s[c, kk, h:h + 1, :] = mx
            return carry

        lax.fori_loop(0, P_TOPK, body, 0)

    sv0 = [sv_s[0, k] for k in range(P_TOPK)]
    sv1 = [sv_s[1, k] for k in range(P_TOPK)]
    cnt = [jnp.zeros((P_HEADS, tm), F32) for _ in range(P_TOPK)]
    front = [sv0[k] + sv1[0] for k in range(P_TOPK)]
    top = front[0]
    z = jnp.zeros((P_HEADS, tm), F32)
    for r in range(P_TOPK):
        live = min(r + 1, P_TOPK)
        mx = front[0]
        for k in range(1, live):
            mx = jnp.maximum(mx, front[k])
        pick = jnp.full((P_HEADS, tm), float(P_TOPK), F32)
        for k in reversed(range(live)):
            pick = jnp.where(front[k] == mx, float(k), pick)
        z = z + jnp.exp(mx - top)
        hits = [pick == float(k) for k in range(live)]
        newcnt = jnp.zeros((P_HEADS, tm), F32)
        for k in range(live):
            cnt[k] = cnt[k] + jnp.where(hits[k], 1.0, 0.0)
            newcnt = jnp.where(hits[k], cnt[k], newcnt)
        nxt = jnp.full((P_HEADS, tm), NEG_INF, F32)
        for j in range(1, min(r + 2, P_TOPK)):
            nxt = jnp.where(newcnt == float(j), sv1[j], nxt)
        for k in range(live):
            front[k] = jnp.where(hits[k], sv0[k] + nxt, front[k])
    zinv = 1.0 / z

    for h in range(P_HEADS):
        rank0 = rk_s[2 * h]
        nn = jnp.zeros((P_NKEYS, tm), F32)
        for k in range(P_TOPK):
            nn = jnp.where(rank0 == float(k), cnt[k][h:h + 1, :], nn)
        nn_ref[h] = nn
        c0_ref[h] = jnp.exp(sc_s[2 * h] - sv0[0][h:h + 1, :]) * zinv[h:h + 1, :]
        r1_ref[h] = pltpu.bitcast(rk_s[2 * h + 1].astype(BF16), jnp.int32)
        e1_ref[h] = pltpu.bitcast(jnp.exp(sc_s[2 * h + 1] - sv1[0][h:h + 1, :]).astype(BF16), jnp.int32)


def _route(xb, wq, keys, li, *, tm=256):
    m, k = xb.shape
    tab = jax.ShapeDtypeStruct((P_HEADS, P_NKEYS, m), F32)
    tab16 = jax.ShapeDtypeStruct((P_HEADS, P_NKEYS // 2, m), jnp.int32)
    tab_spec = pl.BlockSpec((P_HEADS, P_NKEYS, tm), lambda i: (0, 0, i))
    tab16_spec = pl.BlockSpec((P_HEADS, P_NKEYS // 2, tm), lambda i: (0, 0, i))
    return pl.pallas_call(
        _route_kernel,
        grid=(m // tm,),
        in_specs=[
            pl.BlockSpec((tm, k), lambda i: (i, 0)),
            pl.BlockSpec((None, k, P_HEADS * 2 * P_DH), lambda i: (li, 0, 0)),
            pl.BlockSpec((None, 2, P_NKEYS, P_DH), lambda i: (li, 0, 0, 0)),
        ],
        out_specs=[tab16_spec, tab16_spec, tab_spec, tab_spec],
        out_shape=[tab16, tab16, tab, tab],
        scratch_shapes=[
            pltpu.VMEM((tm, P_HEADS * 2 * P_DH), F32),
            pltpu.VMEM((2 * P_HEADS, P_NKEYS, tm), F32),
            pltpu.VMEM((2, P_NKEYS, tm), F32),
            pltpu.VMEM((2 * P_HEADS, P_NKEYS, tm), F32),
            pltpu.VMEM((2, P_TOPK, P_HEADS, tm), F32),
        ],
        compiler_params=_params(("parallel",)),
        name="peer_route",
    )(xb, wq, keys)


def _gelu_tanh(x):
    c = 0.7978845608028654
    inner = x + x * x * x * 0.044715
    return x * 0.5 * (jnp.tanh(inner * c) + 1.0)


def _peer_kernel(xb_ref, u_ref, v_ref, r1_ref, e1_ref, nn_ref, c0_ref, xf_ref, g_ref, b_ref,
                 of_ref, ob_ref, acc_s, st_s, act_s, xt_s, *, te):
    e = pl.program_id(1)
    last = pl.num_programs(1) - 1
    n_i1 = te // P_NKEYS
    tb = st_s.shape[1]

    @pl.when(e == 0)
    def _():
        acc_s[...] = jnp.zeros_like(acc_s)
        st_s[...] = jnp.zeros_like(st_s)
        xt_s[...] = xb_ref[...].T

    base = jnp.maximum(e - 1, 0) * n_i1
    shape3 = (P_NKEYS // BF16_ROWS, BF16_ROWS, LANES)
    zero = jnp.zeros(shape3, BF16)
    for j in range(n_i1):
        rows = slice(j * P_NKEYS, (j + 1) * P_NKEYS)
        for t in range(tb // LANES):
            cols = slice(t * LANES, (t + 1) * LANES)
            gate = zero
            for h in range(P_HEADS):
                nn = nn_ref[h, pl.ds(base + j, 1), :][:, cols]
                c0 = c0_ref[h, pl.ds(base + j, 1), :][:, cols]
                nn = jnp.broadcast_to(nn, (BF16_ROWS, LANES)).astype(BF16)[None]
                c0 = jnp.broadcast_to(c0, (BF16_ROWS, LANES)).astype(BF16)[None]
                r1 = pltpu.bitcast(r1_ref[h, :, cols], BF16).reshape(shape3)
                e1 = pltpu.bitcast(e1_ref[h, :, cols], BF16).reshape(shape3)
                gate = gate + jnp.where(r1 < nn, e1, zero) * c0
            act = _gelu_tanh(st_s[rows, cols].astype(BF16).reshape(shape3)) * gate
            act_s[rows, cols] = act.reshape(P_NKEYS, LANES)
    acc_s[...] += lax.dot_general(act_s[...], v_ref[...], (((0,), (0,)), ((), ())),
                                  preferred_element_type=F32)
    st_s[...] = jnp.dot(u_ref[...], xt_s[...], preferred_element_type=F32)

    @pl.when(e == last)
    def _():
        o = _layer_norm_rows(ALPHA * xf_ref[...] + acc_s[...], g_ref[...], b_ref[...])
        of_ref[...] = o
        ob_ref[...] = o.astype(BF16)


def _peer(xb, xf, u, v, r1, e1, nn, c0, ln_g, ln_b, li, *, tb=PEER_TB, te=PEER_TE):
    m, d = xf.shape
    ne = P_NEXP // te
    once = pl.Buffered(1)
    tab_spec = pl.BlockSpec((P_HEADS, P_NKEYS, tb), lambda i, e: (0, 0, i), pipeline_mode=once)
    tab16_spec = pl.BlockSpec((P_HEADS, P_NKEYS // 2, tb), lambda i, e: (0, 0, i), pipeline_mode=once)
    return pl.pallas_call(
        functools.partial(_peer_kernel, te=te),
        grid=(m // tb, ne + 1),
        in_specs=[
            pl.BlockSpec((tb, d), lambda i, e: (i, 0), pipeline_mode=once),
            pl.BlockSpec((None, te, d), lambda i, e: (li, jnp.minimum(e, ne - 1), 0)),
            pl.BlockSpec((None, te, d), lambda i, e: (li, jnp.maximum(e - 1, 0), 0)),
            tab16_spec, tab16_spec, tab_spec, tab_spec,
            pl.BlockSpec((tb, d), lambda i, e: (i, 0), pipeline_mode=once),
            _ln_spec(d, li, 1),
            _ln_spec(d, li, 1),
        ],
        out_specs=[pl.BlockSpec((tb, d), lambda i, e: (i, 0)), pl.BlockSpec((tb, d), lambda i, e: (i, 0))],
        out_shape=[jax.ShapeDtypeStruct((m, d), F32), jax.ShapeDtypeStruct((m, d), BF16)],
        scratch_shapes=[
            pltpu.VMEM((tb, d), F32),
            pltpu.VMEM((te, tb), F32),
            pltpu.VMEM((te, tb), BF16),
            pltpu.VMEM((d, tb), BF16),
        ],
        compiler_params=_params(("parallel", "arbitrary")),
        name="peer_dense",
    )(xb, u, v, r1, e1, nn, c0, xf, ln_g, ln_b)


def _trunk(x, p, c_in, n_in, m_in, *, batch, seq, wts):
    m = batch * seq
    xf = x
    xb = x.astype(BF16)
    decode = seq < MLSTM_PAD
    lc = min(seq, S_CHUNK)
    states = None
    vs = []
    for i in range(DEPTH):
        j = i // N_MIXERS
        if i % N_MIXERS == 0:
            if decode:
                seg, valid, nc, bb = MLSTM_PAD, seq, 1, 8
                xin = jnp.pad(xb.reshape(batch, seq, D_MODEL), ((0, 0), (0, seg - seq), (0, 0)))
                xin = xin.reshape(batch * seg, D_MODEL)
            else:
                seg, valid, nc, bb = MLSTM_CHUNK, MLSTM_CHUNK, seq // MLSTM_CHUNK, 1
                xin = xb
            proj = _mm(xin, wts["a_in"], j, n=A_MAIN)
            gates = _mm(xin, wts["a_gate"], j)
            hg, states = _mlstm_scan(proj, gates, wts["a_gbias"], wts["a_norm"], c_in, n_in, m_in, j, states,
                                     batch=batch, nc=nc, seg=seg, valid=valid, bb=bb)
            if decode:
                hg = hg.reshape(batch, seg, HV)[:, :seq].reshape(m, HV)
            w_out = wts["a_out"]
        else:
            hdn = _mm(xb, wts["b_in"], j, wts["b_in_bias"], act="gelu")
            hg, v = _mix(hdn, wts["b_mix"][lc], wts["b_mix_bias"][lc], wts["b_norm"], j, lc=lc)
            vs.append(v)
            w_out = wts["b_out"]
        x1f, x1b = _mm_ln(hg, w_out, j, xf, wts["ln_g"], wts["ln_b"], i)
        r1, e1, nn, c0 = _route(x1b, wts["peer_wq"], wts["peer_keys"], i)
        x2f, x2b = _peer(x1b, x1f, wts["peer_u"], wts["peer_v"], r1, e1, nn, c0, wts["ln_g"], wts["ln_b"], i)
        xf, xb = _ple(x2f, x2b, wts["ple_gate"], p, wts["ple_w"], i)
    c_out, n_out, m_out = states
    return xf, c_out, n_out[:, :, :, 0, :], m_out[:, :, :, 0, 0], vs


def _prep_weights(w_a_in, b_a_gate, a_norm_w, w_a_out, w_b_in, b_b_in, b_norm_w, w_b_s, b_b_s, w_b_out,
                  ln_g, ln_b, peer_wq, peer_keys, peer_u, peer_v, ple_w, ple_gate_w, chunk_lens):
    n_a = w_a_in.shape[0]
    n_b = w_b_in.shape[0]
    gpad = LANES - 2 * M_HEADS
    wts = {
        "a_in": w_a_in.astype(BF16),
        "a_gate": jnp.pad(w_a_in[:, :, A_MAIN:], ((0, 0), (0, 0), (0, gpad))).astype(BF16),
        "a_gbias": jnp.pad(b_a_gate, ((0, 0), (0, gpad))).reshape(n_a, 1, LANES).astype(F32),
        "a_norm": a_norm_w.reshape(n_a, 1, HV).astype(F32),
        "a_out": w_a_out.astype(BF16),
        "b_in": w_b_in.astype(BF16),
        "b_in_bias": b_b_in.reshape(n_b, 1, -1).astype(F32),
        "b_norm": b_norm_w.reshape(n_b, 1, S_WIDTH).astype(F32),
        "b_out": w_b_out.astype(BF16),
        "b_mix": {},
        "b_mix_bias": {},
        "ln_g": ln_g.reshape(DEPTH, 2, 1, D_MODEL).astype(F32),
        "ln_b": ln_b.reshape(DEPTH, 2, 1, D_MODEL).astype(F32),
        "peer_wq": peer_wq.astype(BF16),
        "peer_keys": peer_keys.astype(BF16),
        "peer_u": peer_u.astype(BF16),
        "peer_v": peer_v.astype(BF16),
        "ple_w": ple_w.astype(BF16),
        "ple_gate": ple_gate_w.astype(BF16),
    }
    for lc in chunk_lens:
        rep = S_CHUNK // lc
        if rep == 1:
            wts["b_mix"][lc] = w_b_s.astype(F32)
        else:
            wts["b_mix"][lc] = jnp.tile(w_b_s[:, :, :lc, :lc], (1, 1, rep, rep)).astype(F32)
        wts["b_mix_bias"][lc] = jnp.swapaxes(jnp.tile(b_b_s[:, :, :lc], (1, 1, rep)), 1, 2).astype(F32)
    return wts


def kernel(x_prompt, x_sample, state_C, state_n, state_m, p_prompt, p_sample, w_a_in, b_a_gate, a_norm_w, w_a_out, w_b_in, b_b_in, b_norm_w, w_b_s, b_b_s, w_b_out, ln_g, ln_b, peer_wq, peer_keys, peer_u, peer_v, ple_w, ple_gate_w):
    bp, sp, d = x_prompt.shape
    bs, ss, _ = x_sample.shape
    n_a = state_C.shape[0]
    chunk_lens = sorted({min(sp, S_CHUNK), min(ss, S_CHUNK)})
    wts = _prep_weights(w_a_in, b_a_gate, a_norm_w, w_a_out, w_b_in, b_b_in, b_norm_w, w_b_s, b_b_s, w_b_out,
                        ln_g, ln_b, peer_wq, peer_keys, peer_u, peer_v, ple_w, ple_gate_w, chunk_lens)

    c0 = jnp.zeros((n_a, bp, M_HEADS, M_DK, M_DV), F32)
    n0 = jnp.zeros((n_a, bp, M_HEADS, M_DK), F32)
    m0 = jnp.zeros((n_a, bp, M_HEADS), F32)
    yp, pc, pn, pm, _ = _trunk(x_prompt.reshape(bp * sp, d), p_prompt.reshape(DEPTH, bp * sp, -1),
                               c0, n0, m0, batch=bp, seq=sp, wts=wts)
    ys, sc, sn, sm, sv = _trunk(x_sample.reshape(bs * ss, d), p_sample.reshape(DEPTH, bs * ss, -1),
                                state_C, state_n, state_m, batch=bs, seq=ss, wts=wts)
    return (yp.reshape(bp, sp, d), ys.reshape(bs, ss, d), pc, pn, pm, sc, sn, sm,
            jnp.stack([v.reshape(bs, ss, S_WIDTH) for v in sv]))
```

```python
import functools

import jax
import jax.numpy as jnp
from jax import lax
from jax.experimental import pallas as pl
from jax.experimental.pallas import tpu as pltpu

F32 = jnp.float32
BF16 = jnp.bfloat16

D_MODEL = 2048
DEPTH = 4
N_MIXERS = 2
M_HEADS = 8
M_DV = D_MODEL // M_HEADS
M_DK = M_DV // 2
HK = M_HEADS * M_DK
HV = M_HEADS * M_DV
A_MAIN = 2 * HK + 2 * HV
S_GROUPS = 8
S_CHUNK = 128
S_WIDTH = D_MODEL
S_DG = S_WIDTH // S_GROUPS
P_HEADS = 8
P_NKEYS = 128
P_NEXP = P_NKEYS * P_NKEYS
P_DH = 128
P_TOPK = 16
ALPHA = (2 * DEPTH) ** 0.25
LN_EPS = 1e-5

LANES = 128
BF16_ROWS = 16
VMEM_LIMIT = 56 * 1024 * 1024
NEG_INF = float("-inf")

MLSTM_CHUNK = 256
MLSTM_PAD = 16
PEER_TB = 512
PEER_TE = 1024
MM_ROWS = 1024
RES_ROWS = 512


def _params(sem, flags=None):
    return pltpu.CompilerParams(dimension_semantics=sem, vmem_limit_bytes=VMEM_LIMIT, flags=flags)


def _layer_norm_rows(z, g, b):
    mu = jnp.mean(z, axis=-1, keepdims=True)
    zc = z - mu
    var = jnp.mean(zc * zc, axis=-1, keepdims=True)
    return zc * lax.rsqrt(var + LN_EPS) * g + b


def _mm_kernel(*refs, act, has_bias):
    if has_bias:
        x_ref, w_ref, b_ref, o_ref = refs
    else:
        x_ref, w_ref, o_ref = refs
    acc = jnp.dot(x_ref[...].astype(BF16), w_ref[...], preferred_element_type=F32)
    if has_bias:
        acc = acc + b_ref[...]
    if act == "gelu":
        acc = jax.nn.gelu(acc)
    o_ref[...] = acc.astype(o_ref.dtype)


def _mm(x, w, layer, bias=None, *, n=None, act=None, out_dtype=F32, tm=None, tn=512):
    m, k = x.shape
    n = w.shape[2] if n is None else n
    tn = min(tn, n)
    if tm is None:
        tm = MM_ROWS if m % MM_ROWS == 0 else m
    in_specs = [
        pl.BlockSpec((tm, k), lambda i, j: (i, 0)),
        pl.BlockSpec((None, k, tn), lambda i, j: (layer, 0, j)),
    ]
    args = [x, w]
    if bias is not None:
        in_specs.append(pl.BlockSpec((None, 1, tn), lambda i, j: (layer, 0, j)))
        args.append(bias)
    return pl.pallas_call(
        functools.partial(_mm_kernel, act=act, has_bias=bias is not None),
        grid=(m // tm, n // tn),
        in_specs=in_specs,
        out_specs=pl.BlockSpec((tm, tn), lambda i, j: (i, j)),
        out_shape=jax.ShapeDtypeStruct((m, n), out_dtype),
        compiler_params=_params(("parallel", "parallel")),
        name="mm",
    )(*args)


def _mm_ln_kernel(a_ref, w_ref, r_ref, g_ref, b_ref, of_ref, ob_ref):
    y = jnp.dot(a_ref[...].astype(BF16), w_ref[...], preferred_element_type=F32)
    o = _layer_norm_rows(ALPHA * r_ref[...] + y, g_ref[...], b_ref[...])
    of_ref[...] = o
    ob_ref[...] = o.astype(BF16)


def _ln_spec(n, li, s):
    return pl.BlockSpec((None, None, 1, n), lambda *_: (li, s, 0, 0))


def _mm_ln(a, w, layer, res, ln_g, ln_b, li, *, tm=RES_ROWS):
    m, k = a.shape
    n = w.shape[2]
    return pl.pallas_call(
        _mm_ln_kernel,
        grid=(m // tm,),
        in_specs=[
            pl.BlockSpec((tm, k), lambda i: (i, 0)),
            pl.BlockSpec((None, k, n), lambda i: (layer, 0, 0)),
            pl.BlockSpec((tm, n), lambda i: (i, 0)),
            _ln_spec(n, li, 0),
            _ln_spec(n, li, 0),
        ],
        out_specs=[pl.BlockSpec((tm, n), lambda i: (i, 0)), pl.BlockSpec((tm, n), lambda i: (i, 0))],
        out_shape=[jax.ShapeDtypeStruct((m, n), F32), jax.ShapeDtypeStruct((m, n), BF16)],
        compiler_params=_params(("parallel",)),
        name="mm_ln",
    )(a, w, res, ln_g, ln_b)


def _ple_kernel(xf_ref, xb_ref, gw_ref, p_ref, pw_ref, of_ref, ob_ref):
    gate = jax.nn.sigmoid(jnp.dot(xb_ref[...], gw_ref[...], preferred_element_type=F32))
    pe = jnp.dot(p_ref[...].astype(BF16), pw_ref[...], preferred_element_type=F32)
    o = xf_ref[...] + gate * pe
    of_ref[...] = o
    ob_ref[...] = o.astype(BF16)


def _ple(xf, xb, gw, p, pw, li, *, tm=RES_ROWS):
    m, n = xf.shape
    kp = p.shape[2]
    return pl.pallas_call(
        _ple_kernel,
        grid=(m // tm,),
        in_specs=[
            pl.BlockSpec((tm, n), lambda i: (i, 0)),
            pl.BlockSpec((tm, n), lambda i: (i, 0)),
            pl.BlockSpec((None, n, n), lambda i: (li, 0, 0)),
            pl.BlockSpec((None, tm, kp), lambda i: (li, i, 0)),
            pl.BlockSpec((None, kp, n), lambda i: (li, 0, 0)),
        ],
        out_specs=[pl.BlockSpec((tm, n), lambda i: (i, 0)), pl.BlockSpec((tm, n), lambda i: (i, 0))],
        out_shape=[jax.ShapeDtypeStruct((m, n), F32), jax.ShapeDtypeStruct((m, n), BF16)],
        compiler_params=_params(("parallel",)),
        name="ple",
    )(xf, xb, gw, p, pw)


def _group_causal(rr, cc, group):
    shift = group.bit_length() - 1
    same = (rr >> shift) == (cc >> shift)
    return jnp.where(rr >= cc, jnp.where(same, 1.0, 0.0), 0.0)


def _split3(x):
    hi = x.astype(BF16)
    r1 = x - hi.astype(F32)
    mid = r1.astype(BF16)
    lo = (r1 - mid.astype(F32)).astype(BF16)
    return hi, mid, lo


def _gate_prep_kernel(g_ref, gb_ref, x_ref, xt_ref, *, seg, valid):
    rows = g_ref.shape[0]
    g = g_ref[...] + gb_ref[...]
    lane = lax.broadcasted_iota(jnp.int32, (rows, LANES), 1)
    r = lax.broadcasted_iota(jnp.int32, (rows, LANES), 0)
    pad = (r & (seg - 1)) >= valid
    lf = jnp.where(pad, 0.0, jax.nn.log_sigmoid(g))
    ig = jnp.where(pad, NEG_INF, g)
    rr = lax.broadcasted_iota(jnp.int32, (rows, rows), 0)
    cc = lax.broadcasted_iota(jnp.int32, (rows, rows), 1)
    tri = _group_causal(rr, cc, seg).astype(BF16)
    hi, mid, lo = _split3(lf)
    bcum = (jnp.dot(tri, hi, preferred_element_type=F32)
            + jnp.dot(tri, mid, preferred_element_type=F32)
            + jnp.dot(tri, lo, preferred_element_type=F32))
    x = jnp.where(lane >= M_HEADS, bcum, ig)
    x_ref[...] = x
    xt_ref[...] = x.T


def _gate_prep(gates, gbias, layer, *, seg, valid, rows):
    m = gates.shape[0]
    return pl.pallas_call(
        functools.partial(_gate_prep_kernel, seg=seg, valid=valid),
        grid=(m // rows,),
        in_specs=[pl.BlockSpec((rows, LANES), lambda i: (i, 0)),
                  pl.BlockSpec((None, 1, LANES), lambda i: (layer, 0, 0))],
        out_specs=[pl.BlockSpec((rows, LANES), lambda i: (i, 0)), pl.BlockSpec((LANES, rows), lambda i: (0, i))],
        out_shape=[jax.ShapeDtypeStruct((m, LANES), F32), jax.ShapeDtypeStruct((LANES, m), F32)],
        compiler_params=_params(("parallel",)),
        name="gate_prep",
    )(gates, gbias)


def _mlstm_kernel(q_ref, k_ref, v_ref, o_ref, x_ref, irow_ref, brow_ref, nw_ref, c0_ref, n0_ref, m0_ref,
                  *rest, bb, seg, n_prev):
    hg_ref, c_ref, n_ref, m_ref, cs, ns, ms = rest[n_prev:]
    h = pl.program_id(1)
    c = pl.program_id(2)
    rows = bb * seg
    lane = lax.broadcasted_iota(jnp.int32, (rows, LANES), 1)

    @pl.when(c == 0)
    def _():
        cs[...] = c0_ref[...]
        ns[...] = n0_ref[...]
        ms[...] = jnp.broadcast_to(m0_ref[...], (bb, 1, LANES))

    x = x_ref[...]
    icol = jnp.sum(jnp.where(lane == h, x, 0.0), axis=1, keepdims=True)
    bcol = jnp.sum(jnp.where(lane == h + M_HEADS, x, 0.0), axis=1, keepdims=True)
    irow = irow_ref[...]
    brow = brow_ref[...]
    rr = lax.broadcasted_iota(jnp.int32, (rows, rows), 0)
    cc = lax.broadcasted_iota(jnp.int32, (rows, rows), 1)
    visible = _group_causal(rr, cc, seg) > 0.0
    seg_id = lax.broadcasted_iota(jnp.int32, (rows, 1), 0) >> (seg.bit_length() - 1)

    def per_row(vals):
        out = vals[0]
        for bi in range(1, bb):
            out = jnp.where(seg_id == bi, vals[bi], out)
        return out

    ends = [(bi + 1) * seg - 1 for bi in range(bb)]
    m_prev_b = [ms[bi][:, :1] for bi in range(bb)]
    b_last_b = [bcol[r:r + 1, :] for r in ends]
    m_prev = per_row(m_prev_b)
    b_last = per_row(b_last_b)
    n_rows = per_row([ns[bi] for bi in range(bb)])

    q = q_ref[...] * (M_DK ** -0.5)
    k = k_ref[...]
    qb = q.astype(BF16)
    kb = k.astype(BF16)
    vb = v_ref[...].astype(BF16)

    logd = jnp.where(visible, bcol - brow + irow, NEG_INF)
    inter = bcol + m_prev
    m_row = jnp.maximum(inter, jnp.max(logd, axis=1, keepdims=True))
    dmat = jnp.exp(logd - m_row)
    s = lax.dot_general(qb, kb, (((1,), (1,)), ((), ())), preferred_element_type=F32) * dmat
    w_inter = jnp.exp(inter - m_row)
    qc = [jnp.dot(qb[bi * seg:(bi + 1) * seg], cs[bi].astype(BF16), preferred_element_type=F32)
          for bi in range(bb)]
    qc = qc[0] if bb == 1 else jnp.concatenate(qc, axis=0)
    num = w_inter * qc + jnp.dot(s.astype(BF16), vb, preferred_element_type=F32)
    den = w_inter * jnp.sum(q * n_rows, axis=1, keepdims=True) + jnp.sum(s, axis=1, keepdims=True)
    hh = num / jnp.maximum(jnp.abs(den), jnp.exp(-m_row))
    mu = jnp.mean(hh, axis=1, keepdims=True)
    hc = hh - mu
    var = jnp.mean(hc * hc, axis=1, keepdims=True)
    hn = hc * lax.rsqrt(var + LN_EPS) * nw_ref[...]
    hg_ref[...] = (hn * jax.nn.sigmoid(o_ref[...])).astype(hg_ref.dtype)

    m_new_b = [m_row[r:r + 1, :] for r in ends]
    g_tok = jnp.exp(b_last - bcol + icol - per_row(m_new_b))
    kg = g_tok * k
    kgb = kg.astype(BF16)
    for bi in range(bb):
        lo_r, hi_r = bi * seg, (bi + 1) * seg
        g_state = jnp.exp(b_last_b[bi] + m_prev_b[bi] - m_new_b[bi])
        c_new = g_state * cs[bi] + lax.dot_general(
            kgb[lo_r:hi_r], vb[lo_r:hi_r], (((0,), (0,)), ((), ())), preferred_element_type=F32)
        n_new = g_state * ns[bi] + jnp.sum(kg[lo_r:hi_r], axis=0, keepdims=True)
        m_new = jnp.broadcast_to(m_new_b[bi], (1, LANES))
        cs[bi] = c_new
        ns[bi] = n_new
        ms[bi] = m_new
        c_ref[bi] = c_new
        n_ref[bi] = n_new
        m_ref[bi] = m_new


def _mlstm_scan(proj, gates, gbias, norm_w, c0, n0, m0, layer, prev, *, batch, nc, seg, valid, bb):
    rows = bb * seg
    nb = batch // bb
    n_a = c0.shape[0]
    kq, kv = M_DK, M_DV
    xg, xgt = _gate_prep(gates, gbias, layer, seg=seg, valid=valid, rows=rows)
    xgt = xgt[:2 * M_HEADS].reshape(2 * M_HEADS, 1, -1)
    row_map = lambda col: (lambda i, h, c: (i * nc + c, col(h)))
    st_map = lambda i, h, c: (layer, i, h, 0, 0)
    in_specs = [
        pl.BlockSpec((rows, kq), row_map(lambda h: h)),
        pl.BlockSpec((rows, kq), row_map(lambda h: HK // kq + h)),
        pl.BlockSpec((rows, kv), row_map(lambda h: 2 * HK // kv + h)),
        pl.BlockSpec((rows, kv), row_map(lambda h: (2 * HK + HV) // kv + h)),
        pl.BlockSpec((rows, LANES), lambda i, h, c: (i * nc + c, 0)),
        pl.BlockSpec((None, 1, rows), lambda i, h, c: (h, 0, i * nc + c)),
        pl.BlockSpec((None, 1, rows), lambda i, h, c: (h + M_HEADS, 0, i * nc + c)),
        pl.BlockSpec((None, 1, kv), lambda i, h, c: (layer, 0, h)),
        pl.BlockSpec((None, bb, None, kq, kv), st_map),
        pl.BlockSpec((None, bb, None, 1, kq), st_map),
        pl.BlockSpec((None, bb, None, 1, 1), st_map),
    ]
    args = [proj, proj, proj, proj, xg, xgt, xgt, norm_w, c0,
            n0.reshape(n_a, batch, M_HEADS, 1, kq), m0.reshape(n_a, batch, M_HEADS, 1, 1)]
    aliases = {}
    n_prev = 0
    if prev is not None:
        n_prev = len(prev)
        for t, arr in enumerate(prev):
            aliases[len(args)] = 1 + t
            in_specs.append(pl.BlockSpec(memory_space=pl.ANY))
            args.append(arr)
    hg, c_out, n_out, m_out = pl.pallas_call(
        functools.partial(_mlstm_kernel, bb=bb, seg=seg, n_prev=n_prev),
        grid=(nb, M_HEADS, nc),
        in_specs=in_specs,
        out_specs=[
            pl.BlockSpec((rows, kv), lambda i, h, c: (i * nc + c, h)),
            pl.BlockSpec((None, bb, None, kq, kv), st_map),
            pl.BlockSpec((None, bb, None, 1, kq), st_map),
            pl.BlockSpec((None, bb, None, 1, LANES), st_map),
        ],
        out_shape=[
            jax.ShapeDtypeStruct((batch * nc * seg, HV), BF16),
            jax.ShapeDtypeStruct((n_a, batch, M_HEADS, kq, kv), F32),
            jax.ShapeDtypeStruct((n_a, batch, M_HEADS, 1, kq), F32),
            jax.ShapeDtypeStruct((n_a, batch, M_HEADS, 1, LANES), F32),
        ],
        scratch_shapes=[
            pltpu.VMEM((bb, kq, kv), F32),
            pltpu.VMEM((bb, 1, kq), F32),
            pltpu.VMEM((bb, 1, LANES), F32),
        ],
        input_output_aliases=aliases,
        compiler_params=_params(("parallel", "parallel", "arbitrary")),
        name="mlstm_scan",
    )(*args)
    return hg, (c_out, n_out, m_out)


def _mix_kernel(h_ref, w_ref, bias_ref, vw_ref, um_ref, v_ref, *, lc):
    t = h_ref.shape[0]
    vraw = h_ref[:, S_WIDTH:]
    mu = jnp.mean(vraw, axis=-1, keepdims=True)
    vc = vraw - mu
    var = jnp.mean(vc * vc, axis=-1, keepdims=True)
    v = vc * lax.rsqrt(var + LN_EPS) * vw_ref[...]
    v_ref[...] = v
    vb = v.astype(BF16)
    rr = lax.broadcasted_iota(jnp.int32, (t, t), 0)
    cc = lax.broadcasted_iota(jnp.int32, (t, t), 1)
    keep = _group_causal(rr, cc, lc)
    for g in range(S_GROUPS):
        sl = slice(g * S_DG, (g + 1) * S_DG)
        wg = (w_ref[g] * keep).astype(BF16)
        mixed = jnp.dot(wg, vb[:, sl], preferred_element_type=F32) + bias_ref[:, g:g + 1]
        um_ref[:, sl] = (h_ref[:, sl] * mixed).astype(BF16)


def _mix(hdn, wmix, bias_t, vnorm_w, layer, *, lc):
    m = hdn.shape[0]
    t = S_CHUNK
    return pl.pallas_call(
        functools.partial(_mix_kernel, lc=lc),
        grid=(m // t,),
        in_specs=[
            pl.BlockSpec((t, 2 * S_WIDTH), lambda i: (i, 0)),
            pl.BlockSpec((None, S_GROUPS, t, t), lambda i: (layer, 0, 0, 0)),
            pl.BlockSpec((None, t, S_GROUPS), lambda i: (layer, 0, 0)),
            pl.BlockSpec((None, 1, S_WIDTH), lambda i: (layer, 0, 0)),
        ],
        out_specs=[pl.BlockSpec((t, S_WIDTH), lambda i: (i, 0)), pl.BlockSpec((t, S_WIDTH), lambda i: (i, 0))],
        out_shape=[jax.ShapeDtypeStruct((m, S_WIDTH), BF16), jax.ShapeDtypeStruct((m, S_WIDTH), F32)],
        compiler_params=_params(("parallel",)),
        name="chunk_mix",
    )(hdn, wmix, bias_t, vnorm_w)


def _route_kernel(xb_ref, wq_ref, keys_ref, r1_ref, e1_ref, nn_ref, c0_ref, q_s, sc_s, wk_s, rk_s, sv_s):
    tm = xb_ref.shape[0]
    q_s[...] = jnp.dot(xb_ref[...], wq_ref[...], preferred_element_type=F32)
    kidx = lax.broadcasted_iota(jnp.int32, (P_NKEYS, tm), 0).astype(F32)
    for h in range(P_HEADS):
        for c in range(2):
            hc = 2 * h + c
            qhc = q_s[:, hc * P_DH:(hc + 1) * P_DH].astype(BF16)
            sc = lax.dot_general(keys_ref[c], qhc, (((1,), (1,)), ((), ())), preferred_element_type=F32)
            sc_s[hc] = sc
            wk_s[c] = sc
            rk_s[hc] = jnp.full((P_NKEYS, tm), float(P_NKEYS), F32)

        def body(kk, carry, h=h):
            for c in range(2):
                hc = 2 * h + c
                s = wk_s[c]
                mx = jnp.max(s, axis=0, keepdims=True)
                idx = jnp.min(jnp.where(s == mx, kidx, float(P_NKEYS)), axis=0, keepdims=True)
                sel = kidx == idx
                wk_s[c] = jnp.where(sel, NEG_INF, s)
                rk_s[hc] = jnp.where(sel, lax.convert_element_type(kk, F32), rk_s[hc])
                sv_s[c, kk, h:h + 1, :] = mx
            return carry

        lax.fori_loop(0, P_TOPK, body, 0)

    sv0 = [sv_s[0, k] for k in range(P_TOPK)]
    sv1 = [sv_s[1, k] for k in range(P_TOPK)]
    cnt = [jnp.zeros((P_HEADS, tm), F32) for _ in range(P_TOPK)]
    front = [sv0[k] + sv1[0] for k in range(P_TOPK)]
    top = front[0]
    z = jnp.zeros((P_HEADS, tm), F32)
    for r in range(P_TOPK):
        live = min(r + 1, P_TOPK)
        mx = front[0]
        for k in range(1, live):
            mx = jnp.maximum(mx, front[k])
        pick = jnp.full((P_HEADS, tm), float(P_TOPK), F32)
        for k in reversed(range(live)):
            pick = jnp.where(front[k] == mx, float(k), pick)
        z = z + jnp.exp(mx - top)
        hits = [pick == float(k) for k in range(live)]
        newcnt = jnp.zeros((P_HEADS, tm), F32)
        for k in range(live):
            cnt[k] = cnt[k] + jnp.where(hits[k], 1.0, 0.0)
            newcnt = jnp.where(hits[k], cnt[k], newcnt)
        nxt = jnp.full((P_HEADS, tm), NEG_INF, F32)
        for j in range(1, min(r + 2, P_TOPK)):
            nxt = jnp.where(newcnt == float(j), sv1[j], nxt)
        for k in range(live):
            front[k] = jnp.where(hits[k], sv0[k] + nxt, front[k])
    zinv = 1.0 / z

    for h in range(P_HEADS):
        rank0 = rk_s[2 * h]
        nn = jnp.zeros((P_NKEYS, tm), F32)
        for k in range(P_TOPK):
            nn = jnp.where(rank0 == float(k), cnt[k][h:h + 1, :], nn)
        nn_ref[h] = nn
        c0_ref[h] = jnp.exp(sc_s[2 * h] - sv0[0][h:h + 1, :]) * zinv[h:h + 1, :]
        r1_ref[h] = pltpu.bitcast(rk_s[2 * h + 1].astype(BF16), jnp.int32)
        e1_ref[h] = pltpu.bitcast(jnp.exp(sc_s[2 * h + 1] - sv1[0][h:h + 1, :]).astype(BF16), jnp.int32)


def _route(xb, wq, keys, li, *, tm=256):
    m, k = xb.shape
    tab = jax.ShapeDtypeStruct((P_HEADS, P_NKEYS, m), F32)
    tab16 = jax.ShapeDtypeStruct((P_HEADS, P_NKEYS // 2, m), jnp.int32)
    tab_spec = pl.BlockSpec((P_HEADS, P_NKEYS, tm), lambda i: (0, 0, i))
    tab16_spec = pl.BlockSpec((P_HEADS, P_NKEYS // 2, tm), lambda i: (0, 0, i))
    return pl.pallas_call(
        _route_kernel,
        grid=(m // tm,),
        in_specs=[
            pl.BlockSpec((tm, k), lambda i: (i, 0)),
            pl.BlockSpec((None, k, P_HEADS * 2 * P_DH), lambda i: (li, 0, 0)),
            pl.BlockSpec((None, 2, P_NKEYS, P_DH), lambda i: (li, 0, 0, 0)),
        ],
        out_specs=[tab16_spec, tab16_spec, tab_spec, tab_spec],
        out_shape=[tab16, tab16, tab, tab],
        scratch_shapes=[
            pltpu.VMEM((tm, P_HEADS * 2 * P_DH), F32),
            pltpu.VMEM((2 * P_HEADS, P_NKEYS, tm), F32),
            pltpu.VMEM((2, P_NKEYS, tm), F32),
            pltpu.VMEM((2 * P_HEADS, P_NKEYS, tm), F32),
            pltpu.VMEM((2, P_TOPK, P_HEADS, tm), F32),
        ],
        compiler_params=_params(("parallel",)),
        name="peer_route",
    )(xb, wq, keys)


def _gelu_tanh(x):
    c = 0.7978845608028654
    inner = x + x * x * x * 0.044715
    return x * 0.5 * (jnp.tanh(inner * c) + 1.0)


def _peer_kernel(xb_ref, u_ref, v_ref, r1_ref, e1_ref, nn_ref, c0_ref, xf_ref, g_ref, b_ref,
                 of_ref, ob_ref, acc_s, st_s, act_s, xt_s, *, te):
    e = pl.program_id(1)
    last = pl.num_programs(1) - 1
    n_i1 = te // P_NKEYS
    tb = st_s.shape[1]

    @pl.when(e == 0)
    def _():
        acc_s[...] = jnp.zeros_like(acc_s)
        xt_s[...] = xb_ref[...].T

    st_s[...] = jnp.dot(u_ref[...], xt_s[...], preferred_element_type=F32)
    base = e * n_i1
    shape3 = (P_NKEYS // BF16_ROWS, BF16_ROWS, LANES)
    zero = jnp.zeros(shape3, BF16)
    for j in range(n_i1):
        rows = slice(j * P_NKEYS, (j + 1) * P_NKEYS)
        for t in range(tb // LANES):
            cols = slice(t * LANES, (t + 1) * LANES)
            gate = zero
            for h in range(P_HEADS):
                nn = nn_ref[h, pl.ds(base + j, 1), :][:, cols]
                c0 = c0_ref[h, pl.ds(base + j, 1), :][:, cols]
                nn = jnp.broadcast_to(nn, (BF16_ROWS, LANES)).astype(BF16)[None]
                c0 = jnp.broadcast_to(c0, (BF16_ROWS, LANES)).astype(BF16)[None]
                r1 = pltpu.bitcast(r1_ref[h, :, cols], BF16).reshape(shape3)
                e1 = pltpu.bitcast(e1_ref[h, :, cols], BF16).reshape(shape3)
                gate = gate + jnp.where(r1 < nn, e1, zero) * c0
            act = _gelu_tanh(st_s[rows, cols].astype(BF16).reshape(shape3)) * gate
            act_s[rows, cols] = act.reshape(P_NKEYS, LANES)
    acc_s[...] += lax.dot_general(act_s[...], v_ref[...], (((0,), (0,)), ((), ())),
                                  preferred_element_type=F32)

    @pl.when(e == last)
    def _():
        o = _layer_norm_rows(ALPHA * xf_ref[...] + acc_s[...], g_ref[...], b_ref[...])
        of_ref[...] = o
        ob_ref[...] = o.astype(BF16)


def _peer(xb, xf, u, v, r1, e1, nn, c0, ln_g, ln_b, li, *, tb=PEER_TB, te=PEER_TE):
    m, d = xf.shape
    ne = P_NEXP // te
    once = pl.Buffered(1)
    tab_spec = pl.BlockSpec((P_HEADS, P_NKEYS, tb), lambda i, e: (0, 0, i), pipeline_mode=once)
    tab16_spec = pl.BlockSpec((P_HEADS, P_NKEYS // 2, tb), lambda i, e: (0, 0, i), pipeline_mode=once)
    return pl.pallas_call(
        functools.partial(_peer_kernel, te=te),
        grid=(m // tb, ne),
        in_specs=[
            pl.BlockSpec((tb, d), lambda i, e: (i, 0), pipeline_mode=once),
            pl.BlockSpec((None, te, d), lambda i, e: (li, e, 0)),
            pl.BlockSpec((None, te, d), lambda i, e: (li, e, 0)),
            tab16_spec, tab16_spec, tab_spec, tab_spec,
            pl.BlockSpec((tb, d), lambda i, e: (i, 0), pipeline_mode=once),
            _ln_spec(d, li, 1),
            _ln_spec(d, li, 1),
        ],
        out_specs=[pl.BlockSpec((tb, d), lambda i, e: (i, 0)), pl.BlockSpec((tb, d), lambda i, e: (i, 0))],
        out_shape=[jax.ShapeDtypeStruct((m, d), F32), jax.ShapeDtypeStruct((m, d), BF16)],
        scratch_shapes=[
            pltpu.VMEM((tb, d), F32),
            pltpu.VMEM((te, tb), F32),
            pltpu.VMEM((te, tb), BF16),
            pltpu.VMEM((d, tb), BF16),
        ],
        compiler_params=_params(("parallel", "arbitrary")),
        name="peer_dense",
    )(xb, u, v, r1, e1, nn, c0, xf, ln_g, ln_b)


def _trunk(x, p, c_in, n_in, m_in, *, batch, seq, wts):
    m = batch * seq
    xf = x
    xb = x.astype(BF16)
    decode = seq < MLSTM_PAD
    lc = min(seq, S_CHUNK)
    states = None
    vs = []
    for i in range(DEPTH):
        j = i // N_MIXERS
        if i % N_MIXERS == 0:
            if decode:
                seg, valid, nc, bb = MLSTM_PAD, seq, 1, 8
                xin = jnp.pad(xb.reshape(batch, seq, D_MODEL), ((0, 0), (0, seg - seq), (0, 0)))
                xin = xin.reshape(batch * seg, D_MODEL)
            else:
                seg, valid, nc, bb = MLSTM_CHUNK, MLSTM_CHUNK, seq // MLSTM_CHUNK, 1
                xin = xb
            proj = _mm(xin, wts["a_in"], j, n=A_MAIN)
            gates = _mm(xin, wts["a_gate"], j)
            hg, states = _mlstm_scan(proj, gates, wts["a_gbias"], wts["a_norm"], c_in, n_in, m_in, j, states,
                                     batch=batch, nc=nc, seg=seg, valid=valid, bb=bb)
            if decode:
                hg = hg.reshape(batch, seg, HV)[:, :seq].reshape(m, HV)
            w_out = wts["a_out"]
        else:
            hdn = _mm(xb, wts["b_in"], j, wts["b_in_bias"], act="gelu")
            hg, v = _mix(hdn, wts["b_mix"][lc], wts["b_mix_bias"][lc], wts["b_norm"], j, lc=lc)
            vs.append(v)
            w_out = wts["b_out"]
        x1f, x1b = _mm_ln(hg, w_out, j, xf, wts["ln_g"], wts["ln_b"], i)
        r1, e1, nn, c0 = _route(x1b, wts["peer_wq"], wts["peer_keys"], i)
        x2f, x2b = _peer(x1b, x1f, wts["peer_u"], wts["peer_v"], r1, e1, nn, c0, wts["ln_g"], wts["ln_b"], i)
        xf, xb = _ple(x2f, x2b, wts["ple_gate"], p, wts["ple_w"], i)
    c_out, n_out, m_out = states
    return xf, c_out, n_out[:, :, :, 0, :], m_out[:, :, :, 0, 0], vs


def _prep_weights(w_a_in, b_a_gate, a_norm_w, w_a_out, w_b_in, b_b_in, b_norm_w, w_b_s, b_b_s, w_b_out,
                  ln_g, ln_b, peer_wq, peer_keys, peer_u, peer_v, ple_w, ple_gate_w, chunk_lens):
    n_a = w_a_in.shape[0]
    n_b = w_b_in.shape[0]
    gpad = LANES - 2 * M_HEADS
    wts = {
        "a_in": w_a_in.astype(BF16),
        "a_gate": jnp.pad(w_a_in[:, :, A_MAIN:], ((0, 0), (0, 0), (0, gpad))).astype(BF16),
        "a_gbias": jnp.pad(b_a_gate, ((0, 0), (0, gpad))).reshape(n_a, 1, LANES).astype(F32),
        "a_norm": a_norm_w.reshape(n_a, 1, HV).astype(F32),
        "a_out": w_a_out.astype(BF16),
        "b_in": w_b_in.astype(BF16),
        "b_in_bias": b_b_in.reshape(n_b, 1, -1).astype(F32),
        "b_norm": b_norm_w.reshape(n_b, 1, S_WIDTH).astype(F32),
        "b_out": w_b_out.astype(BF16),
        "b_mix": {},
        "b_mix_bias": {},
        "ln_g": ln_g.reshape(DEPTH, 2, 1, D_MODEL).astype(F32),
        "ln_b": ln_b.reshape(DEPTH, 2, 1, D_MODEL).astype(F32),
        "peer_wq": peer_wq.astype(BF16),
        "peer_keys": peer_keys.astype(BF16),
        "peer_u": peer_u.astype(BF16),
        "peer_v": peer_v.astype(BF16),
        "ple_w": ple_w.astype(BF16),
        "ple_gate": ple_gate_w.astype(BF16),
    }
    for lc in chunk_lens:
        rep = S_CHUNK // lc
        if rep == 1:
            wts["b_mix"][lc] = w_b_s.astype(F32)
        else:
            wts["b_mix"][lc] = jnp.tile(w_b_s[:, :, :lc, :lc], (1, 1, rep, rep)).astype(F32)
        wts["b_mix_bias"][lc] = jnp.swapaxes(jnp.tile(b_b_s[:, :, :lc], (1, 1, rep)), 1, 2).astype(F32)
    return wts


def kernel(x_prompt, x_sample, state_C, state_n, state_m, p_prompt, p_sample, w_a_in, b_a_gate, a_norm_w, w_a_out, w_b_in, b_b_in, b_norm_w, w_b_s, b_b_s, w_b_out, ln_g, ln_b, peer_wq, peer_keys, peer_u, peer_v, ple_w, ple_gate_w):
    bp, sp, d = x_prompt.shape
    bs, ss, _ = x_sample.shape
    n_a = state_C.shape[0]
    chunk_lens = sorted({min(sp, S_CHUNK), min(ss, S_CHUNK)})
    wts = _prep_weights(w_a_in, b_a_gate, a_norm_w, w_a_out, w_b_in, b_b_in, b_norm_w, w_b_s, b_b_s, w_b_out,
                        ln_g, ln_b, peer_wq, peer_keys, peer_u, peer_v, ple_w, ple_gate_w, chunk_lens)

    c0 = jnp.zeros((n_a, bp, M_HEADS, M_DK, M_DV), F32)
    n0 = jnp.zeros((n_a, bp, M_HEADS, M_DK), F32)
    m0 = jnp.zeros((n_a, bp, M_HEADS), F32)
    yp, pc, pn, pm, _ = _trunk(x_prompt.reshape(bp * sp, d), p_prompt.reshape(DEPTH, bp * sp, -1),
                               c0, n0, m0, batch=bp, seq=sp, wts=wts)
    ys, sc, sn, sm, sv = _trunk(x_sample.reshape(bs * ss, d), p_sample.reshape(DEPTH, bs * ss, -1),
                                state_C, state_n, state_m, batch=bs, seq=ss, wts=wts)
    return (yp.reshape(bp, sp, d), ys.reshape(bs, ss, d), pc, pn, pm, sc, sn, sm,
            jnp.stack([v.reshape(bs, ss, S_WIDTH) for v in sv]))
```

```python
import functools

import jax
import jax.numpy as jnp
from jax import lax
from jax.experimental import pallas as pl
from jax.experimental.pallas import tpu as pltpu

F32 = jnp.float32
BF16 = jnp.bfloat16

D_MODEL = 2048
DEPTH = 4
N_MIXERS = 2
M_HEADS = 8
M_DV = D_MODEL // M_HEADS
M_DK = M_DV // 2
HK = M_HEADS * M_DK
HV = M_HEADS * M_DV
A_MAIN = 2 * HK + 2 * HV
S_GROUPS = 8
S_CHUNK = 128
S_WIDTH = D_MODEL
S_DG = S_WIDTH // S_GROUPS
P_HEADS = 8
P_NKEYS = 128
P_NEXP = P_NKEYS * P_NKEYS
P_DH = 128
P_TOPK = 16
ALPHA = (2 * DEPTH) ** 0.25
LN_EPS = 1e-5

LANES = 128
BF16_ROWS = 16
VMEM_LIMIT = 56 * 1024 * 1024
NEG_INF = float("-inf")

MLSTM_CHUNK = 256
MLSTM_PAD = 16
MLSTM_HEADS_PER_STEP = 4
PEER_TB = 512
PEER_TE = 1024
MM_ROWS = 1024
RES_ROWS = 512


def _params(sem, flags=None):
    return pltpu.CompilerParams(dimension_semantics=sem, vmem_limit_bytes=VMEM_LIMIT, flags=flags)


def _layer_norm_rows(z, g, b):
    mu = jnp.mean(z, axis=-1, keepdims=True)
    zc = z - mu
    var = jnp.mean(zc * zc, axis=-1, keepdims=True)
    return zc * lax.rsqrt(var + LN_EPS) * g + b


def _mm_kernel(*refs, act, has_bias):
    if has_bias:
        x_ref, w_ref, b_ref, o_ref = refs
    else:
        x_ref, w_ref, o_ref = refs
    acc = jnp.dot(x_ref[...].astype(BF16), w_ref[...], preferred_element_type=F32)
    if has_bias:
        acc = acc + b_ref[...]
    if act == "gelu":
        acc = jax.nn.gelu(acc)
    o_ref[...] = acc.astype(o_ref.dtype)


def _mm(x, w, layer, bias=None, *, n=None, act=None, out_dtype=F32, tm=None, tn=512):
    m, k = x.shape
    n = w.shape[2] if n is None else n
    tn = min(tn, n)
    if tm is None:
        tm = MM_ROWS if m % MM_ROWS == 0 else m
    in_specs = [
        pl.BlockSpec((tm, k), lambda i, j: (i, 0)),
        pl.BlockSpec((None, k, tn), lambda i, j: (layer, 0, j)),
    ]
    args = [x, w]
    if bias is not None:
        in_specs.append(pl.BlockSpec((None, 1, tn), lambda i, j: (layer, 0, j)))
        args.append(bias)
    return pl.pallas_call(
        functools.partial(_mm_kernel, act=act, has_bias=bias is not None),
        grid=(m // tm, n // tn),
        in_specs=in_specs,
        out_specs=pl.BlockSpec((tm, tn), lambda i, j: (i, j)),
        out_shape=jax.ShapeDtypeStruct((m, n), out_dtype),
        compiler_params=_params(("parallel", "parallel")),
        name="mm",
    )(*args)


def _mm_ln_kernel(a_ref, w_ref, r_ref, g_ref, b_ref, of_ref, ob_ref):
    y = jnp.dot(a_ref[...].astype(BF16), w_ref[...], preferred_element_type=F32)
    o = _layer_norm_rows(ALPHA * r_ref[...] + y, g_ref[...], b_ref[...])
    of_ref[...] = o
    ob_ref[...] = o.astype(BF16)


def _ln_spec(n, li, s):
    return pl.BlockSpec((None, None, 1, n), lambda *_: (li, s, 0, 0))


def _mm_ln(a, w, layer, res, ln_g, ln_b, li, *, tm=RES_ROWS):
    m, k = a.shape
    n = w.shape[2]
    return pl.pallas_call(
        _mm_ln_kernel,
        grid=(m // tm,),
        in_specs=[
            pl.BlockSpec((tm, k), lambda i: (i, 0)),
            pl.BlockSpec((None, k, n), lambda i: (layer, 0, 0)),
            pl.BlockSpec((tm, n), lambda i: (i, 0)),
            _ln_spec(n, li, 0),
            _ln_spec(n, li, 0),
        ],
        out_specs=[pl.BlockSpec((tm, n), lambda i: (i, 0)), pl.BlockSpec((tm, n), lambda i: (i, 0))],
        out_shape=[jax.ShapeDtypeStruct((m, n), F32), jax.ShapeDtypeStruct((m, n), BF16)],
        compiler_params=_params(("parallel",)),
        name="mm_ln",
    )(a, w, res, ln_g, ln_b)


def _ple_kernel(xf_ref, xb_ref, gw_ref, p_ref, pw_ref, of_ref, ob_ref):
    gate = jax.nn.sigmoid(jnp.dot(xb_ref[...], gw_ref[...], preferred_element_type=F32))
    pe = jnp.dot(p_ref[...].astype(BF16), pw_ref[...], preferred_element_type=F32)
    o = xf_ref[...] + gate * pe
    of_ref[...] = o
    ob_ref[...] = o.astype(BF16)


def _ple(xf, xb, gw, p, pw, li, *, tm=RES_ROWS):
    m, n = xf.shape
    kp = p.shape[2]
    return pl.pallas_call(
        _ple_kernel,
        grid=(m // tm,),
        in_specs=[
            pl.BlockSpec((tm, n), lambda i: (i, 0)),
            pl.BlockSpec((tm, n), lambda i: (i, 0)),
            pl.BlockSpec((None, n, n), lambda i: (li, 0, 0)),
            pl.BlockSpec((None, tm, kp), lambda i: (li, i, 0)),
            pl.BlockSpec((None, kp, n), lambda i: (li, 0, 0)),
        ],
        out_specs=[pl.BlockSpec((tm, n), lambda i: (i, 0)), pl.BlockSpec((tm, n), lambda i: (i, 0))],
        out_shape=[jax.ShapeDtypeStruct((m, n), F32), jax.ShapeDtypeStruct((m, n), BF16)],
        compiler_params=_params(("parallel",)),
        name="ple",
    )(xf, xb, gw, p, pw)


def _group_causal(rr, cc, group):
    shift = group.bit_length() - 1
    same = (rr >> shift) == (cc >> shift)
    return jnp.where(rr >= cc, jnp.where(same, 1.0, 0.0), 0.0)


def _split3(x):
    hi = x.astype(BF16)
    r1 = x - hi.astype(F32)
    mid = r1.astype(BF16)
    lo = (r1 - mid.astype(F32)).astype(BF16)
    return hi, mid, lo


def _gate_prep_kernel(g_ref, gb_ref, x_ref, xt_ref, *, seg, valid):
    rows = g_ref.shape[0]
    g = g_ref[...] + gb_ref[...]
    lane = lax.broadcasted_iota(jnp.int32, (rows, LANES), 1)
    r = lax.broadcasted_iota(jnp.int32, (rows, LANES), 0)
    pad = (r & (seg - 1)) >= valid
    lf = jnp.where(pad, 0.0, jax.nn.log_sigmoid(g))
    ig = jnp.where(pad, NEG_INF, g)
    rr = lax.broadcasted_iota(jnp.int32, (rows, rows), 0)
    cc = lax.broadcasted_iota(jnp.int32, (rows, rows), 1)
    tri = _group_causal(rr, cc, seg).astype(BF16)
    hi, mid, lo = _split3(lf)
    bcum = (jnp.dot(tri, hi, preferred_element_type=F32)
            + jnp.dot(tri, mid, preferred_element_type=F32)
            + jnp.dot(tri, lo, preferred_element_type=F32))
    x = jnp.where(lane >= M_HEADS, bcum, ig)
    x_ref[...] = x
    xt_ref[...] = x.T


def _gate_prep(gates, gbias, layer, *, seg, valid, rows):
    m = gates.shape[0]
    return pl.pallas_call(
        functools.partial(_gate_prep_kernel, seg=seg, valid=valid),
        grid=(m // rows,),
        in_specs=[pl.BlockSpec((rows, LANES), lambda i: (i, 0)),
                  pl.BlockSpec((None, 1, LANES), lambda i: (layer, 0, 0))],
        out_specs=[pl.BlockSpec((rows, LANES), lambda i: (i, 0)), pl.BlockSpec((LANES, rows), lambda i: (0, i))],
        out_shape=[jax.ShapeDtypeStruct((m, LANES), F32), jax.ShapeDtypeStruct((LANES, m), F32)],
        compiler_params=_params(("parallel",)),
        name="gate_prep",
    )(gates, gbias)


def _mlstm_kernel(q_ref, k_ref, v_ref, o_ref, x_ref, irow_ref, brow_ref, nw_ref, c0_ref, n0_ref, m0_ref,
                  *rest, bb, seg, hp, n_prev):
    hg_ref, c_ref, n_ref, m_ref, cs, ns, ms = rest[n_prev:]
    h0 = pl.program_id(1) * hp
    c = pl.program_id(2)
    rows = bb * seg
    lane = lax.broadcasted_iota(jnp.int32, (rows, LANES), 1)

    @pl.when(c == 0)
    def _():
        cs[...] = c0_ref[...]
        ns[...] = n0_ref[...]
        ms[...] = jnp.broadcast_to(m0_ref[...], (bb, hp, 1, LANES))

    x = x_ref[...]
    rr = lax.broadcasted_iota(jnp.int32, (rows, rows), 0)
    cc = lax.broadcasted_iota(jnp.int32, (rows, rows), 1)
    visible = _group_causal(rr, cc, seg) > 0.0
    seg_id = lax.broadcasted_iota(jnp.int32, (rows, 1), 0) >> (seg.bit_length() - 1)

    def per_row(vals):
        out = vals[0]
        for bi in range(1, bb):
            out = jnp.where(seg_id == bi, vals[bi], out)
        return out

    ends = [(bi + 1) * seg - 1 for bi in range(bb)]
    for g in range(hp):
        h = h0 + g
        icol = jnp.sum(jnp.where(lane == h, x, 0.0), axis=1, keepdims=True)
        bcol = jnp.sum(jnp.where(lane == h + M_HEADS, x, 0.0), axis=1, keepdims=True)
        irow = irow_ref[g]
        brow = brow_ref[g]
        m_prev_b = [ms[bi, g][:, :1] for bi in range(bb)]
        b_last_b = [bcol[r:r + 1, :] for r in ends]
        m_prev = per_row(m_prev_b)
        b_last = per_row(b_last_b)
        n_rows = per_row([ns[bi, g] for bi in range(bb)])

        q = q_ref[:, g * M_DK:(g + 1) * M_DK].astype(F32) * (M_DK ** -0.5)
        k = k_ref[:, g * M_DK:(g + 1) * M_DK].astype(F32)
        qb = q.astype(BF16)
        kb = k.astype(BF16)
        vcols = slice(g * M_DV, (g + 1) * M_DV)
        vb = v_ref[:, vcols].astype(BF16)

        logd = jnp.where(visible, bcol - brow + irow, NEG_INF)
        inter = bcol + m_prev
        m_row = jnp.maximum(inter, jnp.max(logd, axis=1, keepdims=True))
        dmat = jnp.exp(logd - m_row)
        s = lax.dot_general(qb, kb, (((1,), (1,)), ((), ())), preferred_element_type=F32) * dmat
        w_inter = jnp.exp(inter - m_row)
        qc = [jnp.dot(qb[bi * seg:(bi + 1) * seg], cs[bi, g].astype(BF16), preferred_element_type=F32)
              for bi in range(bb)]
        qc = qc[0] if bb == 1 else jnp.concatenate(qc, axis=0)
        num = w_inter * qc + jnp.dot(s.astype(BF16), vb, preferred_element_type=F32)
        den = w_inter * jnp.sum(q * n_rows, axis=1, keepdims=True) + jnp.sum(s, axis=1, keepdims=True)
        hh = num / jnp.maximum(jnp.abs(den), jnp.exp(-m_row))
        mu = jnp.mean(hh, axis=1, keepdims=True)
        hc = hh - mu
        var = jnp.mean(hc * hc, axis=1, keepdims=True)
        hn = hc * lax.rsqrt(var + LN_EPS) * nw_ref[:, vcols]
        hg_ref[:, vcols] = (hn * jax.nn.sigmoid(o_ref[:, vcols].astype(F32))).astype(hg_ref.dtype)

        m_new_b = [m_row[r:r + 1, :] for r in ends]
        g_tok = jnp.exp(b_last - bcol + icol - per_row(m_new_b))
        kg = g_tok * k
        kgb = kg.astype(BF16)
        for bi in range(bb):
            lo_r, hi_r = bi * seg, (bi + 1) * seg
            g_state = jnp.exp(b_last_b[bi] + m_prev_b[bi] - m_new_b[bi])
            c_new = g_state * cs[bi, g] + lax.dot_general(
                kgb[lo_r:hi_r], vb[lo_r:hi_r], (((0,), (0,)), ((), ())), preferred_element_type=F32)
            n_new = g_state * ns[bi, g] + jnp.sum(kg[lo_r:hi_r], axis=0, keepdims=True)
            m_new = jnp.broadcast_to(m_new_b[bi], (1, LANES))
            cs[bi, g] = c_new
            ns[bi, g] = n_new
            ms[bi, g] = m_new
            c_ref[bi, g] = c_new
            n_ref[bi, g] = n_new
            m_ref[bi, g] = m_new


def _mlstm_scan(proj, gates, gbias, norm_w, c0, n0, m0, layer, prev, *, batch, nc, seg, valid, bb,
                hp=MLSTM_HEADS_PER_STEP):
    rows = bb * seg
    nb = batch // bb
    n_a = c0.shape[0]
    kq, kv = M_DK, M_DV
    xg, xgt = _gate_prep(gates, gbias, layer, seg=seg, valid=valid, rows=rows)
    xgt = xgt[:2 * M_HEADS].reshape(2 * M_HEADS, 1, -1)
    nh = M_HEADS // hp
    row_map = lambda col: (lambda i, h, c: (i * nc + c, col(h)))
    st_map = lambda i, h, c: (layer, i, h, 0, 0)
    in_specs = [
        pl.BlockSpec((rows, hp * kq), row_map(lambda h: h)),
        pl.BlockSpec((rows, hp * kq), row_map(lambda h: HK // (hp * kq) + h)),
        pl.BlockSpec((rows, hp * kv), row_map(lambda h: 2 * HK // (hp * kv) + h)),
        pl.BlockSpec((rows, hp * kv), row_map(lambda h: (2 * HK + HV) // (hp * kv) + h)),
        pl.BlockSpec((rows, LANES), lambda i, h, c: (i * nc + c, 0)),
        pl.BlockSpec((hp, 1, rows), lambda i, h, c: (h, 0, i * nc + c)),
        pl.BlockSpec((hp, 1, rows), lambda i, h, c: (h + nh, 0, i * nc + c)),
        pl.BlockSpec((None, 1, hp * kv), lambda i, h, c: (layer, 0, h)),
        pl.BlockSpec((None, bb, hp, kq, kv), st_map),
        pl.BlockSpec((None, bb, hp, 1, kq), st_map),
        pl.BlockSpec((None, bb, hp, 1, 1), st_map),
    ]
    args = [proj, proj, proj, proj, xg, xgt, xgt, norm_w, c0,
            n0.reshape(n_a, batch, M_HEADS, 1, kq), m0.reshape(n_a, batch, M_HEADS, 1, 1)]
    aliases = {}
    n_prev = 0
    if prev is not None:
        n_prev = len(prev)
        for t, arr in enumerate(prev):
            aliases[len(args)] = 1 + t
            in_specs.append(pl.BlockSpec(memory_space=pl.ANY))
            args.append(arr)
    hg, c_out, n_out, m_out = pl.pallas_call(
        functools.partial(_mlstm_kernel, bb=bb, seg=seg, hp=hp, n_prev=n_prev),
        grid=(nb, nh, nc),
        in_specs=in_specs,
        out_specs=[
            pl.BlockSpec((rows, hp * kv), lambda i, h, c: (i * nc + c, h)),
            pl.BlockSpec((None, bb, hp, kq, kv), st_map),
            pl.BlockSpec((None, bb, hp, 1, kq), st_map),
            pl.BlockSpec((None, bb, hp, 1, LANES), st_map),
        ],
        out_shape=[
            jax.ShapeDtypeStruct((batch * nc * seg, HV), BF16),
            jax.ShapeDtypeStruct((n_a, batch, M_HEADS, kq, kv), F32),
            jax.ShapeDtypeStruct((n_a, batch, M_HEADS, 1, kq), F32),
            jax.ShapeDtypeStruct((n_a, batch, M_HEADS, 1, LANES), F32),
        ],
        scratch_shapes=[
            pltpu.VMEM((bb, hp, kq, kv), F32),
            pltpu.VMEM((bb, hp, 1, kq), F32),
            pltpu.VMEM((bb, hp, 1, LANES), F32),
        ],
        input_output_aliases=aliases,
        compiler_params=_params(("parallel", "parallel", "arbitrary")),
        name="mlstm_scan",
    )(*args)
    return hg, (c_out, n_out, m_out)


def _mix_kernel(h_ref, w_ref, bias_ref, vw_ref, um_ref, *maybe_v_ref, lc):
    t = h_ref.shape[0]
    vraw = h_ref[:, S_WIDTH:].astype(F32)
    mu = jnp.mean(vraw, axis=-1, keepdims=True)
    vc = vraw - mu
    var = jnp.mean(vc * vc, axis=-1, keepdims=True)
    v = vc * lax.rsqrt(var + LN_EPS) * vw_ref[...]
    for v_ref in maybe_v_ref:
        v_ref[...] = v
    vb = v.astype(BF16)
    rr = lax.broadcasted_iota(jnp.int32, (t, t), 0)
    cc = lax.broadcasted_iota(jnp.int32, (t, t), 1)
    keep = _group_causal(rr, cc, lc)
    for g in range(S_GROUPS):
        sl = slice(g * S_DG, (g + 1) * S_DG)
        wg = (w_ref[g] * keep).astype(BF16)
        mixed = jnp.dot(wg, vb[:, sl], preferred_element_type=F32) + bias_ref[:, g:g + 1]
        um_ref[:, sl] = (h_ref[:, sl].astype(F32) * mixed).astype(BF16)


def _mix(hdn, wmix, bias_t, vnorm_w, layer, *, lc, emit_v):
    m = hdn.shape[0]
    t = S_CHUNK
    row_spec = pl.BlockSpec((t, S_WIDTH), lambda i: (i, 0))
    out_specs = [row_spec, row_spec] if emit_v else [row_spec]
    out_shape = [jax.ShapeDtypeStruct((m, S_WIDTH), BF16)]
    if emit_v:
        out_shape.append(jax.ShapeDtypeStruct((m, S_WIDTH), F32))
    outs = pl.pallas_call(
        functools.partial(_mix_kernel, lc=lc),
        grid=(m // t,),
        in_specs=[
            pl.BlockSpec((t, 2 * S_WIDTH), lambda i: (i, 0)),
            pl.BlockSpec((None, S_GROUPS, t, t), lambda i: (layer, 0, 0, 0)),
            pl.BlockSpec((None, t, S_GROUPS), lambda i: (layer, 0, 0)),
            pl.BlockSpec((None, 1, S_WIDTH), lambda i: (layer, 0, 0)),
        ],
        out_specs=out_specs,
        out_shape=out_shape,
        compiler_params=_params(("parallel",)),
        name="chunk_mix",
    )(hdn, wmix, bias_t, vnorm_w)
    return (outs[0], outs[1]) if emit_v else (outs[0], None)


def _route_kernel(xb_ref, wq_ref, keys_ref, r1_ref, e1_ref, nn_ref, c0_ref, q_s, sc_s, wk_s, rk_s, sv_s):
    tm = xb_ref.shape[0]
    q_s[...] = jnp.dot(xb_ref[...], wq_ref[...], preferred_element_type=F32)
    kidx = lax.broadcasted_iota(jnp.int32, (P_NKEYS, tm), 0).astype(F32)
    for h in range(P_HEADS):
        for c in range(2):
            hc = 2 * h + c
            qhc = q_s[:, hc * P_DH:(hc + 1) * P_DH].astype(BF16)
            sc = lax.dot_general(keys_ref[c], qhc, (((1,), (1,)), ((), ())), preferred_element_type=F32)
            sc_s[hc] = sc
            wk_s[c] = sc
            rk_s[hc] = jnp.full((P_NKEYS, tm), float(P_NKEYS), F32)

        def body(kk, carry, h=h):
            for c in range(2):
                hc = 2 * h + c
                s = wk_s[c]
                mx = jnp.max(s, axis=0, keepdims=True)
                idx = jnp.min(jnp.where(s == mx, kidx, float(P_NKEYS)), axis=0, keepdims=True)
                sel = kidx == idx
                wk_s[c] = jnp.where(sel, NEG_INF, s)
                rk_s[hc] = jnp.where(sel, lax.convert_element_type(kk, F32), rk_s[hc])
                sv_s[c, kk, h:h + 1, :] = mx
            return carry

        lax.fori_loop(0, P_TOPK, body, 0)

    sv0 = [sv_s[0, k] for k in range(P_TOPK)]
    sv1 = [sv_s[1, k] for k in range(P_TOPK)]
    cnt = [jnp.zeros((P_HEADS, tm), F32) for _ in range(P_TOPK)]
    front = [sv0[k] + sv1[0] for k in range(P_TOPK)]
    top = front[0]
    z = jnp.zeros((P_HEADS, tm), F32)
    for r in range(P_TOPK):
        live = min(r + 1, P_TOPK)
        mx = front[0]
        for k in range(1, live):
            mx = jnp.maximum(mx, front[k])
        pick = jnp.full((P_HEADS, tm), float(P_TOPK), F32)
        for k in reversed(range(live)):
            pick = jnp.where(front[k] == mx, float(k), pick)
        z = z + jnp.exp(mx - top)
        hits = [pick == float(k) for k in range(live)]
        newcnt = jnp.zeros((P_HEADS, tm), F32)
        for k in range(live):
            cnt[k] = cnt[k] + jnp.where(hits[k], 1.0, 0.0)
            newcnt = jnp.where(hits[k], cnt[k], newcnt)
        nxt = jnp.full((P_HEADS, tm), NEG_INF, F32)
        for j in range(1, min(r + 2, P_TOPK)):
            nxt = jnp.where(newcnt == float(j), sv1[j], nxt)
        for k in range(live):
            front[k] = jnp.where(hits[k], sv0[k] + nxt, front[k])
    zinv = 1.0 / z

    for h in range(P_HEADS):
        rank0 = rk_s[2 * h]
        nn = jnp.zeros((P_NKEYS, tm), F32)
        for k in range(P_TOPK):
            nn = jnp.where(rank0 == float(k), cnt[k][h:h + 1, :], nn)
        nn_ref[h] = nn
        c0_ref[h] = jnp.exp(sc_s[2 * h] - sv0[0][h:h + 1, :]) * zinv[h:h + 1, :]
        r1_ref[h] = pltpu.bitcast(rk_s[2 * h + 1].astype(BF16), jnp.int32)
        e1_ref[h] = pltpu.bitcast(jnp.exp(sc_s[2 * h + 1] - sv1[0][h:h + 1, :]).astype(BF16), jnp.int32)


def _route(xb, wq, keys, li, *, tm=256):
    m, k = xb.shape
    tab = jax.ShapeDtypeStruct((P_HEADS, P_NKEYS, m), F32)
    tab16 = jax.ShapeDtypeStruct((P_HEADS, P_NKEYS // 2, m), jnp.int32)
    tab_spec = pl.BlockSpec((P_HEADS, P_NKEYS, tm), lambda i: (0, 0, i))
    tab16_spec = pl.BlockSpec((P_HEADS, P_NKEYS // 2, tm), lambda i: (0, 0, i))
    return pl.pallas_call(
        _route_kernel,
        grid=(m // tm,),
        in_specs=[
            pl.BlockSpec((tm, k), lambda i: (i, 0)),
            pl.BlockSpec((None, k, P_HEADS * 2 * P_DH), lambda i: (li, 0, 0)),
            pl.BlockSpec((None, 2, P_NKEYS, P_DH), lambda i: (li, 0, 0, 0)),
        ],
        out_specs=[tab16_spec, tab16_spec, tab_spec, tab_spec],
        out_shape=[tab16, tab16, tab, tab],
        scratch_shapes=[
            pltpu.VMEM((tm, P_HEADS * 2 * P_DH), F32),
            pltpu.VMEM((2 * P_HEADS, P_NKEYS, tm), F32),
            pltpu.VMEM((2, P_NKEYS, tm), F32),
            pltpu.VMEM((2 * P_HEADS, P_NKEYS, tm), F32),
            pltpu.VMEM((2, P_TOPK, P_HEADS, tm), F32),
        ],
        compiler_params=_params(("parallel",)),
        name="peer_route",
    )(xb, wq, keys)


def _gelu_tanh(x):
    c = 0.7978845608028654
    inner = x + x * x * x * 0.044715
    return x * 0.5 * (jnp.tanh(inner * c) + 1.0)


def _peer_kernel(xb_ref, u_ref, v_ref, r1_ref, e1_ref, nn_ref, c0_ref, xf_ref, g_ref, b_ref,
                 of_ref, ob_ref, acc_s, st_s, act_s, xt_s, *, te):
    e = pl.program_id(1)
    last = pl.num_programs(1) - 1
    n_i1 = te // P_NKEYS
    tb = st_s.shape[1]

    @pl.when(e == 0)
    def _():
        acc_s[...] = jnp.zeros_like(acc_s)
        xt_s[...] = xb_ref[...].T

    st_s[...] = jnp.dot(u_ref[...], xt_s[...], preferred_element_type=F32)
    base = e * n_i1
    shape3 = (P_NKEYS // BF16_ROWS, BF16_ROWS, LANES)
    zero = jnp.zeros(shape3, BF16)
    for j in range(n_i1):
        rows = slice(j * P_NKEYS, (j + 1) * P_NKEYS)
        for t in range(tb // LANES):
            cols = slice(t * LANES, (t + 1) * LANES)
            gate = zero
            for h in range(P_HEADS):
                nn = nn_ref[h, pl.ds(base + j, 1), :][:, cols]
                c0 = c0_ref[h, pl.ds(base + j, 1), :][:, cols]
                nn = jnp.broadcast_to(nn, (BF16_ROWS, LANES)).astype(BF16)[None]
                c0 = jnp.broadcast_to(c0, (BF16_ROWS, LANES)).astype(BF16)[None]
                r1 = pltpu.bitcast(r1_ref[h, :, cols], BF16).reshape(shape3)
                e1 = pltpu.bitcast(e1_ref[h, :, cols], BF16).reshape(shape3)
                gate = gate + jnp.where(r1 < nn, e1, zero) * c0
            act = _gelu_tanh(st_s[rows, cols].astype(BF16).reshape(shape3)) * gate
            act_s[rows, cols] = act.reshape(P_NKEYS, LANES)
    acc_s[...] += lax.dot_general(act_s[...], v_ref[...], (((0,), (0,)), ((), ())),
                                  preferred_element_type=F32)

    @pl.when(e == last)
    def _():
        o = _layer_norm_rows(ALPHA * xf_ref[...] + acc_s[...], g_ref[...], b_ref[...])
        of_ref[...] = o
        ob_ref[...] = o.astype(BF16)


def _peer(xb, xf, u, v, r1, e1, nn, c0, ln_g, ln_b, li, *, tb=PEER_TB, te=PEER_TE):
    m, d = xf.shape
    ne = P_NEXP // te
    once = pl.Buffered(1)
    tab_spec = pl.BlockSpec((P_HEADS, P_NKEYS, tb), lambda i, e: (0, 0, i), pipeline_mode=once)
    tab16_spec = pl.BlockSpec((P_HEADS, P_NKEYS // 2, tb), lambda i, e: (0, 0, i), pipeline_mode=once)
    return pl.pallas_call(
        functools.partial(_peer_kernel, te=te),
        grid=(m // tb, ne),
        in_specs=[
            pl.BlockSpec((tb, d), lambda i, e: (i, 0), pipeline_mode=once),
            pl.BlockSpec((None, te, d), lambda i, e: (li, e, 0)),
            pl.BlockSpec((None, te, d), lambda i, e: (li, e, 0)),
            tab16_spec, tab16_spec, tab_spec, tab_spec,
            pl.BlockSpec((tb, d), lambda i, e: (i, 0), pipeline_mode=once),
            _ln_spec(d, li, 1),
            _ln_spec(d, li, 1),
        ],
        out_specs=[pl.BlockSpec((tb, d), lambda i, e: (i, 0)), pl.BlockSpec((tb, d), lambda i, e: (i, 0))],
        out_shape=[jax.ShapeDtypeStruct((m, d), F32), jax.ShapeDtypeStruct((m, d), BF16)],
        scratch_shapes=[
            pltpu.VMEM((tb, d), F32),
            pltpu.VMEM((te, tb), F32),
            pltpu.VMEM((te, tb), BF16),
            pltpu.VMEM((d, tb), BF16),
        ],
        compiler_params=_params(("parallel", "arbitrary")),
        name="peer_dense",
    )(xb, u, v, r1, e1, nn, c0, xf, ln_g, ln_b)


def _trunk(x, p, c_in, n_in, m_in, *, batch, seq, wts, want_v):
    m = batch * seq
    xf = x
    xb = x.astype(BF16)
    decode = seq < MLSTM_PAD
    lc = min(seq, S_CHUNK)
    states = None
    vs = []
    for i in range(DEPTH):
        j = i // N_MIXERS
        if i % N_MIXERS == 0:
            if decode:
                seg, valid, nc, bb = MLSTM_PAD, seq, 1, 8
                xin = jnp.pad(xb.reshape(batch, seq, D_MODEL), ((0, 0), (0, seg - seq), (0, 0)))
                xin = xin.reshape(batch * seg, D_MODEL)
            else:
                seg, valid, nc, bb = MLSTM_CHUNK, MLSTM_CHUNK, seq // MLSTM_CHUNK, 1
                xin = xb
            proj = _mm(xin, wts["a_in"], j, n=A_MAIN, out_dtype=BF16)
            gates = _mm(xin, wts["a_gate"], j)
            hg, states = _mlstm_scan(proj, gates, wts["a_gbias"], wts["a_norm"], c_in, n_in, m_in, j, states,
                                     batch=batch, nc=nc, seg=seg, valid=valid, bb=bb)
            if decode:
                hg = hg.reshape(batch, seg, HV)[:, :seq].reshape(m, HV)
            w_out = wts["a_out"]
        else:
            hdn = _mm(xb, wts["b_in"], j, wts["b_in_bias"], act="gelu", out_dtype=BF16)
            hg, v = _mix(hdn, wts["b_mix"][lc], wts["b_mix_bias"][lc], wts["b_norm"], j, lc=lc,
                         emit_v=want_v)
            vs.append(v)
            w_out = wts["b_out"]
        x1f, x1b = _mm_ln(hg, w_out, j, xf, wts["ln_g"], wts["ln_b"], i)
        r1, e1, nn, c0 = _route(x1b, wts["peer_wq"], wts["peer_keys"], i)
        x2f, x2b = _peer(x1b, x1f, wts["peer_u"], wts["peer_v"], r1, e1, nn, c0, wts["ln_g"], wts["ln_b"], i)
        xf, xb = _ple(x2f, x2b, wts["ple_gate"], p, wts["ple_w"], i)
    c_out, n_out, m_out = states
    return xf, c_out, n_out[:, :, :, 0, :], m_out[:, :, :, 0, 0], vs


def _prep_weights(w_a_in, b_a_gate, a_norm_w, w_a_out, w_b_in, b_b_in, b_norm_w, w_b_s, b_b_s, w_b_out,
                  ln_g, ln_b, peer_wq, peer_keys, peer_u, peer_v, ple_w, ple_gate_w, chunk_lens):
    n_a = w_a_in.shape[0]
    n_b = w_b_in.shape[0]
    gpad = LANES - 2 * M_HEADS
    wts = {
        "a_in": w_a_in.astype(BF16),
        "a_gate": jnp.pad(w_a_in[:, :, A_MAIN:], ((0, 0), (0, 0), (0, gpad))).astype(BF16),
        "a_gbias": jnp.pad(b_a_gate, ((0, 0), (0, gpad))).reshape(n_a, 1, LANES).astype(F32),
        "a_norm": a_norm_w.reshape(n_a, 1, HV).astype(F32),
        "a_out": w_a_out.astype(BF16),
        "b_in": w_b_in.astype(BF16),
        "b_in_bias": b_b_in.reshape(n_b, 1, -1).astype(F32),
        "b_norm": b_norm_w.reshape(n_b, 1, S_WIDTH).astype(F32),
        "b_out": w_b_out.astype(BF16),
        "b_mix": {},
        "b_mix_bias": {},
        "ln_g": ln_g.reshape(DEPTH, 2, 1, D_MODEL).astype(F32),
        "ln_b": ln_b.reshape(DEPTH, 2, 1, D_MODEL).astype(F32),
        "peer_wq": peer_wq.astype(BF16),
        "peer_keys": peer_keys.astype(BF16),
        "peer_u": peer_u.astype(BF16),
        "peer_v": peer_v.astype(BF16),
        "ple_w": ple_w.astype(BF16),
        "ple_gate": ple_gate_w.astype(BF16),
    }
    for lc in chunk_lens:
        rep = S_CHUNK // lc
        if rep == 1:
            wts["b_mix"][lc] = w_b_s.astype(F32)
        else:
            wts["b_mix"][lc] = jnp.tile(w_b_s[:, :, :lc, :lc], (1, 1, rep, rep)).astype(F32)
        wts["b_mix_bias"][lc] = jnp.swapaxes(jnp.tile(b_b_s[:, :, :lc], (1, 1, rep)), 1, 2).astype(F32)
    return wts


def kernel(x_prompt, x_sample, state_C, state_n, state_m, p_prompt, p_sample, w_a_in, b_a_gate, a_norm_w, w_a_out, w_b_in, b_b_in, b_norm_w, w_b_s, b_b_s, w_b_out, ln_g, ln_b, peer_wq, peer_keys, peer_u, peer_v, ple_w, ple_gate_w):
    bp, sp, d = x_prompt.shape
    bs, ss, _ = x_sample.shape
    n_a = state_C.shape[0]
    chunk_lens = sorted({min(sp, S_CHUNK), min(ss, S_CHUNK)})
    wts = _prep_weights(w_a_in, b_a_gate, a_norm_w, w_a_out, w_b_in, b_b_in, b_norm_w, w_b_s, b_b_s, w_b_out,
                        ln_g, ln_b, peer_wq, peer_keys, peer_u, peer_v, ple_w, ple_gate_w, chunk_lens)

    c0 = jnp.zeros((n_a, bp, M_HEADS, M_DK, M_DV), F32)
    n0 = jnp.zeros((n_a, bp, M_HEADS, M_DK), F32)
    m0 = jnp.zeros((n_a, bp, M_HEADS), F32)
    yp, pc, pn, pm, _ = _trunk(x_prompt.reshape(bp * sp, d), p_prompt.reshape(DEPTH, bp * sp, -1),
                               c0, n0, m0, batch=bp, seq=sp, wts=wts, want_v=False)
    ys, sc, sn, sm, sv = _trunk(x_sample.reshape(bs * ss, d), p_sample.reshape(DEPTH, bs * ss, -1),
                                state_C, state_n, state_m, batch=bs, seq=ss, wts=wts, want_v=True)
    return (yp.reshape(bp, sp, d), ys.reshape(bs, ss, d), pc, pn, pm, sc, sn, sm,
            jnp.stack([v.reshape(bs, ss, S_WIDTH) for v in sv]))
```

```python
import functools

import jax
import jax.numpy as jnp
from jax import lax
from jax.experimental import pallas as pl
from jax.experimental.pallas import tpu as pltpu

F32 = jnp.float32
BF16 = jnp.bfloat16

D_MODEL = 2048
DEPTH = 4
N_MIXERS = 2
M_HEADS = 8
M_DV = D_MODEL // M_HEADS
M_DK = M_DV // 2
HK = M_HEADS * M_DK
HV = M_HEADS * M_DV
A_MAIN = 2 * HK + 2 * HV
S_GROUPS = 8
S_CHUNK = 128
S_WIDTH = D_MODEL
S_DG = S_WIDTH // S_GROUPS
P_HEADS = 8
P_NKEYS = 128
P_NEXP = P_NKEYS * P_NKEYS
P_DH = 128
P_TOPK = 16
ALPHA = (2 * DEPTH) ** 0.25
LN_EPS = 1e-5

LANES = 128
BF16_ROWS = 16
VMEM_LIMIT = 56 * 1024 * 1024
NEG_INF = float("-inf")

MLSTM_CHUNK = 256
MLSTM_PAD = 16
MLSTM_HEADS_PER_STEP = 4
PEER_TB = 512
PEER_TE = 1024
MM_ROWS = 1024
RES_ROWS = 512


def _params(sem, flags=None):
    return pltpu.CompilerParams(dimension_semantics=sem, vmem_limit_bytes=VMEM_LIMIT, flags=flags)


def _layer_norm_rows(z, g, b):
    mu = jnp.mean(z, axis=-1, keepdims=True)
    zc = z - mu
    var = jnp.mean(zc * zc, axis=-1, keepdims=True)
    return zc * lax.rsqrt(var + LN_EPS) * g + b


def _mm_kernel(*refs, act, has_bias):
    if has_bias:
        x_ref, w_ref, b_ref, o_ref = refs
    else:
        x_ref, w_ref, o_ref = refs
    acc = jnp.dot(x_ref[...].astype(BF16), w_ref[...], preferred_element_type=F32)
    if has_bias:
        acc = acc + b_ref[...]
    if act == "gelu":
        acc = jax.nn.gelu(acc)
    o_ref[...] = acc.astype(o_ref.dtype)


def _mm(x, w, layer, bias=None, *, n=None, act=None, out_dtype=F32, tm=None, tn=512):
    m, k = x.shape
    n = w.shape[2] if n is None else n
    tn = min(tn, n)
    if tm is None:
        tm = MM_ROWS if m % MM_ROWS == 0 else m
    in_specs = [
        pl.BlockSpec((tm, k), lambda i, j: (i, 0)),
        pl.BlockSpec((None, k, tn), lambda i, j: (layer, 0, j)),
    ]
    args = [x, w]
    if bias is not None:
        in_specs.append(pl.BlockSpec((None, 1, tn), lambda i, j: (layer, 0, j)))
        args.append(bias)
    return pl.pallas_call(
        functools.partial(_mm_kernel, act=act, has_bias=bias is not None),
        grid=(m // tm, n // tn),
        in_specs=in_specs,
        out_specs=pl.BlockSpec((tm, tn), lambda i, j: (i, j)),
        out_shape=jax.ShapeDtypeStruct((m, n), out_dtype),
        compiler_params=_params(("parallel", "parallel")),
        name="mm",
    )(*args)


def _mm_ln_kernel(a_ref, w_ref, r_ref, g_ref, b_ref, of_ref, ob_ref):
    y = jnp.dot(a_ref[...].astype(BF16), w_ref[...], preferred_element_type=F32)
    o = _layer_norm_rows(ALPHA * r_ref[...] + y, g_ref[...], b_ref[...])
    of_ref[...] = o
    ob_ref[...] = o.astype(BF16)


def _ln_spec(n, li, s):
    return pl.BlockSpec((None, None, 1, n), lambda *_: (li, s, 0, 0))


def _mm_ln(a, w, layer, res, ln_g, ln_b, li, *, tm=RES_ROWS):
    m, k = a.shape
    n = w.shape[2]
    return pl.pallas_call(
        _mm_ln_kernel,
        grid=(m // tm,),
        in_specs=[
            pl.BlockSpec((tm, k), lambda i: (i, 0)),
            pl.BlockSpec((None, k, n), lambda i: (layer, 0, 0)),
            pl.BlockSpec((tm, n), lambda i: (i, 0)),
            _ln_spec(n, li, 0),
            _ln_spec(n, li, 0),
        ],
        out_specs=[pl.BlockSpec((tm, n), lambda i: (i, 0)), pl.BlockSpec((tm, n), lambda i: (i, 0))],
        out_shape=[jax.ShapeDtypeStruct((m, n), F32), jax.ShapeDtypeStruct((m, n), BF16)],
        compiler_params=_params(("parallel",)),
        name="mm_ln",
    )(a, w, res, ln_g, ln_b)


def _ple_kernel(xf_ref, xb_ref, gw_ref, p_ref, pw_ref, of_ref, ob_ref):
    gate = jax.nn.sigmoid(jnp.dot(xb_ref[...], gw_ref[...], preferred_element_type=F32))
    pe = jnp.dot(p_ref[...].astype(BF16), pw_ref[...], preferred_element_type=F32)
    o = xf_ref[...] + gate * pe
    of_ref[...] = o
    ob_ref[...] = o.astype(BF16)


def _ple(xf, xb, gw, p, pw, li, *, tm=RES_ROWS):
    m, n = xf.shape
    kp = p.shape[2]
    return pl.pallas_call(
        _ple_kernel,
        grid=(m // tm,),
        in_specs=[
            pl.BlockSpec((tm, n), lambda i: (i, 0)),
            pl.BlockSpec((tm, n), lambda i: (i, 0)),
            pl.BlockSpec((None, n, n), lambda i: (li, 0, 0)),
            pl.BlockSpec((None, tm, kp), lambda i: (li, i, 0)),
            pl.BlockSpec((None, kp, n), lambda i: (li, 0, 0)),
        ],
        out_specs=[pl.BlockSpec((tm, n), lambda i: (i, 0)), pl.BlockSpec((tm, n), lambda i: (i, 0))],
        out_shape=[jax.ShapeDtypeStruct((m, n), F32), jax.ShapeDtypeStruct((m, n), BF16)],
        compiler_params=_params(("parallel",)),
        name="ple",
    )(xf, xb, gw, p, pw)


def _group_causal(rr, cc, group):
    shift = group.bit_length() - 1
    same = (rr >> shift) == (cc >> shift)
    return jnp.where(rr >= cc, jnp.where(same, 1.0, 0.0), 0.0)


def _split3(x):
    hi = x.astype(BF16)
    r1 = x - hi.astype(F32)
    mid = r1.astype(BF16)
    lo = (r1 - mid.astype(F32)).astype(BF16)
    return hi, mid, lo


def _gate_prep_kernel(g_ref, gb_ref, x_ref, xt_ref, *, seg, valid):
    rows = g_ref.shape[0]
    g = g_ref[...] + gb_ref[...]
    lane = lax.broadcasted_iota(jnp.int32, (rows, LANES), 1)
    r = lax.broadcasted_iota(jnp.int32, (rows, LANES), 0)
    pad = (r & (seg - 1)) >= valid
    lf = jnp.where(pad, 0.0, jax.nn.log_sigmoid(g))
    ig = jnp.where(pad, NEG_INF, g)
    rr = lax.broadcasted_iota(jnp.int32, (rows, rows), 0)
    cc = lax.broadcasted_iota(jnp.int32, (rows, rows), 1)
    tri = _group_causal(rr, cc, seg).astype(BF16)
    hi, mid, lo = _split3(lf)
    bcum = (jnp.dot(tri, hi, preferred_element_type=F32)
            + jnp.dot(tri, mid, preferred_element_type=F32)
            + jnp.dot(tri, lo, preferred_element_type=F32))
    x = jnp.where(lane >= M_HEADS, bcum, ig)
    x_ref[...] = x
    xt_ref[...] = x.T


def _gate_prep(gates, gbias, layer, *, seg, valid, rows):
    m = gates.shape[0]
    return pl.pallas_call(
        functools.partial(_gate_prep_kernel, seg=seg, valid=valid),
        grid=(m // rows,),
        in_specs=[pl.BlockSpec((rows, LANES), lambda i: (i, 0)),
                  pl.BlockSpec((None, 1, LANES), lambda i: (layer, 0, 0))],
        out_specs=[pl.BlockSpec((rows, LANES), lambda i: (i, 0)), pl.BlockSpec((LANES, rows), lambda i: (0, i))],
        out_shape=[jax.ShapeDtypeStruct((m, LANES), F32), jax.ShapeDtypeStruct((LANES, m), F32)],
        compiler_params=_params(("parallel",)),
        name="gate_prep",
    )(gates, gbias)


def _mlstm_kernel(q_ref, k_ref, v_ref, o_ref, x_ref, irow_ref, brow_ref, nw_ref, c0_ref, n0_ref, m0_ref,
                  *rest, bb, seg, hp, n_prev):
    hg_ref, c_ref, n_ref, m_ref, cs, ns, ms = rest[n_prev:]
    h0 = pl.program_id(1) * hp
    c = pl.program_id(2)
    rows = bb * seg
    lane = lax.broadcasted_iota(jnp.int32, (rows, LANES), 1)

    @pl.when(c == 0)
    def _():
        cs[...] = c0_ref[...]
        ns[...] = n0_ref[...]
        ms[...] = jnp.broadcast_to(m0_ref[...], (bb, hp, 1, LANES))

    x = x_ref[...]
    rr = lax.broadcasted_iota(jnp.int32, (rows, rows), 0)
    cc = lax.broadcasted_iota(jnp.int32, (rows, rows), 1)
    visible = _group_causal(rr, cc, seg) > 0.0
    seg_id = lax.broadcasted_iota(jnp.int32, (rows, 1), 0) >> (seg.bit_length() - 1)

    def per_row(vals):
        out = vals[0]
        for bi in range(1, bb):
            out = jnp.where(seg_id == bi, vals[bi], out)
        return out

    ends = [(bi + 1) * seg - 1 for bi in range(bb)]
    for g in range(hp):
        h = h0 + g
        icol = jnp.sum(jnp.where(lane == h, x, 0.0), axis=1, keepdims=True)
        bcol = jnp.sum(jnp.where(lane == h + M_HEADS, x, 0.0), axis=1, keepdims=True)
        irow = irow_ref[g]
        brow = brow_ref[g]
        m_prev_b = [ms[bi, g][:, :1] for bi in range(bb)]
        b_last_b = [bcol[r:r + 1, :] for r in ends]
        m_prev = per_row(m_prev_b)
        b_last = per_row(b_last_b)
        n_rows = per_row([ns[bi, g] for bi in range(bb)])

        q = q_ref[:, g * M_DK:(g + 1) * M_DK].astype(F32) * (M_DK ** -0.5)
        k = k_ref[:, g * M_DK:(g + 1) * M_DK].astype(F32)
        qb = q.astype(BF16)
        kb = k.astype(BF16)
        vcols = slice(g * M_DV, (g + 1) * M_DV)
        vb = v_ref[:, vcols].astype(BF16)

        logd = jnp.where(visible, bcol - brow + irow, NEG_INF)
        inter = bcol + m_prev
        m_row = jnp.maximum(inter, jnp.max(logd, axis=1, keepdims=True))
        dmat = jnp.exp(logd - m_row)
        s = lax.dot_general(qb, kb, (((1,), (1,)), ((), ())), preferred_element_type=F32) * dmat
        w_inter = jnp.exp(inter - m_row)
        qc = [jnp.dot(qb[bi * seg:(bi + 1) * seg], cs[bi, g].astype(BF16), preferred_element_type=F32)
              for bi in range(bb)]
        qc = qc[0] if bb == 1 else jnp.concatenate(qc, axis=0)
        num = w_inter * qc + jnp.dot(s.astype(BF16), vb, preferred_element_type=F32)
        den = w_inter * jnp.sum(q * n_rows, axis=1, keepdims=True) + jnp.sum(s, axis=1, keepdims=True)
        hh = num / jnp.maximum(jnp.abs(den), jnp.exp(-m_row))
        mu = jnp.mean(hh, axis=1, keepdims=True)
        hc = hh - mu
        var = jnp.mean(hc * hc, axis=1, keepdims=True)
        hn = hc * lax.rsqrt(var + LN_EPS) * nw_ref[:, vcols]
        hg_ref[:, vcols] = (hn * jax.nn.sigmoid(o_ref[:, vcols].astype(F32))).astype(hg_ref.dtype)

        m_new_b = [m_row[r:r + 1, :] for r in ends]
        g_tok = jnp.exp(b_last - bcol + icol - per_row(m_new_b))
        kg = g_tok * k
        kgb = kg.astype(BF16)
        for bi in range(bb):
            lo_r, hi_r = bi * seg, (bi + 1) * seg
            g_state = jnp.exp(b_last_b[bi] + m_prev_b[bi] - m_new_b[bi])
            c_new = g_state * cs[bi, g] + lax.dot_general(
                kgb[lo_r:hi_r], vb[lo_r:hi_r], (((0,), (0,)), ((), ())), preferred_element_type=F32)
            n_new = g_state * ns[bi, g] + jnp.sum(kg[lo_r:hi_r], axis=0, keepdims=True)
            m_new = jnp.broadcast_to(m_new_b[bi], (1, LANES))
            cs[bi, g] = c_new
            ns[bi, g] = n_new
            ms[bi, g] = m_new
            c_ref[bi, g] = c_new
            n_ref[bi, g] = n_new
            m_ref[bi, g] = m_new


def _mlstm_scan(proj, gates, gbias, norm_w, c0, n0, m0, layer, prev, *, batch, nc, seg, valid, bb,
                hp=MLSTM_HEADS_PER_STEP):
    rows = bb * seg
    nb = batch // bb
    n_a = c0.shape[0]
    kq, kv = M_DK, M_DV
    xg, xgt = _gate_prep(gates, gbias, layer, seg=seg, valid=valid, rows=rows)
    xgt = xgt[:2 * M_HEADS].reshape(2 * M_HEADS, 1, -1)
    nh = M_HEADS // hp
    row_map = lambda col: (lambda i, h, c: (i * nc + c, col(h)))
    st_map = lambda i, h, c: (layer, i, h, 0, 0)
    in_specs = [
        pl.BlockSpec((rows, hp * kq), row_map(lambda h: h)),
        pl.BlockSpec((rows, hp * kq), row_map(lambda h: HK // (hp * kq) + h)),
        pl.BlockSpec((rows, hp * kv), row_map(lambda h: 2 * HK // (hp * kv) + h)),
        pl.BlockSpec((rows, hp * kv), row_map(lambda h: (2 * HK + HV) // (hp * kv) + h)),
        pl.BlockSpec((rows, LANES), lambda i, h, c: (i * nc + c, 0)),
        pl.BlockSpec((hp, 1, rows), lambda i, h, c: (h, 0, i * nc + c)),
        pl.BlockSpec((hp, 1, rows), lambda i, h, c: (h + nh, 0, i * nc + c)),
        pl.BlockSpec((None, 1, hp * kv), lambda i, h, c: (layer, 0, h)),
        pl.BlockSpec((None, bb, hp, kq, kv), st_map),
        pl.BlockSpec((None, bb, hp, 1, kq), st_map),
        pl.BlockSpec((None, bb, hp, 1, 1), st_map),
    ]
    args = [proj, proj, proj, proj, xg, xgt, xgt, norm_w, c0,
            n0.reshape(n_a, batch, M_HEADS, 1, kq), m0.reshape(n_a, batch, M_HEADS, 1, 1)]
    aliases = {}
    n_prev = 0
    if prev is not None:
        n_prev = len(prev)
        for t, arr in enumerate(prev):
            aliases[len(args)] = 1 + t
            in_specs.append(pl.BlockSpec(memory_space=pl.ANY))
            args.append(arr)
    hg, c_out, n_out, m_out = pl.pallas_call(
        functools.partial(_mlstm_kernel, bb=bb, seg=seg, hp=hp, n_prev=n_prev),
        grid=(nb, nh, nc),
        in_specs=in_specs,
        out_specs=[
            pl.BlockSpec((rows, hp * kv), lambda i, h, c: (i * nc + c, h)),
            pl.BlockSpec((None, bb, hp, kq, kv), st_map),
            pl.BlockSpec((None, bb, hp, 1, kq), st_map),
            pl.BlockSpec((None, bb, hp, 1, LANES), st_map),
        ],
        out_shape=[
            jax.ShapeDtypeStruct((batch * nc * seg, HV), BF16),
            jax.ShapeDtypeStruct((n_a, batch, M_HEADS, kq, kv), F32),
            jax.ShapeDtypeStruct((n_a, batch, M_HEADS, 1, kq), F32),
            jax.ShapeDtypeStruct((n_a, batch, M_HEADS, 1, LANES), F32),
        ],
        scratch_shapes=[
            pltpu.VMEM((bb, hp, kq, kv), F32),
            pltpu.VMEM((bb, hp, 1, kq), F32),
            pltpu.VMEM((bb, hp, 1, LANES), F32),
        ],
        input_output_aliases=aliases,
        compiler_params=_params(("parallel", "parallel", "arbitrary")),
        name="mlstm_scan",
    )(*args)
    return hg, (c_out, n_out, m_out)


def _mix_kernel(h_ref, w_ref, bias_ref, vw_ref, um_ref, *maybe_v_ref, lc):
    t = h_ref.shape[0]
    vraw = h_ref[:, S_WIDTH:].astype(F32)
    mu = jnp.mean(vraw, axis=-1, keepdims=True)
    vc = vraw - mu
    var = jnp.mean(vc * vc, axis=-1, keepdims=True)
    v = vc * lax.rsqrt(var + LN_EPS) * vw_ref[...]
    for v_ref in maybe_v_ref:
        v_ref[...] = v
    vb = v.astype(BF16)
    rr = lax.broadcasted_iota(jnp.int32, (t, t), 0)
    cc = lax.broadcasted_iota(jnp.int32, (t, t), 1)
    keep = _group_causal(rr, cc, lc)
    for g in range(S_GROUPS):
        sl = slice(g * S_DG, (g + 1) * S_DG)
        wg = (w_ref[g] * keep).astype(BF16)
        mixed = jnp.dot(wg, vb[:, sl], preferred_element_type=F32) + bias_ref[:, g:g + 1]
        um_ref[:, sl] = (h_ref[:, sl].astype(F32) * mixed).astype(BF16)


def _mix(hdn, wmix, bias_t, vnorm_w, layer, *, lc, emit_v):
    m = hdn.shape[0]
    t = S_CHUNK
    row_spec = pl.BlockSpec((t, S_WIDTH), lambda i: (i, 0))
    out_specs = [row_spec, row_spec] if emit_v else [row_spec]
    out_shape = [jax.ShapeDtypeStruct((m, S_WIDTH), BF16)]
    if emit_v:
        out_shape.append(jax.ShapeDtypeStruct((m, S_WIDTH), F32))
    outs = pl.pallas_call(
        functools.partial(_mix_kernel, lc=lc),
        grid=(m // t,),
        in_specs=[
            pl.BlockSpec((t, 2 * S_WIDTH), lambda i: (i, 0)),
            pl.BlockSpec((None, S_GROUPS, t, t), lambda i: (layer, 0, 0, 0)),
            pl.BlockSpec((None, t, S_GROUPS), lambda i: (layer, 0, 0)),
            pl.BlockSpec((None, 1, S_WIDTH), lambda i: (layer, 0, 0)),
        ],
        out_specs=out_specs,
        out_shape=out_shape,
        compiler_params=_params(("parallel",)),
        name="chunk_mix",
    )(hdn, wmix, bias_t, vnorm_w)
    return (outs[0], outs[1]) if emit_v else (outs[0], None)


def _route_kernel(xb_ref, wq_ref, keys_ref, r1_ref, e1_ref, nn_ref, c0_ref, q_s, sc_s, wk_s, rk_s, sv_s):
    tm = xb_ref.shape[0]
    q_s[...] = jnp.dot(xb_ref[...], wq_ref[...], preferred_element_type=F32)
    kidx = lax.broadcasted_iota(jnp.int32, (P_NKEYS, tm), 0).astype(F32)
    for h in range(P_HEADS):
        for c in range(2):
            hc = 2 * h + c
            qhc = q_s[:, hc * P_DH:(hc + 1) * P_DH].astype(BF16)
            sc = lax.dot_general(keys_ref[c], qhc, (((1,), (1,)), ((), ())), preferred_element_type=F32)
            sc_s[hc] = sc

        def top16(exact, h=h):
            for c in range(2):
                wk_s[c] = sc_s[2 * h + c]
                rk_s[2 * h + c] = jnp.full((P_NKEYS, tm), float(P_NKEYS), F32)

            def body(kk, carry):
                for c in range(2):
                    hc = 2 * h + c
                    s = wk_s[c]
                    mx = jnp.max(s, axis=0, keepdims=True)
                    sel = s == mx
                    if exact:
                        idx = jnp.min(jnp.where(sel, kidx, float(P_NKEYS)), axis=0, keepdims=True)
                        sel = kidx == idx
                    wk_s[c] = jnp.where(sel, NEG_INF, s)
                    rk_s[hc] = jnp.where(sel, lax.convert_element_type(kk, F32), rk_s[hc])
                    sv_s[c, kk, h:h + 1, :] = mx
                return carry

            lax.fori_loop(0, P_TOPK, body, 0)

        top16(exact=False)
        removed = sum(jnp.sum(jnp.where(wk_s[c] == NEG_INF, 1.0, 0.0), axis=0, keepdims=True)
                      for c in range(2))
        tied = jnp.max(jnp.abs(removed - 2.0 * P_TOPK)) > 0.0

        @pl.when(tied)
        def _(top16=top16):
            top16(exact=True)

    sv0 = [sv_s[0, k] for k in range(P_TOPK)]
    sv1 = [sv_s[1, k] for k in range(P_TOPK)]
    cnt = [jnp.zeros((P_HEADS, tm), F32) for _ in range(P_TOPK)]
    front = [sv0[k] + sv1[0] for k in range(P_TOPK)]
    top = front[0]
    z = jnp.zeros((P_HEADS, tm), F32)
    for r in range(P_TOPK):
        live = min(r + 1, P_TOPK)
        mx = front[0]
        for k in range(1, live):
            mx = jnp.maximum(mx, front[k])
        pick = jnp.full((P_HEADS, tm), float(P_TOPK), F32)
        for k in reversed(range(live)):
            pick = jnp.where(front[k] == mx, float(k), pick)
        z = z + jnp.exp(mx - top)
        hits = [pick == float(k) for k in range(live)]
        newcnt = jnp.zeros((P_HEADS, tm), F32)
        for k in range(live):
            cnt[k] = cnt[k] + jnp.where(hits[k], 1.0, 0.0)
            newcnt = jnp.where(hits[k], cnt[k], newcnt)
        nxt = jnp.full((P_HEADS, tm), NEG_INF, F32)
        for j in range(1, min(r + 2, P_TOPK)):
            nxt = jnp.where(newcnt == float(j), sv1[j], nxt)
        for k in range(live):
            front[k] = jnp.where(hits[k], sv0[k] + nxt, front[k])
    zinv = 1.0 / z

    for h in range(P_HEADS):
        rank0 = rk_s[2 * h]
        nn = jnp.zeros((P_NKEYS, tm), F32)
        for k in range(P_TOPK):
            nn = jnp.where(rank0 == float(k), cnt[k][h:h + 1, :], nn)
        nn_ref[h] = nn
        c0_ref[h] = jnp.exp(sc_s[2 * h] - sv0[0][h:h + 1, :]) * zinv[h:h + 1, :]
        r1_ref[h] = pltpu.bitcast(rk_s[2 * h + 1].astype(BF16), jnp.int32)
        e1_ref[h] = pltpu.bitcast(jnp.exp(sc_s[2 * h + 1] - sv1[0][h:h + 1, :]).astype(BF16), jnp.int32)


def _route(xb, wq, keys, li, *, tm=256):
    m, k = xb.shape
    tab = jax.ShapeDtypeStruct((P_HEADS, P_NKEYS, m), F32)
    tab16 = jax.ShapeDtypeStruct((P_HEADS, P_NKEYS // 2, m), jnp.int32)
    tab_spec = pl.BlockSpec((P_HEADS, P_NKEYS, tm), lambda i: (0, 0, i))
    tab16_spec = pl.BlockSpec((P_HEADS, P_NKEYS // 2, tm), lambda i: (0, 0, i))
    return pl.pallas_call(
        _route_kernel,
        grid=(m // tm,),
        in_specs=[
            pl.BlockSpec((tm, k), lambda i: (i, 0)),
            pl.BlockSpec((None, k, P_HEADS * 2 * P_DH), lambda i: (li, 0, 0)),
            pl.BlockSpec((None, 2, P_NKEYS, P_DH), lambda i: (li, 0, 0, 0)),
        ],
        out_specs=[tab16_spec, tab16_spec, tab_spec, tab_spec],
        out_shape=[tab16, tab16, tab, tab],
        scratch_shapes=[
            pltpu.VMEM((tm, P_HEADS * 2 * P_DH), F32),
            pltpu.VMEM((2 * P_HEADS, P_NKEYS, tm), F32),
            pltpu.VMEM((2, P_NKEYS, tm), F32),
            pltpu.VMEM((2 * P_HEADS, P_NKEYS, tm), F32),
            pltpu.VMEM((2, P_TOPK, P_HEADS, tm), F32),
        ],
        compiler_params=_params(("parallel",)),
        name="peer_route",
    )(xb, wq, keys)


def _gelu_tanh(x):
    c = 0.7978845608028654
    inner = x + x * x * x * 0.044715
    return x * 0.5 * (jnp.tanh(inner * c) + 1.0)


def _peer_kernel(xb_ref, u_ref, v_ref, r1_ref, e1_ref, nn_ref, c0_ref, xf_ref, g_ref, b_ref,
                 of_ref, ob_ref, acc_s, st_s, act_s, xt_s, *, te):
    e = pl.program_id(1)
    last = pl.num_programs(1) - 1
    n_i1 = te // P_NKEYS
    tb = st_s.shape[1]

    @pl.when(e == 0)
    def _():
        acc_s[...] = jnp.zeros_like(acc_s)
        xt_s[...] = xb_ref[...].T

    st_s[...] = jnp.dot(u_ref[...], xt_s[...], preferred_element_type=F32)
    base = e * n_i1
    shape3 = (P_NKEYS // BF16_ROWS, BF16_ROWS, LANES)
    zero = jnp.zeros(shape3, BF16)
    for j in range(n_i1):
        rows = slice(j * P_NKEYS, (j + 1) * P_NKEYS)
        for t in range(tb // LANES):
            cols = slice(t * LANES, (t + 1) * LANES)
            gate = zero
            for h in range(P_HEADS):
                nn = nn_ref[h, pl.ds(base + j, 1), :][:, cols]
                c0 = c0_ref[h, pl.ds(base + j, 1), :][:, cols]
                nn = jnp.broadcast_to(nn, (BF16_ROWS, LANES)).astype(BF16)[None]
                c0 = jnp.broadcast_to(c0, (BF16_ROWS, LANES)).astype(BF16)[None]
                r1 = pltpu.bitcast(r1_ref[h, :, cols], BF16).reshape(shape3)
                e1 = pltpu.bitcast(e1_ref[h, :, cols], BF16).reshape(shape3)
                gate = gate + jnp.where(r1 < nn, e1, zero) * c0
            act = _gelu_tanh(st_s[rows, cols].astype(BF16).reshape(shape3)) * gate
            act_s[rows, cols] = act.reshape(P_NKEYS, LANES)
    acc_s[...] += lax.dot_general(act_s[...], v_ref[...], (((0,), (0,)), ((), ())),
                                  preferred_element_type=F32)

    @pl.when(e == last)
    def _():
        o = _layer_norm_rows(ALPHA * xf_ref[...] + acc_s[...], g_ref[...], b_ref[...])
        of_ref[...] = o
        ob_ref[...] = o.astype(BF16)


def _peer(xb, xf, u, v, r1, e1, nn, c0, ln_g, ln_b, li, *, tb=PEER_TB, te=PEER_TE):
    m, d = xf.shape
    ne = P_NEXP // te
    once = pl.Buffered(1)
    tab_spec = pl.BlockSpec((P_HEADS, P_NKEYS, tb), lambda i, e: (0, 0, i), pipeline_mode=once)
    tab16_spec = pl.BlockSpec((P_HEADS, P_NKEYS // 2, tb), lambda i, e: (0, 0, i), pipeline_mode=once)
    return pl.pallas_call(
        functools.partial(_peer_kernel, te=te),
        grid=(m // tb, ne),
        in_specs=[
            pl.BlockSpec((tb, d), lambda i, e: (i, 0), pipeline_mode=once),
            pl.BlockSpec((None, te, d), lambda i, e: (li, e, 0)),
            pl.BlockSpec((None, te, d), lambda i, e: (li, e, 0)),
            tab16_spec, tab16_spec, tab_spec, tab_spec,
            pl.BlockSpec((tb, d), lambda i, e: (i, 0), pipeline_mode=once),
            _ln_spec(d, li, 1),
            _ln_spec(d, li, 1),
        ],
        out_specs=[pl.BlockSpec((tb, d), lambda i, e: (i, 0)), pl.BlockSpec((tb, d), lambda i, e: (i, 0))],
        out_shape=[jax.ShapeDtypeStruct((m, d), F32), jax.ShapeDtypeStruct((m, d), BF16)],
        scratch_shapes=[
            pltpu.VMEM((tb, d), F32),
            pltpu.VMEM((te, tb), F32),
            pltpu.VMEM((te, tb), BF16),
            pltpu.VMEM((d, tb), BF16),
        ],
        compiler_params=_params(("parallel", "arbitrary")),
        name="peer_dense",
    )(xb, u, v, r1, e1, nn, c0, xf, ln_g, ln_b)


def _trunk(x, p, c_in, n_in, m_in, *, batch, seq, wts, want_v):
    m = batch * seq
    xf = x
    xb = x.astype(BF16)
    decode = seq < MLSTM_PAD
    lc = min(seq, S_CHUNK)
    states = None
    vs = []
    for i in range(DEPTH):
        j = i // N_MIXERS
        if i % N_MIXERS == 0:
            if decode:
                seg, valid, nc, bb = MLSTM_PAD, seq, 1, 8
                xin = jnp.pad(xb.reshape(batch, seq, D_MODEL), ((0, 0), (0, seg - seq), (0, 0)))
                xin = xin.reshape(batch * seg, D_MODEL)
            else:
                seg, valid, nc, bb = MLSTM_CHUNK, MLSTM_CHUNK, seq // MLSTM_CHUNK, 1
                xin = xb
            proj = _mm(xin, wts["a_in"], j, n=A_MAIN, out_dtype=BF16)
            gates = _mm(xin, wts["a_gate"], j)
            hg, states = _mlstm_scan(proj, gates, wts["a_gbias"], wts["a_norm"], c_in, n_in, m_in, j, states,
                                     batch=batch, nc=nc, seg=seg, valid=valid, bb=bb)
            if decode:
                hg = hg.reshape(batch, seg, HV)[:, :seq].reshape(m, HV)
            w_out = wts["a_out"]
        else:
            hdn = _mm(xb, wts["b_in"], j, wts["b_in_bias"], act="gelu", out_dtype=BF16)
            hg, v = _mix(hdn, wts["b_mix"][lc], wts["b_mix_bias"][lc], wts["b_norm"], j, lc=lc,
                         emit_v=want_v)
            vs.append(v)
            w_out = wts["b_out"]
        x1f, x1b = _mm_ln(hg, w_out, j, xf, wts["ln_g"], wts["ln_b"], i)
        r1, e1, nn, c0 = _route(x1b, wts["peer_wq"], wts["peer_keys"], i)
        x2f, x2b = _peer(x1b, x1f, wts["peer_u"], wts["peer_v"], r1, e1, nn, c0, wts["ln_g"], wts["ln_b"], i)
        xf, xb = _ple(x2f, x2b, wts["ple_gate"], p, wts["ple_w"], i)
    c_out, n_out, m_out = states
    return xf, c_out, n_out[:, :, :, 0, :], m_out[:, :, :, 0, 0], vs


def _prep_weights(w_a_in, b_a_gate, a_norm_w, w_a_out, w_b_in, b_b_in, b_norm_w, w_b_s, b_b_s, w_b_out,
                  ln_g, ln_b, peer_wq, peer_keys, peer_u, peer_v, ple_w, ple_gate_w, chunk_lens):
    n_a = w_a_in.shape[0]
    n_b = w_b_in.shape[0]
    gpad = LANES - 2 * M_HEADS
    wts = {
        "a_in": w_a_in.astype(BF16),
        "a_gate": jnp.pad(w_a_in[:, :, A_MAIN:], ((0, 0), (0, 0), (0, gpad))).astype(BF16),
        "a_gbias": jnp.pad(b_a_gate, ((0, 0), (0, gpad))).reshape(n_a, 1, LANES).astype(F32),
        "a_norm": a_norm_w.reshape(n_a, 1, HV).astype(F32),
        "a_out": w_a_out.astype(BF16),
        "b_in": w_b_in.astype(BF16),
        "b_in_bias": b_b_in.reshape(n_b, 1, -1).astype(F32),
        "b_norm": b_norm_w.reshape(n_b, 1, S_WIDTH).astype(F32),
        "b_out": w_b_out.astype(BF16),
        "b_mix": {},
        "b_mix_bias": {},
        "ln_g": ln_g.reshape(DEPTH, 2, 1, D_MODEL).astype(F32),
        "ln_b": ln_b.reshape(DEPTH, 2, 1, D_MODEL).astype(F32),
        "peer_wq": peer_wq.astype(BF16),
        "peer_keys": peer_keys.astype(BF16),
        "peer_u": peer_u.astype(BF16),
        "peer_v": peer_v.astype(BF16),
        "ple_w": ple_w.astype(BF16),
        "ple_gate": ple_gate_w.astype(BF16),
    }
    for lc in chunk_lens:
        rep = S_CHUNK // lc
        if rep == 1:
            wts["b_mix"][lc] = w_b_s.astype(F32)
        else:
            wts["b_mix"][lc] = jnp.tile(w_b_s[:, :, :lc, :lc], (1, 1, rep, rep)).astype(F32)
        wts["b_mix_bias"][lc] = jnp.swapaxes(jnp.tile(b_b_s[:, :, :lc], (1, 1, rep)), 1, 2).astype(F32)
    return wts


def kernel(x_prompt, x_sample, state_C, state_n, state_m, p_prompt, p_sample, w_a_in, b_a_gate, a_norm_w, w_a_out, w_b_in, b_b_in, b_norm_w, w_b_s, b_b_s, w_b_out, ln_g, ln_b, peer_wq, peer_keys, peer_u, peer_v, ple_w, ple_gate_w):
    bp, sp, d = x_prompt.shape
    bs, ss, _ = x_sample.shape
    n_a = state_C.shape[0]
    chunk_lens = sorted({min(sp, S_CHUNK), min(ss, S_CHUNK)})
    wts = _prep_weights(w_a_in, b_a_gate, a_norm_w, w_a_out, w_b_in, b_b_in, b_norm_w, w_b_s, b_b_s, w_b_out,
                        ln_g, ln_b, peer_wq, peer_keys, peer_u, peer_v, ple_w, ple_gate_w, chunk_lens)

    c0 = jnp.zeros((n_a, bp, M_HEADS, M_DK, M_DV), F32)
    n0 = jnp.zeros((n_a, bp, M_HEADS, M_DK), F32)
    m0 = jnp.zeros((n_a, bp, M_HEADS), F32)
    yp, pc, pn, pm, _ = _trunk(x_prompt.reshape(bp * sp, d), p_prompt.reshape(DEPTH, bp * sp, -1),
                               c0, n0, m0, batch=bp, seq=sp, wts=wts, want_v=False)
    ys, sc, sn, sm, sv = _trunk(x_sample.reshape(bs * ss, d), p_sample.reshape(DEPTH, bs * ss, -1),
                                state_C, state_n, state_m, batch=bs, seq=ss, wts=wts, want_v=True)
    return (yp.reshape(bp, sp, d), ys.reshape(bs, ss, d), pc, pn, pm, sc, sn, sm,
            jnp.stack([v.reshape(bs, ss, S_WIDTH) for v in sv]))
```

```python
import functools

import jax
import jax.numpy as jnp
from jax import lax
from jax.experimental import pallas as pl
from jax.experimental.pallas import tpu as pltpu

F32 = jnp.float32
BF16 = jnp.bfloat16

D_MODEL = 2048
DEPTH = 4
N_MIXERS = 2
M_HEADS = 8
M_DV = D_MODEL // M_HEADS
M_DK = M_DV // 2
HK = M_HEADS * M_DK
HV = M_HEADS * M_DV
A_MAIN = 2 * HK + 2 * HV
S_GROUPS = 8
S_CHUNK = 128
S_WIDTH = D_MODEL
S_DG = S_WIDTH // S_GROUPS
P_HEADS = 8
P_NKEYS = 128
P_NEXP = P_NKEYS * P_NKEYS
P_DH = 128
P_TOPK = 16
ALPHA = (2 * DEPTH) ** 0.25
LN_EPS = 1e-5

LANES = 128
BF16_ROWS = 16
VMEM_LIMIT = 56 * 1024 * 1024
NEG_INF = float("-inf")
TOPK_MARK = 2.0 ** 120

MLSTM_CHUNK = 256
MLSTM_PAD = 16
MLSTM_HEADS_PER_STEP = 4
PEER_TB = 512
PEER_TE = 1024
MM_ROWS = 1024
RES_ROWS = 512


def _params(sem):
    return pltpu.CompilerParams(dimension_semantics=sem, vmem_limit_bytes=VMEM_LIMIT)


def _layer_norm_rows(z, g, b):
    mu = jnp.mean(z, axis=-1, keepdims=True)
    zc = z - mu
    var = jnp.mean(zc * zc, axis=-1, keepdims=True)
    return zc * lax.rsqrt(var + LN_EPS) * g + b


def _mm_kernel(*refs, act, has_bias):
    if has_bias:
        x_ref, w_ref, b_ref, o_ref = refs
    else:
        x_ref, w_ref, o_ref = refs
    acc = jnp.dot(x_ref[...].astype(BF16), w_ref[...], preferred_element_type=F32)
    if has_bias:
        acc = acc + b_ref[...]
    if act == "gelu":
        acc = jax.nn.gelu(acc)
    o_ref[...] = acc.astype(o_ref.dtype)


def _mm(x, w, layer, bias=None, *, n=None, act=None, out_dtype=F32, tm=None, tn=512):
    m, k = x.shape
    n = w.shape[2] if n is None else n
    tn = min(tn, n)
    if tm is None:
        tm = MM_ROWS if m % MM_ROWS == 0 else m
    in_specs = [
        pl.BlockSpec((tm, k), lambda i, j: (i, 0)),
        pl.BlockSpec((None, k, tn), lambda i, j: (layer, 0, j)),
    ]
    args = [x, w]
    if bias is not None:
        in_specs.append(pl.BlockSpec((None, 1, tn), lambda i, j: (layer, 0, j)))
        args.append(bias)
    return pl.pallas_call(
        functools.partial(_mm_kernel, act=act, has_bias=bias is not None),
        grid=(m // tm, n // tn),
        in_specs=in_specs,
        out_specs=pl.BlockSpec((tm, tn), lambda i, j: (i, j)),
        out_shape=jax.ShapeDtypeStruct((m, n), out_dtype),
        compiler_params=_params(("parallel", "parallel")),
        name="mm",
    )(*args)


def _mm_ln_kernel(a_ref, w_ref, r_ref, g_ref, b_ref, of_ref, ob_ref):
    y = jnp.dot(a_ref[...].astype(BF16), w_ref[...], preferred_element_type=F32)
    o = _layer_norm_rows(ALPHA * r_ref[...] + y, g_ref[...], b_ref[...])
    of_ref[...] = o
    ob_ref[...] = o.astype(BF16)


def _ln_spec(n, li, s):
    return pl.BlockSpec((None, None, 1, n), lambda *_: (li, s, 0, 0))


def _mm_ln(a, w, layer, res, ln_g, ln_b, li, *, tm=RES_ROWS):
    m, k = a.shape
    n = w.shape[2]
    return pl.pallas_call(
        _mm_ln_kernel,
        grid=(m // tm,),
        in_specs=[
            pl.BlockSpec((tm, k), lambda i: (i, 0)),
            pl.BlockSpec((None, k, n), lambda i: (layer, 0, 0)),
            pl.BlockSpec((tm, n), lambda i: (i, 0)),
            _ln_spec(n, li, 0),
            _ln_spec(n, li, 0),
        ],
        out_specs=[pl.BlockSpec((tm, n), lambda i: (i, 0)), pl.BlockSpec((tm, n), lambda i: (i, 0))],
        out_shape=[jax.ShapeDtypeStruct((m, n), F32), jax.ShapeDtypeStruct((m, n), BF16)],
        compiler_params=_params(("parallel",)),
        name="mm_ln",
    )(a, w, res, ln_g, ln_b)


def _ple_kernel(xf_ref, xb_ref, gw_ref, p_ref, pw_ref, of_ref, ob_ref):
    gate = jax.nn.sigmoid(jnp.dot(xb_ref[...], gw_ref[...], preferred_element_type=F32))
    pe = jnp.dot(p_ref[...].astype(BF16), pw_ref[...], preferred_element_type=F32)
    o = xf_ref[...] + gate * pe
    of_ref[...] = o
    ob_ref[...] = o.astype(BF16)


def _ple(xf, xb, gw, p, pw, li, *, tm=RES_ROWS):
    m, n = xf.shape
    kp = p.shape[2]
    return pl.pallas_call(
        _ple_kernel,
        grid=(m // tm,),
        in_specs=[
            pl.BlockSpec((tm, n), lambda i: (i, 0)),
            pl.BlockSpec((tm, n), lambda i: (i, 0)),
            pl.BlockSpec((None, n, n), lambda i: (li, 0, 0)),
            pl.BlockSpec((None, tm, kp), lambda i: (li, i, 0)),
            pl.BlockSpec((None, kp, n), lambda i: (li, 0, 0)),
        ],
        out_specs=[pl.BlockSpec((tm, n), lambda i: (i, 0)), pl.BlockSpec((tm, n), lambda i: (i, 0))],
        out_shape=[jax.ShapeDtypeStruct((m, n), F32), jax.ShapeDtypeStruct((m, n), BF16)],
        compiler_params=_params(("parallel",)),
        name="ple",
    )(xf, xb, gw, p, pw)


def _group_causal(rr, cc, group):
    shift = group.bit_length() - 1
    same = (rr >> shift) == (cc >> shift)
    return jnp.where(rr >= cc, jnp.where(same, 1.0, 0.0), 0.0)


def _split3(x):
    hi = x.astype(BF16)
    r1 = x - hi.astype(F32)
    mid = r1.astype(BF16)
    lo = (r1 - mid.astype(F32)).astype(BF16)
    return hi, mid, lo


def _gate_prep_kernel(g_ref, gb_ref, x_ref, xt_ref, *, seg, valid):
    rows = g_ref.shape[0]
    g = g_ref[...] + gb_ref[...]
    lane = lax.broadcasted_iota(jnp.int32, (rows, LANES), 1)
    r = lax.broadcasted_iota(jnp.int32, (rows, LANES), 0)
    pad = (r & (seg - 1)) >= valid
    lf = jnp.where(pad, 0.0, jax.nn.log_sigmoid(g))
    ig = jnp.where(pad, NEG_INF, g)
    rr = lax.broadcasted_iota(jnp.int32, (rows, rows), 0)
    cc = lax.broadcasted_iota(jnp.int32, (rows, rows), 1)
    tri = _group_causal(rr, cc, seg).astype(BF16)
    hi, mid, lo = _split3(lf)
    bcum = (jnp.dot(tri, hi, preferred_element_type=F32)
            + jnp.dot(tri, mid, preferred_element_type=F32)
            + jnp.dot(tri, lo, preferred_element_type=F32))
    x = jnp.where(lane >= M_HEADS, bcum, ig)
    x_ref[...] = x
    xt_ref[...] = x.T


def _gate_prep(gates, gbias, layer, *, seg, valid, rows):
    m = gates.shape[0]
    return pl.pallas_call(
        functools.partial(_gate_prep_kernel, seg=seg, valid=valid),
        grid=(m // rows,),
        in_specs=[pl.BlockSpec((rows, LANES), lambda i: (i, 0)),
                  pl.BlockSpec((None, 1, LANES), lambda i: (layer, 0, 0))],
        out_specs=[pl.BlockSpec((rows, LANES), lambda i: (i, 0)), pl.BlockSpec((LANES, rows), lambda i: (0, i))],
        out_shape=[jax.ShapeDtypeStruct((m, LANES), F32), jax.ShapeDtypeStruct((LANES, m), F32)],
        compiler_params=_params(("parallel",)),
        name="gate_prep",
    )(gates, gbias)


def _mlstm_kernel(q_ref, k_ref, v_ref, o_ref, x_ref, irow_ref, brow_ref, nw_ref, c0_ref, n0_ref, m0_ref,
                  *rest, bb, seg, hp, n_prev):
    hg_ref, c_ref, n_ref, m_ref, cs, ns, ms = rest[n_prev:]
    h0 = pl.program_id(1) * hp
    c = pl.program_id(2)
    rows = bb * seg
    lane = lax.broadcasted_iota(jnp.int32, (rows, LANES), 1)

    @pl.when(c == 0)
    def _():
        cs[...] = c0_ref[...]
        ns[...] = n0_ref[...]
        ms[...] = jnp.broadcast_to(m0_ref[...], (bb, hp, 1, LANES))

    x = x_ref[...]
    rr = lax.broadcasted_iota(jnp.int32, (rows, rows), 0)
    cc = lax.broadcasted_iota(jnp.int32, (rows, rows), 1)
    visible = _group_causal(rr, cc, seg) > 0.0
    seg_id = lax.broadcasted_iota(jnp.int32, (rows, 1), 0) >> (seg.bit_length() - 1)

    def per_row(vals):
        out = vals[0]
        for bi in range(1, bb):
            out = jnp.where(seg_id == bi, vals[bi], out)
        return out

    ends = [(bi + 1) * seg - 1 for bi in range(bb)]
    for g in range(hp):
        h = h0 + g
        icol = jnp.sum(jnp.where(lane == h, x, 0.0), axis=1, keepdims=True)
        bcol = jnp.sum(jnp.where(lane == h + M_HEADS, x, 0.0), axis=1, keepdims=True)
        irow = irow_ref[g]
        brow = brow_ref[g]
        m_prev_b = [ms[bi, g][:, :1] for bi in range(bb)]
        b_last_b = [bcol[r:r + 1, :] for r in ends]
        m_prev = per_row(m_prev_b)
        b_last = per_row(b_last_b)
        n_rows = per_row([ns[bi, g] for bi in range(bb)])

        q = q_ref[:, g * M_DK:(g + 1) * M_DK].astype(F32) * (M_DK ** -0.5)
        k = k_ref[:, g * M_DK:(g + 1) * M_DK].astype(F32)
        qb = q.astype(BF16)
        kb = k.astype(BF16)
        vcols = slice(g * M_DV, (g + 1) * M_DV)
        vb = v_ref[:, vcols].astype(BF16)

        logd = jnp.where(visible, bcol - brow + irow, NEG_INF)
        inter = bcol + m_prev
        m_row = jnp.maximum(inter, jnp.max(logd, axis=1, keepdims=True))
        dmat = jnp.exp(logd - m_row)
        s = lax.dot_general(qb, kb, (((1,), (1,)), ((), ())), preferred_element_type=F32) * dmat
        w_inter = jnp.exp(inter - m_row)
        qc = [jnp.dot(qb[bi * seg:(bi + 1) * seg], cs[bi, g].astype(BF16), preferred_element_type=F32)
              for bi in range(bb)]
        qc = qc[0] if bb == 1 else jnp.concatenate(qc, axis=0)
        num = w_inter * qc + jnp.dot(s.astype(BF16), vb, preferred_element_type=F32)
        den = w_inter * jnp.sum(q * n_rows, axis=1, keepdims=True) + jnp.sum(s, axis=1, keepdims=True)
        hh = num / jnp.maximum(jnp.abs(den), jnp.exp(-m_row))
        mu = jnp.mean(hh, axis=1, keepdims=True)
        hc = hh - mu
        var = jnp.mean(hc * hc, axis=1, keepdims=True)
        hn = hc * lax.rsqrt(var + LN_EPS) * nw_ref[:, vcols]
        hg_ref[:, vcols] = (hn * jax.nn.sigmoid(o_ref[:, vcols].astype(F32))).astype(hg_ref.dtype)

        m_new_b = [m_row[r:r + 1, :] for r in ends]
        g_tok = jnp.exp(b_last - bcol + icol - per_row(m_new_b))
        kg = g_tok * k
        kgb = kg.astype(BF16)
        for bi in range(bb):
            lo_r, hi_r = bi * seg, (bi + 1) * seg
            g_state = jnp.exp(b_last_b[bi] + m_prev_b[bi] - m_new_b[bi])
            c_new = g_state * cs[bi, g] + lax.dot_general(
                kgb[lo_r:hi_r], vb[lo_r:hi_r], (((0,), (0,)), ((), ())), preferred_element_type=F32)
            n_new = g_state * ns[bi, g] + jnp.sum(kg[lo_r:hi_r], axis=0, keepdims=True)
            m_new = jnp.broadcast_to(m_new_b[bi], (1, LANES))
            cs[bi, g] = c_new
            ns[bi, g] = n_new
            ms[bi, g] = m_new
            c_ref[bi, g] = c_new
            n_ref[bi, g] = n_new
            m_ref[bi, g] = m_new


def _mlstm_scan(proj, gates, gbias, norm_w, c0, n0, m0, layer, prev, *, batch, nc, seg, valid, bb,
                hp=MLSTM_HEADS_PER_STEP):
    rows = bb * seg
    nb = batch // bb
    n_a = c0.shape[0]
    kq, kv = M_DK, M_DV
    xg, xgt = _gate_prep(gates, gbias, layer, seg=seg, valid=valid, rows=rows)
    xgt = xgt[:2 * M_HEADS].reshape(2 * M_HEADS, 1, -1)
    nh = M_HEADS // hp
    row_map = lambda col: (lambda i, h, c: (i * nc + c, col(h)))
    st_map = lambda i, h, c: (layer, i, h, 0, 0)
    in_specs = [
        pl.BlockSpec((rows, hp * kq), row_map(lambda h: h)),
        pl.BlockSpec((rows, hp * kq), row_map(lambda h: HK // (hp * kq) + h)),
        pl.BlockSpec((rows, hp * kv), row_map(lambda h: 2 * HK // (hp * kv) + h)),
        pl.BlockSpec((rows, hp * kv), row_map(lambda h: (2 * HK + HV) // (hp * kv) + h)),
        pl.BlockSpec((rows, LANES), lambda i, h, c: (i * nc + c, 0)),
        pl.BlockSpec((hp, 1, rows), lambda i, h, c: (h, 0, i * nc + c)),
        pl.BlockSpec((hp, 1, rows), lambda i, h, c: (h + nh, 0, i * nc + c)),
        pl.BlockSpec((None, 1, hp * kv), lambda i, h, c: (layer, 0, h)),
        pl.BlockSpec((None, bb, hp, kq, kv), st_map),
        pl.BlockSpec((None, bb, hp, 1, kq), st_map),
        pl.BlockSpec((None, bb, hp, 1, 1), st_map),
    ]
    args = [proj, proj, proj, proj, xg, xgt, xgt, norm_w, c0,
            n0.reshape(n_a, batch, M_HEADS, 1, kq), m0.reshape(n_a, batch, M_HEADS, 1, 1)]
    aliases = {}
    n_prev = len(prev)
    for t, arr in enumerate(prev):
        aliases[len(args)] = 1 + t
        in_specs.append(pl.BlockSpec(memory_space=pl.ANY))
        args.append(arr)
    hg, c_out, n_out, m_out = pl.pallas_call(
        functools.partial(_mlstm_kernel, bb=bb, seg=seg, hp=hp, n_prev=n_prev),
        grid=(nb, nh, nc),
        in_specs=in_specs,
        out_specs=[
            pl.BlockSpec((rows, hp * kv), lambda i, h, c: (i * nc + c, h)),
            pl.BlockSpec((None, bb, hp, kq, kv), st_map),
            pl.BlockSpec((None, bb, hp, 1, kq), st_map),
            pl.BlockSpec((None, bb, hp, 1, LANES), st_map),
        ],
        out_shape=[
            jax.ShapeDtypeStruct((batch * nc * seg, HV), BF16),
            jax.ShapeDtypeStruct((n_a, batch, M_HEADS, kq, kv), F32),
            jax.ShapeDtypeStruct((n_a, batch, M_HEADS, 1, kq), F32),
            jax.ShapeDtypeStruct((n_a, batch, M_HEADS, 1, LANES), F32),
        ],
        scratch_shapes=[
            pltpu.VMEM((bb, hp, kq, kv), F32),
            pltpu.VMEM((bb, hp, 1, kq), F32),
            pltpu.VMEM((bb, hp, 1, LANES), F32),
        ],
        input_output_aliases=aliases,
        compiler_params=_params(("parallel", "parallel", "arbitrary")),
        name="mlstm_scan",
    )(*args)
    return hg, (c_out, n_out, m_out)


def _mix_kernel(h_ref, w_ref, bias_ref, vw_ref, um_ref, *maybe_v_ref, lc):
    t = h_ref.shape[0]
    vraw = h_ref[:, S_WIDTH:].astype(F32)
    mu = jnp.mean(vraw, axis=-1, keepdims=True)
    vc = vraw - mu
    var = jnp.mean(vc * vc, axis=-1, keepdims=True)
    v = vc * lax.rsqrt(var + LN_EPS) * vw_ref[...]
    for v_ref in maybe_v_ref:
        v_ref[...] = v
    vb = v.astype(BF16)
    rr = lax.broadcasted_iota(jnp.int32, (t, t), 0)
    cc = lax.broadcasted_iota(jnp.int32, (t, t), 1)
    keep = _group_causal(rr, cc, lc)
    for g in range(S_GROUPS):
        sl = slice(g * S_DG, (g + 1) * S_DG)
        wg = (w_ref[g] * keep).astype(BF16)
        mixed = jnp.dot(wg, vb[:, sl], preferred_element_type=F32) + bias_ref[:, g:g + 1]
        um_ref[:, sl] = (h_ref[:, sl].astype(F32) * mixed).astype(BF16)


def _mix(hdn, wmix, bias_t, vnorm_w, layer, *, lc, emit_v):
    m = hdn.shape[0]
    t = S_CHUNK
    row_spec = pl.BlockSpec((t, S_WIDTH), lambda i: (i, 0))
    out_specs = [row_spec, row_spec] if emit_v else [row_spec]
    out_shape = [jax.ShapeDtypeStruct((m, S_WIDTH), BF16)]
    if emit_v:
        out_shape.append(jax.ShapeDtypeStruct((m, S_WIDTH), F32))
    outs = pl.pallas_call(
        functools.partial(_mix_kernel, lc=lc),
        grid=(m // t,),
        in_specs=[
            pl.BlockSpec((t, 2 * S_WIDTH), lambda i: (i, 0)),
            pl.BlockSpec((None, S_GROUPS, t, t), lambda i: (layer, 0, 0, 0)),
            pl.BlockSpec((None, t, S_GROUPS), lambda i: (layer, 0, 0)),
            pl.BlockSpec((None, 1, S_WIDTH), lambda i: (layer, 0, 0)),
        ],
        out_specs=out_specs,
        out_shape=out_shape,
        compiler_params=_params(("parallel",)),
        name="chunk_mix",
    )(hdn, wmix, bias_t, vnorm_w)
    return (outs[0], outs[1]) if emit_v else (outs[0], None)


def _route_kernel(xb_ref, wq_ref, keys_ref, r1_ref, e1_ref, nn_ref, c0_ref, q_s, sc_s, wk_s, rk_s, sv_s):
    tm = xb_ref.shape[0]
    q_s[...] = jnp.dot(xb_ref[...], wq_ref[...], preferred_element_type=F32)
    kidx = lax.broadcasted_iota(jnp.int32, (P_NKEYS, tm), 0).astype(F32)
    for h in range(P_HEADS):
        for c in range(2):
            hc = 2 * h + c
            qhc = q_s[:, hc * P_DH:(hc + 1) * P_DH].astype(BF16)
            sc = lax.dot_general(keys_ref[c], qhc, (((1,), (1,)), ((), ())), preferred_element_type=F32)
            sc_s[hc] = sc

        def top16(exact, h=h):
            for c in range(2):
                wk_s[c] = sc_s[2 * h + c]
                if exact:
                    rk_s[2 * h + c] = jnp.full((P_NKEYS, tm), float(P_NKEYS), F32)

            def body(kk, carry):
                kf = lax.convert_element_type(kk, F32)
                for c in range(2):
                    hc = 2 * h + c
                    s = wk_s[c]
                    mx = jnp.max(s, axis=0, keepdims=True)
                    sel = s == mx
                    if exact:
                        idx = jnp.min(jnp.where(sel, kidx, float(P_NKEYS)), axis=0, keepdims=True)
                        sel = kidx == idx
                        rk_s[hc] = jnp.where(sel, kf, rk_s[hc])
                        wk_s[c] = jnp.where(sel, NEG_INF, s)
                    else:
                        wk_s[c] = jnp.where(sel, (kf + 1.0) * -TOPK_MARK, s)
                    sv_s[c, kk, h:h + 1, :] = mx
                return carry

            lax.fori_loop(0, P_TOPK, body, 0)

        top16(exact=False)
        removed = jnp.zeros((1, tm), F32)
        for c in range(2):
            w = wk_s[c]
            gone = w <= -TOPK_MARK
            removed = removed + jnp.sum(jnp.where(gone, 1.0, 0.0), axis=0, keepdims=True)
            rk_s[2 * h + c] = jnp.where(gone, w * (-1.0 / TOPK_MARK) - 1.0, float(P_NKEYS))
        tied = jnp.max(jnp.abs(removed - 2.0 * P_TOPK)) > 0.0

        @pl.when(tied)
        def _(top16=top16):
            top16(exact=True)

    sv0 = [sv_s[0, k] for k in range(P_TOPK)]
    sv1 = [sv_s[1, k] for k in range(P_TOPK)]
    cnt = [jnp.zeros((P_HEADS, tm), F32) for _ in range(P_TOPK)]
    front = [sv0[k] + sv1[0] for k in range(P_TOPK)]
    top = front[0]
    z = jnp.zeros((P_HEADS, tm), F32)
    for r in range(P_TOPK):
        live = min(r + 1, P_TOPK)
        mx = front[0]
        for k in range(1, live):
            mx = jnp.maximum(mx, front[k])
        pick = jnp.full((P_HEADS, tm), float(P_TOPK), F32)
        for k in reversed(range(live)):
            pick = jnp.where(front[k] == mx, float(k), pick)
        z = z + jnp.exp(mx - top)
        hits = [pick == float(k) for k in range(live)]
        newcnt = jnp.zeros((P_HEADS, tm), F32)
        for k in range(live):
            cnt[k] = cnt[k] + jnp.where(hits[k], 1.0, 0.0)
            newcnt = jnp.where(hits[k], cnt[k], newcnt)
        nxt = jnp.full((P_HEADS, tm), NEG_INF, F32)
        for j in range(1, min(r + 2, P_TOPK)):
            nxt = jnp.where(newcnt == float(j), sv1[j], nxt)
        for k in range(live):
            front[k] = jnp.where(hits[k], sv0[k] + nxt, front[k])
    zinv = 1.0 / z

    for h in range(P_HEADS):
        rank0 = rk_s[2 * h]
        nn = jnp.zeros((P_NKEYS, tm), F32)
        for k in range(P_TOPK):
            nn = jnp.where(rank0 == float(k), cnt[k][h:h + 1, :], nn)
        nn_ref[h] = nn
        c0_ref[h] = jnp.exp(sc_s[2 * h] - sv0[0][h:h + 1, :]) * zinv[h:h + 1, :]
        r1_ref[h] = pltpu.bitcast(rk_s[2 * h + 1].astype(BF16), jnp.int32)
        e1_ref[h] = pltpu.bitcast(jnp.exp(sc_s[2 * h + 1] - sv1[0][h:h + 1, :]).astype(BF16), jnp.int32)


def _route(xb, wq, keys, li, *, tm=256):
    m, k = xb.shape
    tab = jax.ShapeDtypeStruct((P_HEADS, P_NKEYS, m), F32)
    tab16 = jax.ShapeDtypeStruct((P_HEADS, P_NKEYS // 2, m), jnp.int32)
    tab_spec = pl.BlockSpec((P_HEADS, P_NKEYS, tm), lambda i: (0, 0, i))
    tab16_spec = pl.BlockSpec((P_HEADS, P_NKEYS // 2, tm), lambda i: (0, 0, i))
    return pl.pallas_call(
        _route_kernel,
        grid=(m // tm,),
        in_specs=[
            pl.BlockSpec((tm, k), lambda i: (i, 0)),
            pl.BlockSpec((None, k, P_HEADS * 2 * P_DH), lambda i: (li, 0, 0)),
            pl.BlockSpec((None, 2, P_NKEYS, P_DH), lambda i: (li, 0, 0, 0)),
        ],
        out_specs=[tab16_spec, tab16_spec, tab_spec, tab_spec],
        out_shape=[tab16, tab16, tab, tab],
        scratch_shapes=[
            pltpu.VMEM((tm, P_HEADS * 2 * P_DH), F32),
            pltpu.VMEM((2 * P_HEADS, P_NKEYS, tm), F32),
            pltpu.VMEM((2, P_NKEYS, tm), F32),
            pltpu.VMEM((2 * P_HEADS, P_NKEYS, tm), F32),
            pltpu.VMEM((2, P_TOPK, P_HEADS, tm), F32),
        ],
        compiler_params=_params(("parallel",)),
        name="peer_route",
    )(xb, wq, keys)


def _gelu_tanh(x):
    c = 0.7978845608028654
    inner = x + x * x * x * 0.044715
    return x * 0.5 * (jnp.tanh(inner * c) + 1.0)


def _peer_kernel(xb_ref, u_ref, v_ref, r1_ref, e1_ref, nn_ref, c0_ref, xf_ref, g_ref, b_ref,
                 of_ref, ob_ref, acc_s, st_s, act_s, xt_s, *, te):
    e = pl.program_id(1)
    last = pl.num_programs(1) - 1
    n_i1 = te // P_NKEYS
    tb = st_s.shape[1]

    @pl.when(e == 0)
    def _():
        acc_s[...] = jnp.zeros_like(acc_s)
        xt_s[...] = xb_ref[...].T

    st_s[...] = jnp.dot(u_ref[...], xt_s[...], preferred_element_type=F32)
    base = e * n_i1
    shape3 = (P_NKEYS // BF16_ROWS, BF16_ROWS, LANES)
    zero = jnp.zeros(shape3, BF16)
    for j in range(n_i1):
        rows = slice(j * P_NKEYS, (j + 1) * P_NKEYS)
        for t in range(tb // LANES):
            cols = slice(t * LANES, (t + 1) * LANES)
            gate = zero
            for h in range(P_HEADS):
                nn = nn_ref[h, pl.ds(base + j, 1), :][:, cols]
                c0 = c0_ref[h, pl.ds(base + j, 1), :][:, cols]
                nn = jnp.broadcast_to(nn, (BF16_ROWS, LANES)).astype(BF16)[None]
                c0 = jnp.broadcast_to(c0, (BF16_ROWS, LANES)).astype(BF16)[None]
                r1 = pltpu.bitcast(r1_ref[h, :, cols], BF16).reshape(shape3)
                e1 = pltpu.bitcast(e1_ref[h, :, cols], BF16).reshape(shape3)
                gate = gate + jnp.where(r1 < nn, e1, zero) * c0
            act = _gelu_tanh(st_s[rows, cols].astype(BF16).reshape(shape3)) * gate
            act_s[rows, cols] = act.reshape(P_NKEYS, LANES)
    acc_s[...] += lax.dot_general(act_s[...], v_ref[...], (((0,), (0,)), ((), ())),
                                  preferred_element_type=F32)

    @pl.when(e == last)
    def _():
        o = _layer_norm_rows(ALPHA * xf_ref[...] + acc_s[...], g_ref[...], b_ref[...])
        of_ref[...] = o
        ob_ref[...] = o.astype(BF16)


def _peer(xb, xf, u, v, r1, e1, nn, c0, ln_g, ln_b, li, *, tb=PEER_TB, te=PEER_TE):
    m, d = xf.shape
    ne = P_NEXP // te
    once = pl.Buffered(1)
    tab_spec = pl.BlockSpec((P_HEADS, P_NKEYS, tb), lambda i, e: (0, 0, i), pipeline_mode=once)
    tab16_spec = pl.BlockSpec((P_HEADS, P_NKEYS // 2, tb), lambda i, e: (0, 0, i), pipeline_mode=once)
    return pl.pallas_call(
        functools.partial(_peer_kernel, te=te),
        grid=(m // tb, ne),
        in_specs=[
            pl.BlockSpec((tb, d), lambda i, e: (i, 0), pipeline_mode=once),
            pl.BlockSpec((None, te, d), lambda i, e: (li, e, 0)),
            pl.BlockSpec((None, te, d), lambda i, e: (li, e, 0)),
            tab16_spec, tab16_spec, tab_spec, tab_spec,
            pl.BlockSpec((tb, d), lambda i, e: (i, 0), pipeline_mode=once),
            _ln_spec(d, li, 1),
            _ln_spec(d, li, 1),
        ],
        out_specs=[pl.BlockSpec((tb, d), lambda i, e: (i, 0)), pl.BlockSpec((tb, d), lambda i, e: (i, 0))],
        out_shape=[jax.ShapeDtypeStruct((m, d), F32), jax.ShapeDtypeStruct((m, d), BF16)],
        scratch_shapes=[
            pltpu.VMEM((tb, d), F32),
            pltpu.VMEM((te, tb), F32),
            pltpu.VMEM((te, tb), BF16),
            pltpu.VMEM((d, tb), BF16),
        ],
        compiler_params=_params(("parallel", "arbitrary")),
        name="peer_dense",
    )(xb, u, v, r1, e1, nn, c0, xf, ln_g, ln_b)


def _trunk(x, p, c_in, n_in, m_in, *, batch, seq, wts, want_v):
    m = batch * seq
    xf = x
    xb = x.astype(BF16)
    decode = seq < MLSTM_PAD
    lc = min(seq, S_CHUNK)
    n_a = c_in.shape[0]
    states = (jnp.zeros((n_a, batch, M_HEADS, M_DK, M_DV), F32),
              jnp.zeros((n_a, batch, M_HEADS, 1, M_DK), F32),
              jnp.zeros((n_a, batch, M_HEADS, 1, LANES), F32))
    vs = []
    for i in range(DEPTH):
        j = i // N_MIXERS
        if i % N_MIXERS == 0:
            if decode:
                seg, valid, nc, bb = MLSTM_PAD, seq, 1, 8
                xin = jnp.pad(xb.reshape(batch, seq, D_MODEL), ((0, 0), (0, seg - seq), (0, 0)))
                xin = xin.reshape(batch * seg, D_MODEL)
            else:
                seg, valid, nc, bb = MLSTM_CHUNK, MLSTM_CHUNK, seq // MLSTM_CHUNK, 1
                xin = xb
            proj = _mm(xin, wts["a_in"], j, n=A_MAIN, out_dtype=BF16)
            gates = _mm(xin, wts["a_gate"], j)
            hg, states = _mlstm_scan(proj, gates, wts["a_gbias"], wts["a_norm"], c_in, n_in, m_in, j, states,
                                     batch=batch, nc=nc, seg=seg, valid=valid, bb=bb)
            if decode:
                hg = hg.reshape(batch, seg, HV)[:, :seq].reshape(m, HV)
            w_out = wts["a_out"]
        else:
            hdn = _mm(xb, wts["b_in"], j, wts["b_in_bias"], act="gelu", out_dtype=BF16)
            hg, v = _mix(hdn, wts["b_mix"][lc], wts["b_mix_bias"][lc], wts["b_norm"], j, lc=lc,
                         emit_v=want_v)
            vs.append(v)
            w_out = wts["b_out"]
        x1f, x1b = _mm_ln(hg, w_out, j, xf, wts["ln_g"], wts["ln_b"], i)
        r1, e1, nn, c0 = _route(x1b, wts["peer_wq"], wts["peer_keys"], i)
        x2f, x2b = _peer(x1b, x1f, wts["peer_u"], wts["peer_v"], r1, e1, nn, c0, wts["ln_g"], wts["ln_b"], i)
        xf, xb = _ple(x2f, x2b, wts["ple_gate"], p, wts["ple_w"], i)
    c_out, n_out, m_out = states
    return xf, c_out, n_out[:, :, :, 0, :], m_out[:, :, :, 0, 0], vs


def _prep_weights(w_a_in, b_a_gate, a_norm_w, w_a_out, w_b_in, b_b_in, b_norm_w, w_b_s, b_b_s, w_b_out,
                  ln_g, ln_b, peer_wq, peer_keys, peer_u, peer_v, ple_w, ple_gate_w, chunk_lens):
    n_a = w_a_in.shape[0]
    n_b = w_b_in.shape[0]
    gpad = LANES - 2 * M_HEADS
    wts = {
        "a_in": w_a_in[:, :, :A_MAIN].astype(BF16),
        "a_gate": jnp.pad(w_a_in[:, :, A_MAIN:], ((0, 0), (0, 0), (0, gpad))).astype(BF16),
        "a_gbias": jnp.pad(b_a_gate, ((0, 0), (0, gpad))).reshape(n_a, 1, LANES).astype(F32),
        "a_norm": a_norm_w.reshape(n_a, 1, HV).astype(F32),
        "a_out": w_a_out.astype(BF16),
        "b_in": w_b_in.astype(BF16),
        "b_in_bias": b_b_in.reshape(n_b, 1, -1).astype(F32),
        "b_norm": b_norm_w.reshape(n_b, 1, S_WIDTH).astype(F32),
        "b_out": w_b_out.astype(BF16),
        "b_mix": {},
        "b_mix_bias": {},
        "ln_g": ln_g.reshape(DEPTH, 2, 1, D_MODEL).astype(F32),
        "ln_b": ln_b.reshape(DEPTH, 2, 1, D_MODEL).astype(F32),
        "peer_wq": peer_wq.astype(BF16),
        "peer_keys": peer_keys.astype(BF16),
        "peer_u": peer_u.astype(BF16),
        "peer_v": peer_v.astype(BF16),
        "ple_w": ple_w.astype(BF16),
        "ple_gate": ple_gate_w.astype(BF16),
    }
    for lc in chunk_lens:
        rep = S_CHUNK // lc
        if rep == 1:
            wts["b_mix"][lc] = w_b_s.astype(F32)
        else:
            wts["b_mix"][lc] = jnp.tile(w_b_s[:, :, :lc, :lc], (1, 1, rep, rep)).astype(F32)
        wts["b_mix_bias"][lc] = jnp.swapaxes(jnp.tile(b_b_s[:, :, :lc], (1, 1, rep)), 1, 2).astype(F32)
    return wts


def kernel(x_prompt, x_sample, state_C, state_n, state_m, p_prompt, p_sample, w_a_in, b_a_gate, a_norm_w, w_a_out, w_b_in, b_b_in, b_norm_w, w_b_s, b_b_s, w_b_out, ln_g, ln_b, peer_wq, peer_keys, peer_u, peer_v, ple_w, ple_gate_w):
    bp, sp, d = x_prompt.shape
    bs, ss, _ = x_sample.shape
    n_a = state_C.shape[0]
    chunk_lens = sorted({min(sp, S_CHUNK), min(ss, S_CHUNK)})
    wts = _prep_weights(w_a_in, b_a_gate, a_norm_w, w_a_out, w_b_in, b_b_in, b_norm_w, w_b_s, b_b_s, w_b_out,
                        ln_g, ln_b, peer_wq, peer_keys, peer_u, peer_v, ple_w, ple_gate_w, chunk_lens)

    c0 = jnp.zeros((n_a, bp, M_HEADS, M_DK, M_DV), F32)
    n0 = jnp.zeros((n_a, bp, M_HEADS, M_DK), F32)
    m0 = jnp.zeros((n_a, bp, M_HEADS), F32)
    yp, pc, pn, pm, _ = _trunk(x_prompt.reshape(bp * sp, d), p_prompt.reshape(DEPTH, bp * sp, -1),
                               c0, n0, m0, batch=bp, seq=sp, wts=wts, want_v=False)
    ys, sc, sn, sm, sv = _trunk(x_sample.reshape(bs * ss, d), p_sample.reshape(DEPTH, bs * ss, -1),
                                state_C, state_n, state_m, batch=bs, seq=ss, wts=wts, want_v=True)
    return (yp.reshape(bp, sp, d), ys.reshape(bs, ss, d), pc, pn, pm, sc, sn, sm,
            jnp.stack([v.reshape(bs, ss, S_WIDTH) for v in sv]))
```

```python
import functools

import jax
import jax.numpy as jnp
from jax import lax
from jax.experimental import pallas as pl
from jax.experimental.pallas import tpu as pltpu

F32 = jnp.float32
BF16 = jnp.bfloat16

D_MODEL = 2048
DEPTH = 4
N_MIXERS = 2
M_HEADS = 8
M_DV = D_MODEL // M_HEADS
M_DK = M_DV // 2
HK = M_HEADS * M_DK
HV = M_HEADS * M_DV
A_MAIN = 2 * HK + 2 * HV
S_GROUPS = 8
S_CHUNK = 128
S_WIDTH = D_MODEL
S_DG = S_WIDTH // S_GROUPS
P_HEADS = 8
P_NKEYS = 128
P_NEXP = P_NKEYS * P_NKEYS
P_DH = 128
P_TOPK = 16
ALPHA = (2 * DEPTH) ** 0.25
LN_EPS = 1e-5

LANES = 128
BF16_ROWS = 16
VMEM_LIMIT = 56 * 1024 * 1024
NEG_INF = float("-inf")
TOPK_MARK = 2.0 ** 120

MLSTM_CHUNK = 256
MLSTM_PAD = 16
MLSTM_HEADS_PER_STEP = 4
PEER_TB = 512
PEER_TE = 1024
MM_ROWS = 1024
RES_ROWS = 512


def _params(sem):
    return pltpu.CompilerParams(dimension_semantics=sem, vmem_limit_bytes=VMEM_LIMIT)


def _layer_norm_rows(z, g, b):
    mu = jnp.mean(z, axis=-1, keepdims=True)
    zc = z - mu
    var = jnp.mean(zc * zc, axis=-1, keepdims=True)
    return zc * lax.rsqrt(var + LN_EPS) * g + b


def _mm_kernel(*refs, act, has_bias):
    if has_bias:
        x_ref, w_ref, b_ref, o_ref = refs
    else:
        x_ref, w_ref, o_ref = refs
    acc = jnp.dot(x_ref[...].astype(BF16), w_ref[...].astype(BF16), preferred_element_type=F32)
    if has_bias:
        acc = acc + b_ref[...]
    if act == "gelu":
        acc = jax.nn.gelu(acc)
    o_ref[...] = acc.astype(o_ref.dtype)


def _mm(x, w, layer, bias=None, *, n=None, act=None, out_dtype=F32, tm=None, tn=512):
    m, k = x.shape
    n = w.shape[2] if n is None else n
    tn = min(tn, n)
    if tm is None:
        tm = MM_ROWS if m % MM_ROWS == 0 else m
    in_specs = [
        pl.BlockSpec((tm, k), lambda i, j: (i, 0)),
        pl.BlockSpec((None, k, tn), lambda i, j: (layer, 0, j)),
    ]
    args = [x, w]
    if bias is not None:
        in_specs.append(pl.BlockSpec((None, 1, tn), lambda i, j: (layer, 0, j)))
        args.append(bias)
    return pl.pallas_call(
        functools.partial(_mm_kernel, act=act, has_bias=bias is not None),
        grid=(m // tm, n // tn),
        in_specs=in_specs,
        out_specs=pl.BlockSpec((tm, tn), lambda i, j: (i, j)),
        out_shape=jax.ShapeDtypeStruct((m, n), out_dtype),
        compiler_params=_params(("parallel", "parallel")),
        name="mm",
    )(*args)


def _mm_ln_kernel(a_ref, w_ref, r_ref, g_ref, b_ref, of_ref, ob_ref):
    y = jnp.dot(a_ref[...].astype(BF16), w_ref[...], preferred_element_type=F32)
    o = _layer_norm_rows(ALPHA * r_ref[...] + y, g_ref[...], b_ref[...])
    of_ref[...] = o
    ob_ref[...] = o.astype(BF16)


def _ln_spec(n, li, s):
    return pl.BlockSpec((None, None, 1, n), lambda *_: (li, s, 0, 0))


def _mm_ln(a, w, layer, res, ln_g, ln_b, li, *, tm=RES_ROWS):
    m, k = a.shape
    n = w.shape[2]
    return pl.pallas_call(
        _mm_ln_kernel,
        grid=(m // tm,),
        in_specs=[
            pl.BlockSpec((tm, k), lambda i: (i, 0)),
            pl.BlockSpec((None, k, n), lambda i: (layer, 0, 0)),
            pl.BlockSpec((tm, n), lambda i: (i, 0)),
            _ln_spec(n, li, 0),
            _ln_spec(n, li, 0),
        ],
        out_specs=[pl.BlockSpec((tm, n), lambda i: (i, 0)), pl.BlockSpec((tm, n), lambda i: (i, 0))],
        out_shape=[jax.ShapeDtypeStruct((m, n), F32), jax.ShapeDtypeStruct((m, n), BF16)],
        compiler_params=_params(("parallel",)),
        name="mm_ln",
    )(a, w, res, ln_g, ln_b)


def _ple_kernel(xf_ref, xb_ref, gw_ref, p_ref, pw_ref, of_ref, ob_ref):
    gate = jax.nn.sigmoid(jnp.dot(xb_ref[...], gw_ref[...], preferred_element_type=F32))
    pe = jnp.dot(p_ref[...].astype(BF16), pw_ref[...], preferred_element_type=F32)
    o = xf_ref[...] + gate * pe
    of_ref[...] = o
    ob_ref[...] = o.astype(BF16)


def _ple(xf, xb, gw, p, pw, li, *, tm=RES_ROWS):
    m, n = xf.shape
    kp = p.shape[2]
    return pl.pallas_call(
        _ple_kernel,
        grid=(m // tm,),
        in_specs=[
            pl.BlockSpec((tm, n), lambda i: (i, 0)),
            pl.BlockSpec((tm, n), lambda i: (i, 0)),
            pl.BlockSpec((None, n, n), lambda i: (li, 0, 0)),
            pl.BlockSpec((None, tm, kp), lambda i: (li, i, 0)),
            pl.BlockSpec((None, kp, n), lambda i: (li, 0, 0)),
        ],
        out_specs=[pl.BlockSpec((tm, n), lambda i: (i, 0)), pl.BlockSpec((tm, n), lambda i: (i, 0))],
        out_shape=[jax.ShapeDtypeStruct((m, n), F32), jax.ShapeDtypeStruct((m, n), BF16)],
        compiler_params=_params(("parallel",)),
        name="ple",
    )(xf, xb, gw, p, pw)


def _group_causal(rr, cc, group):
    shift = group.bit_length() - 1
    same = (rr >> shift) == (cc >> shift)
    return jnp.where(rr >= cc, jnp.where(same, 1.0, 0.0), 0.0)


def _split3(x):
    hi = x.astype(BF16)
    r1 = x - hi.astype(F32)
    mid = r1.astype(BF16)
    lo = (r1 - mid.astype(F32)).astype(BF16)
    return hi, mid, lo


def _gate_prep_kernel(g_ref, gb_ref, x_ref, xt_ref, *, seg, valid):
    rows = g_ref.shape[0]
    g = g_ref[...] + gb_ref[...]
    lane = lax.broadcasted_iota(jnp.int32, (rows, LANES), 1)
    r = lax.broadcasted_iota(jnp.int32, (rows, LANES), 0)
    pad = (r & (seg - 1)) >= valid
    lf = jnp.where(pad, 0.0, jax.nn.log_sigmoid(g))
    ig = jnp.where(pad, NEG_INF, g)
    rr = lax.broadcasted_iota(jnp.int32, (rows, rows), 0)
    cc = lax.broadcasted_iota(jnp.int32, (rows, rows), 1)
    tri = _group_causal(rr, cc, seg).astype(BF16)
    hi, mid, lo = _split3(lf)
    bcum = (jnp.dot(tri, hi, preferred_element_type=F32)
            + jnp.dot(tri, mid, preferred_element_type=F32)
            + jnp.dot(tri, lo, preferred_element_type=F32))
    x = jnp.where(lane >= M_HEADS, bcum, ig)
    x_ref[...] = x
    xt_ref[...] = x.T


def _gate_prep(gates, gbias, layer, *, seg, valid, rows):
    m = gates.shape[0]
    return pl.pallas_call(
        functools.partial(_gate_prep_kernel, seg=seg, valid=valid),
        grid=(m // rows,),
        in_specs=[pl.BlockSpec((rows, LANES), lambda i: (i, 0)),
                  pl.BlockSpec((None, 1, LANES), lambda i: (layer, 0, 0))],
        out_specs=[pl.BlockSpec((rows, LANES), lambda i: (i, 0)), pl.BlockSpec((LANES, rows), lambda i: (0, i))],
        out_shape=[jax.ShapeDtypeStruct((m, LANES), F32), jax.ShapeDtypeStruct((LANES, m), F32)],
        compiler_params=_params(("parallel",)),
        name="gate_prep",
    )(gates, gbias)


def _mlstm_kernel(q_ref, k_ref, v_ref, o_ref, x_ref, irow_ref, brow_ref, nw_ref, c0_ref, n0_ref, m0_ref,
                  *rest, bb, seg, hp, n_prev):
    hg_ref, c_ref, n_ref, m_ref, cs, ns, ms = rest[n_prev:]
    h0 = pl.program_id(1) * hp
    c = pl.program_id(2)
    rows = bb * seg
    lane = lax.broadcasted_iota(jnp.int32, (rows, LANES), 1)

    @pl.when(c == 0)
    def _():
        cs[...] = c0_ref[...]
        ns[...] = n0_ref[...]
        ms[...] = jnp.broadcast_to(m0_ref[...], (bb, hp, 1, LANES))

    x = x_ref[...]
    rr = lax.broadcasted_iota(jnp.int32, (rows, rows), 0)
    cc = lax.broadcasted_iota(jnp.int32, (rows, rows), 1)
    visible = _group_causal(rr, cc, seg) > 0.0
    seg_id = lax.broadcasted_iota(jnp.int32, (rows, 1), 0) >> (seg.bit_length() - 1)

    def per_row(vals):
        out = vals[0]
        for bi in range(1, bb):
            out = jnp.where(seg_id == bi, vals[bi], out)
        return out

    ends = [(bi + 1) * seg - 1 for bi in range(bb)]
    for g in range(hp):
        h = h0 + g
        icol = jnp.sum(jnp.where(lane == h, x, 0.0), axis=1, keepdims=True)
        bcol = jnp.sum(jnp.where(lane == h + M_HEADS, x, 0.0), axis=1, keepdims=True)
        irow = irow_ref[g]
        brow = brow_ref[g]
        m_prev_b = [ms[bi, g][:, :1] for bi in range(bb)]
        b_last_b = [bcol[r:r + 1, :] for r in ends]
        m_prev = per_row(m_prev_b)
        b_last = per_row(b_last_b)
        n_rows = per_row([ns[bi, g] for bi in range(bb)])

        q = q_ref[:, g * M_DK:(g + 1) * M_DK].astype(F32) * (M_DK ** -0.5)
        k = k_ref[:, g * M_DK:(g + 1) * M_DK].astype(F32)
        qb = q.astype(BF16)
        kb = k.astype(BF16)
        vcols = slice(g * M_DV, (g + 1) * M_DV)
        vb = v_ref[:, vcols].astype(BF16)

        logd = jnp.where(visible, bcol - brow + irow, NEG_INF)
        inter = bcol + m_prev
        m_row = jnp.maximum(inter, jnp.max(logd, axis=1, keepdims=True))
        dmat = jnp.exp(logd - m_row)
        s = lax.dot_general(qb, kb, (((1,), (1,)), ((), ())), preferred_element_type=F32) * dmat
        w_inter = jnp.exp(inter - m_row)
        qc = [jnp.dot(qb[bi * seg:(bi + 1) * seg], cs[bi, g].astype(BF16), preferred_element_type=F32)
              for bi in range(bb)]
        qc = qc[0] if bb == 1 else jnp.concatenate(qc, axis=0)
        num = w_inter * qc + jnp.dot(s.astype(BF16), vb, preferred_element_type=F32)
        den = w_inter * jnp.sum(q * n_rows, axis=1, keepdims=True) + jnp.sum(s, axis=1, keepdims=True)
        hh = num / jnp.maximum(jnp.abs(den), jnp.exp(-m_row))
        mu = jnp.mean(hh, axis=1, keepdims=True)
        hc = hh - mu
        var = jnp.mean(hc * hc, axis=1, keepdims=True)
        hn = hc * lax.rsqrt(var + LN_EPS) * nw_ref[:, vcols]
        hg_ref[:, vcols] = (hn * jax.nn.sigmoid(o_ref[:, vcols].astype(F32))).astype(hg_ref.dtype)

        m_new_b = [m_row[r:r + 1, :] for r in ends]
        g_tok = jnp.exp(b_last - bcol + icol - per_row(m_new_b))
        kg = g_tok * k
        kgb = kg.astype(BF16)
        for bi in range(bb):
            lo_r, hi_r = bi * seg, (bi + 1) * seg
            g_state = jnp.exp(b_last_b[bi] + m_prev_b[bi] - m_new_b[bi])
            c_new = g_state * cs[bi, g] + lax.dot_general(
                kgb[lo_r:hi_r], vb[lo_r:hi_r], (((0,), (0,)), ((), ())), preferred_element_type=F32)
            n_new = g_state * ns[bi, g] + jnp.sum(kg[lo_r:hi_r], axis=0, keepdims=True)
            m_new = jnp.broadcast_to(m_new_b[bi], (1, LANES))
            cs[bi, g] = c_new
            ns[bi, g] = n_new
            ms[bi, g] = m_new
            c_ref[bi, g] = c_new
            n_ref[bi, g] = n_new
            m_ref[bi, g] = m_new


def _mlstm_scan(proj, gates, gbias, norm_w, c0, n0, m0, layer, prev, *, batch, nc, seg, valid, bb,
                hp=MLSTM_HEADS_PER_STEP):
    rows = bb * seg
    nb = batch // bb
    n_a = c0.shape[0]
    kq, kv = M_DK, M_DV
    xg, xgt = _gate_prep(gates, gbias, layer, seg=seg, valid=valid, rows=rows)
    xgt = xgt[:2 * M_HEADS].reshape(2 * M_HEADS, 1, -1)
    nh = M_HEADS // hp
    row_map = lambda col: (lambda i, h, c: (i * nc + c, col(h)))
    st_map = lambda i, h, c: (layer, i, h, 0, 0)
    in_specs = [
        pl.BlockSpec((rows, hp * kq), row_map(lambda h: h)),
        pl.BlockSpec((rows, hp * kq), row_map(lambda h: HK // (hp * kq) + h)),
        pl.BlockSpec((rows, hp * kv), row_map(lambda h: 2 * HK // (hp * kv) + h)),
        pl.BlockSpec((rows, hp * kv), row_map(lambda h: (2 * HK + HV) // (hp * kv) + h)),
        pl.BlockSpec((rows, LANES), lambda i, h, c: (i * nc + c, 0)),
        pl.BlockSpec((hp, 1, rows), lambda i, h, c: (h, 0, i * nc + c)),
        pl.BlockSpec((hp, 1, rows), lambda i, h, c: (h + nh, 0, i * nc + c)),
        pl.BlockSpec((None, 1, hp * kv), lambda i, h, c: (layer, 0, h)),
        pl.BlockSpec((None, bb, hp, kq, kv), st_map),
        pl.BlockSpec((None, bb, hp, 1, kq), st_map),
        pl.BlockSpec((None, bb, hp, 1, 1), st_map),
    ]
    args = [proj, proj, proj, proj, xg, xgt, xgt, norm_w, c0,
            n0.reshape(n_a, batch, M_HEADS, 1, kq), m0.reshape(n_a, batch, M_HEADS, 1, 1)]
    aliases = {}
    n_prev = len(prev)
    for t, arr in enumerate(prev):
        aliases[len(args)] = 1 + t
        in_specs.append(pl.BlockSpec(memory_space=pl.ANY))
        args.append(arr)
    hg, c_out, n_out, m_out = pl.pallas_call(
        functools.partial(_mlstm_kernel, bb=bb, seg=seg, hp=hp, n_prev=n_prev),
        grid=(nb, nh, nc),
        in_specs=in_specs,
        out_specs=[
            pl.BlockSpec((rows, hp * kv), lambda i, h, c: (i * nc + c, h)),
            pl.BlockSpec((None, bb, hp, kq, kv), st_map),
            pl.BlockSpec((None, bb, hp, 1, kq), st_map),
            pl.BlockSpec((None, bb, hp, 1, LANES), st_map),
        ],
        out_shape=[
            jax.ShapeDtypeStruct((batch * nc * seg, HV), BF16),
            jax.ShapeDtypeStruct((n_a, batch, M_HEADS, kq, kv), F32),
            jax.ShapeDtypeStruct((n_a, batch, M_HEADS, 1, kq), F32),
            jax.ShapeDtypeStruct((n_a, batch, M_HEADS, 1, LANES), F32),
        ],
        scratch_shapes=[
            pltpu.VMEM((bb, hp, kq, kv), F32),
            pltpu.VMEM((bb, hp, 1, kq), F32),
            pltpu.VMEM((bb, hp, 1, LANES), F32),
        ],
        input_output_aliases=aliases,
        compiler_params=_params(("parallel", "parallel", "arbitrary")),
        name="mlstm_scan",
    )(*args)
    return hg, (c_out, n_out, m_out)


def _mix_kernel(h_ref, w_ref, bias_ref, vw_ref, um_ref, *maybe_v_ref, lc):
    t = h_ref.shape[0]
    vraw = h_ref[:, S_WIDTH:].astype(F32)
    mu = jnp.mean(vraw, axis=-1, keepdims=True)
    vc = vraw - mu
    var = jnp.mean(vc * vc, axis=-1, keepdims=True)
    v = vc * lax.rsqrt(var + LN_EPS) * vw_ref[...]
    for v_ref in maybe_v_ref:
        v_ref[...] = v
    vb = v.astype(BF16)
    rr = lax.broadcasted_iota(jnp.int32, (t, t), 0)
    cc = lax.broadcasted_iota(jnp.int32, (t, t), 1)
    keep = _group_causal(rr, cc, lc)
    for g in range(S_GROUPS):
        sl = slice(g * S_DG, (g + 1) * S_DG)
        wg = (w_ref[g] * keep).astype(BF16)
        mixed = jnp.dot(wg, vb[:, sl], preferred_element_type=F32) + bias_ref[:, g:g + 1]
        um_ref[:, sl] = (h_ref[:, sl].astype(F32) * mixed).astype(BF16)


def _mix(hdn, wmix, bias_t, vnorm_w, layer, *, lc, emit_v):
    m = hdn.shape[0]
    t = S_CHUNK
    row_spec = pl.BlockSpec((t, S_WIDTH), lambda i: (i, 0))
    out_specs = [row_spec, row_spec] if emit_v else [row_spec]
    out_shape = [jax.ShapeDtypeStruct((m, S_WIDTH), BF16)]
    if emit_v:
        out_shape.append(jax.ShapeDtypeStruct((m, S_WIDTH), F32))
    outs = pl.pallas_call(
        functools.partial(_mix_kernel, lc=lc),
        grid=(m // t,),
        in_specs=[
            pl.BlockSpec((t, 2 * S_WIDTH), lambda i: (i, 0)),
            pl.BlockSpec((None, S_GROUPS, t, t), lambda i: (layer, 0, 0, 0)),
            pl.BlockSpec((None, t, S_GROUPS), lambda i: (layer, 0, 0)),
            pl.BlockSpec((None, 1, S_WIDTH), lambda i: (layer, 0, 0)),
        ],
        out_specs=out_specs,
        out_shape=out_shape,
        compiler_params=_params(("parallel",)),
        name="chunk_mix",
    )(hdn, wmix, bias_t, vnorm_w)
    return (outs[0], outs[1]) if emit_v else (outs[0], None)


def _route_kernel(xb_ref, wq_ref, keys_ref, r1_ref, e1_ref, nn_ref, c0_ref, q_s, sc_s, wk_s, rk_s, sv_s):
    tm = xb_ref.shape[0]
    q_s[...] = jnp.dot(xb_ref[...], wq_ref[...], preferred_element_type=F32)
    kidx = lax.broadcasted_iota(jnp.int32, (P_NKEYS, tm), 0).astype(F32)
    for h in range(P_HEADS):
        for c in range(2):
            hc = 2 * h + c
            qhc = q_s[:, hc * P_DH:(hc + 1) * P_DH].astype(BF16)
            sc = lax.dot_general(keys_ref[c], qhc, (((1,), (1,)), ((), ())), preferred_element_type=F32)
            sc_s[hc] = sc

        def top16(exact, h=h):
            for c in range(2):
                wk_s[c] = sc_s[2 * h + c]
                if exact:
                    rk_s[2 * h + c] = jnp.full((P_NKEYS, tm), float(P_NKEYS), F32)

            def body(kk, carry):
                kf = lax.convert_element_type(kk, F32)
                for c in range(2):
                    hc = 2 * h + c
                    s = wk_s[c]
                    mx = jnp.max(s, axis=0, keepdims=True)
                    sel = s == mx
                    if exact:
                        idx = jnp.min(jnp.where(sel, kidx, float(P_NKEYS)), axis=0, keepdims=True)
                        sel = kidx == idx
                        rk_s[hc] = jnp.where(sel, kf, rk_s[hc])
                        wk_s[c] = jnp.where(sel, NEG_INF, s)
                    else:
                        wk_s[c] = jnp.where(sel, (kf + 1.0) * -TOPK_MARK, s)
                    sv_s[c, kk, h:h + 1, :] = mx
                return carry

            lax.fori_loop(0, P_TOPK, body, 0)

        top16(exact=False)
        removed = jnp.zeros((1, tm), F32)
        for c in range(2):
            w = wk_s[c]
            gone = w <= -TOPK_MARK
            removed = removed + jnp.sum(jnp.where(gone, 1.0, 0.0), axis=0, keepdims=True)
            rk_s[2 * h + c] = jnp.where(gone, w * (-1.0 / TOPK_MARK) - 1.0, float(P_NKEYS))
        tied = jnp.max(jnp.abs(removed - 2.0 * P_TOPK)) > 0.0

        @pl.when(tied)
        def _(top16=top16):
            top16(exact=True)

    sv0 = [sv_s[0, k] for k in range(P_TOPK)]
    sv1 = [sv_s[1, k] for k in range(P_TOPK)]
    cnt = [jnp.zeros((P_HEADS, tm), F32) for _ in range(P_TOPK)]
    front = [sv0[k] + sv1[0] for k in range(P_TOPK)]
    top = front[0]
    z = jnp.zeros((P_HEADS, tm), F32)
    for r in range(P_TOPK):
        live = min(r + 1, P_TOPK)
        mx = front[0]
        for k in range(1, live):
            mx = jnp.maximum(mx, front[k])
        pick = jnp.full((P_HEADS, tm), float(P_TOPK), F32)
        for k in reversed(range(live)):
            pick = jnp.where(front[k] == mx, float(k), pick)
        z = z + jnp.exp(mx - top)
        hits = [pick == float(k) for k in range(live)]
        newcnt = jnp.zeros((P_HEADS, tm), F32)
        for k in range(live):
            cnt[k] = cnt[k] + jnp.where(hits[k], 1.0, 0.0)
            newcnt = jnp.where(hits[k], cnt[k], newcnt)
        nxt = jnp.full((P_HEADS, tm), NEG_INF, F32)
        for j in range(1, min(r + 2, P_TOPK)):
            nxt = jnp.where(newcnt == float(j), sv1[j], nxt)
        for k in range(live):
            front[k] = jnp.where(hits[k], sv0[k] + nxt, front[k])
    zinv = 1.0 / z

    for h in range(P_HEADS):
        rank0 = rk_s[2 * h]
        nn = jnp.zeros((P_NKEYS, tm), F32)
        for k in range(P_TOPK):
            nn = jnp.where(rank0 == float(k), cnt[k][h:h + 1, :], nn)
        nn_ref[h] = nn
        c0_ref[h] = jnp.exp(sc_s[2 * h] - sv0[0][h:h + 1, :]) * zinv[h:h + 1, :]
        r1_ref[h] = pltpu.bitcast(rk_s[2 * h + 1].astype(BF16), jnp.int32)
        e1_ref[h] = pltpu.bitcast(jnp.exp(sc_s[2 * h + 1] - sv1[0][h:h + 1, :]).astype(BF16), jnp.int32)


def _route(xb, wq, keys, li, *, tm=256):
    m, k = xb.shape
    tab = jax.ShapeDtypeStruct((P_HEADS, P_NKEYS, m), F32)
    tab16 = jax.ShapeDtypeStruct((P_HEADS, P_NKEYS // 2, m), jnp.int32)
    tab_spec = pl.BlockSpec((P_HEADS, P_NKEYS, tm), lambda i: (0, 0, i))
    tab16_spec = pl.BlockSpec((P_HEADS, P_NKEYS // 2, tm), lambda i: (0, 0, i))
    return pl.pallas_call(
        _route_kernel,
        grid=(m // tm,),
        in_specs=[
            pl.BlockSpec((tm, k), lambda i: (i, 0)),
            pl.BlockSpec((None, k, P_HEADS * 2 * P_DH), lambda i: (li, 0, 0)),
            pl.BlockSpec((None, 2, P_NKEYS, P_DH), lambda i: (li, 0, 0, 0)),
        ],
        out_specs=[tab16_spec, tab16_spec, tab_spec, tab_spec],
        out_shape=[tab16, tab16, tab, tab],
        scratch_shapes=[
            pltpu.VMEM((tm, P_HEADS * 2 * P_DH), F32),
            pltpu.VMEM((2 * P_HEADS, P_NKEYS, tm), F32),
            pltpu.VMEM((2, P_NKEYS, tm), F32),
            pltpu.VMEM((2 * P_HEADS, P_NKEYS, tm), F32),
            pltpu.VMEM((2, P_TOPK, P_HEADS, tm), F32),
        ],
        compiler_params=_params(("parallel",)),
        name="peer_route",
    )(xb, wq, keys)


def _gelu_tanh(x):
    c = 0.7978845608028654
    inner = x + x * x * x * 0.044715
    return x * 0.5 * (jnp.tanh(inner * c) + 1.0)


def _peer_kernel(xb_ref, u_ref, v_ref, r1_ref, e1_ref, nn_ref, c0_ref, xf_ref, g_ref, b_ref,
                 of_ref, ob_ref, acc_s, st_s, act_s, xt_s, *, te):
    e = pl.program_id(1)
    last = pl.num_programs(1) - 1
    n_i1 = te // P_NKEYS
    tb = st_s.shape[1]

    @pl.when(e == 0)
    def _():
        acc_s[...] = jnp.zeros_like(acc_s)
        xt_s[...] = xb_ref[...].T

    st_s[...] = jnp.dot(u_ref[...], xt_s[...], preferred_element_type=F32)
    base = e * n_i1
    shape3 = (P_NKEYS // BF16_ROWS, BF16_ROWS, LANES)
    zero = jnp.zeros(shape3, BF16)
    for j in range(n_i1):
        rows = slice(j * P_NKEYS, (j + 1) * P_NKEYS)
        for t in range(tb // LANES):
            cols = slice(t * LANES, (t + 1) * LANES)
            gate = zero
            for h in range(P_HEADS):
                nn = nn_ref[h, pl.ds(base + j, 1), :][:, cols]
                c0 = c0_ref[h, pl.ds(base + j, 1), :][:, cols]
                nn = jnp.broadcast_to(nn, (BF16_ROWS, LANES)).astype(BF16)[None]
                c0 = jnp.broadcast_to(c0, (BF16_ROWS, LANES)).astype(BF16)[None]
                r1 = pltpu.bitcast(r1_ref[h, :, cols], BF16).reshape(shape3)
                e1 = pltpu.bitcast(e1_ref[h, :, cols], BF16).reshape(shape3)
                gate = gate + jnp.where(r1 < nn, e1, zero) * c0
            act = _gelu_tanh(st_s[rows, cols].astype(BF16).reshape(shape3)) * gate
            act_s[rows, cols] = act.reshape(P_NKEYS, LANES)
    acc_s[...] += lax.dot_general(act_s[...], v_ref[...], (((0,), (0,)), ((), ())),
                                  preferred_element_type=F32)

    @pl.when(e == last)
    def _():
        o = _layer_norm_rows(ALPHA * xf_ref[...] + acc_s[...], g_ref[...], b_ref[...])
        of_ref[...] = o
        ob_ref[...] = o.astype(BF16)


def _peer(xb, xf, u, v, r1, e1, nn, c0, ln_g, ln_b, li, *, tb=PEER_TB, te=PEER_TE):
    m, d = xf.shape
    ne = P_NEXP // te
    once = pl.Buffered(1)
    tab_spec = pl.BlockSpec((P_HEADS, P_NKEYS, tb), lambda i, e: (0, 0, i), pipeline_mode=once)
    tab16_spec = pl.BlockSpec((P_HEADS, P_NKEYS // 2, tb), lambda i, e: (0, 0, i), pipeline_mode=once)
    return pl.pallas_call(
        functools.partial(_peer_kernel, te=te),
        grid=(m // tb, ne),
        in_specs=[
            pl.BlockSpec((tb, d), lambda i, e: (i, 0)),
            pl.BlockSpec((None, te, d), lambda i, e: (li, e, 0)),
            pl.BlockSpec((None, te, d), lambda i, e: (li, e, 0)),
            tab16_spec, tab16_spec, tab_spec, tab_spec,
            pl.BlockSpec((tb, d), lambda i, e: (i, 0)),
            _ln_spec(d, li, 1),
            _ln_spec(d, li, 1),
        ],
        out_specs=[pl.BlockSpec((tb, d), lambda i, e: (i, 0), pipeline_mode=once),
                   pl.BlockSpec((tb, d), lambda i, e: (i, 0), pipeline_mode=once)],
        out_shape=[jax.ShapeDtypeStruct((m, d), F32), jax.ShapeDtypeStruct((m, d), BF16)],
        scratch_shapes=[
            pltpu.VMEM((tb, d), F32),
            pltpu.VMEM((te, tb), F32),
            pltpu.VMEM((te, tb), BF16),
            pltpu.VMEM((d, tb), BF16),
        ],
        compiler_params=_params(("parallel", "arbitrary")),
        name="peer_dense",
    )(xb, u, v, r1, e1, nn, c0, xf, ln_g, ln_b)


def _trunk(x, p, c_in, n_in, m_in, *, batch, seq, wts, want_v):
    m = batch * seq
    xf = x
    xb = x.astype(BF16)
    decode = seq < MLSTM_PAD
    lc = min(seq, S_CHUNK)
    n_a = c_in.shape[0]
    states = (jnp.zeros((n_a, batch, M_HEADS, M_DK, M_DV), F32),
              jnp.zeros((n_a, batch, M_HEADS, 1, M_DK), F32),
              jnp.zeros((n_a, batch, M_HEADS, 1, LANES), F32))
    vs = []
    for i in range(DEPTH):
        j = i // N_MIXERS
        if i % N_MIXERS == 0:
            if decode:
                seg, valid, nc, bb = MLSTM_PAD, seq, 1, 8
                xin = jnp.pad(xb.reshape(batch, seq, D_MODEL), ((0, 0), (0, seg - seq), (0, 0)))
                xin = xin.reshape(batch * seg, D_MODEL)
            else:
                seg, valid, nc, bb = MLSTM_CHUNK, MLSTM_CHUNK, seq // MLSTM_CHUNK, 1
                xin = xb
            proj = _mm(xin, wts["a_in"], j, n=A_MAIN, out_dtype=BF16)
            gates = _mm(xin, wts["a_gate"], j)
            hg, states = _mlstm_scan(proj, gates, wts["a_gbias"], wts["a_norm"], c_in, n_in, m_in, j, states,
                                     batch=batch, nc=nc, seg=seg, valid=valid, bb=bb)
            if decode:
                hg = hg.reshape(batch, seg, HV)[:, :seq].reshape(m, HV)
            w_out = wts["a_out"]
        else:
            hdn = _mm(xb, wts["b_in"], j, wts["b_in_bias"], act="gelu", out_dtype=BF16)
            hg, v = _mix(hdn, wts["b_mix"][lc], wts["b_mix_bias"][lc], wts["b_norm"], j, lc=lc,
                         emit_v=want_v)
            vs.append(v)
            w_out = wts["b_out"]
        x1f, x1b = _mm_ln(hg, w_out, j, xf, wts["ln_g"], wts["ln_b"], i)
        r1, e1, nn, c0 = _route(x1b, wts["peer_wq"], wts["peer_keys"], i)
        x2f, x2b = _peer(x1b, x1f, wts["peer_u"], wts["peer_v"], r1, e1, nn, c0, wts["ln_g"], wts["ln_b"], i)
        xf, xb = _ple(x2f, x2b, wts["ple_gate"], p, wts["ple_w"], i)
    c_out, n_out, m_out = states
    return xf, c_out, n_out[:, :, :, 0, :], m_out[:, :, :, 0, 0], vs


def _prep_weights(w_a_in, b_a_gate, a_norm_w, w_a_out, w_b_in, b_b_in, b_norm_w, w_b_s, b_b_s, w_b_out,
                  ln_g, ln_b, peer_wq, peer_keys, peer_u, peer_v, ple_w, ple_gate_w, chunk_lens):
    n_a = w_a_in.shape[0]
    n_b = w_b_in.shape[0]
    gpad = LANES - 2 * M_HEADS
    wts = {
        "a_in": w_a_in,
        "a_gate": jnp.pad(w_a_in[:, :, A_MAIN:], ((0, 0), (0, 0), (0, gpad))).astype(BF16),
        "a_gbias": jnp.pad(b_a_gate, ((0, 0), (0, gpad))).reshape(n_a, 1, LANES).astype(F32),
        "a_norm": a_norm_w.reshape(n_a, 1, HV).astype(F32),
        "a_out": w_a_out.astype(BF16),
        "b_in": w_b_in,
        "b_in_bias": b_b_in.reshape(n_b, 1, -1).astype(F32),
        "b_norm": b_norm_w.reshape(n_b, 1, S_WIDTH).astype(F32),
        "b_out": w_b_out.astype(BF16),
        "b_mix": {},
        "b_mix_bias": {},
        "ln_g": ln_g.reshape(DEPTH, 2, 1, D_MODEL).astype(F32),
        "ln_b": ln_b.reshape(DEPTH, 2, 1, D_MODEL).astype(F32),
        "peer_wq": peer_wq.astype(BF16),
        "peer_keys": peer_keys.astype(BF16),
        "peer_u": peer_u.astype(BF16),
        "peer_v": peer_v.astype(BF16),
        "ple_w": ple_w.astype(BF16),
        "ple_gate": ple_gate_w.astype(BF16),
    }
    for lc in chunk_lens:
        rep = S_CHUNK // lc
        if rep == 1:
            wts["b_mix"][lc] = w_b_s.astype(F32)
        else:
            wts["b_mix"][lc] = jnp.tile(w_b_s[:, :, :lc, :lc], (1, 1, rep, rep)).astype(F32)
        wts["b_mix_bias"][lc] = jnp.swapaxes(jnp.tile(b_b_s[:, :, :lc], (1, 1, rep)), 1, 2).astype(F32)
    return wts


def kernel(x_prompt, x_sample, state_C, state_n, state_m, p_prompt, p_sample, w_a_in, b_a_gate, a_norm_w, w_a_out, w_b_in, b_b_in, b_norm_w, w_b_s, b_b_s, w_b_out, ln_g, ln_b, peer_wq, peer_keys, peer_u, peer_v, ple_w, ple_gate_w):
    bp, sp, d = x_prompt.shape
    bs, ss, _ = x_sample.shape
    n_a = state_C.shape[0]
    chunk_lens = sorted({min(sp, S_CHUNK), min(ss, S_CHUNK)})
    wts = _prep_weights(w_a_in, b_a_gate, a_norm_w, w_a_out, w_b_in, b_b_in, b_norm_w, w_b_s, b_b_s, w_b_out,
                        ln_g, ln_b, peer_wq, peer_keys, peer_u, peer_v, ple_w, ple_gate_w, chunk_lens)

    c0 = jnp.zeros((n_a, bp, M_HEADS, M_DK, M_DV), F32)
    n0 = jnp.zeros((n_a, bp, M_HEADS, M_DK), F32)
    m0 = jnp.zeros((n_a, bp, M_HEADS), F32)
    yp, pc, pn, pm, _ = _trunk(x_prompt.reshape(bp * sp, d), p_prompt.reshape(DEPTH, bp * sp, -1),
                               c0, n0, m0, batch=bp, seq=sp, wts=wts, want_v=False)
    ys, sc, sn, sm, sv = _trunk(x_sample.reshape(bs * ss, d), p_sample.reshape(DEPTH, bs * ss, -1),
                                state_C, state_n, state_m, batch=bs, seq=ss, wts=wts, want_v=True)
    return (yp.reshape(bp, sp, d), ys.reshape(bs, ss, d), pc, pn, pm, sc, sn, sm,
            jnp.stack([v.reshape(bs, ss, S_WIDTH) for v in sv]))
```

```python
import functools

import jax
import jax.numpy as jnp
from jax import lax
from jax.experimental import pallas as pl
from jax.experimental.pallas import tpu as pltpu

F32 = jnp.float32
BF16 = jnp.bfloat16

D_MODEL = 2048
DEPTH = 4
N_MIXERS = 2
M_HEADS = 8
M_DV = D_MODEL // M_HEADS
M_DK = M_DV // 2
HK = M_HEADS * M_DK
HV = M_HEADS * M_DV
A_MAIN = 2 * HK + 2 * HV
S_GROUPS = 8
S_CHUNK = 128
S_WIDTH = D_MODEL
S_DG = S_WIDTH // S_GROUPS
P_HEADS = 8
P_NKEYS = 128
P_NEXP = P_NKEYS * P_NKEYS
P_DH = 128
P_TOPK = 16
ALPHA = (2 * DEPTH) ** 0.25
LN_EPS = 1e-5

LANES = 128
BF16_ROWS = 16
VMEM_LIMIT = 56 * 1024 * 1024
NEG_INF = float("-inf")
TOPK_MARK = 2.0 ** 120

MLSTM_CHUNK = 256
MLSTM_PAD = 16
MLSTM_HEADS_PER_STEP = 4
PEER_TB = 512
PEER_TE = 1024
MM_ROWS = 1024
RES_ROWS = 512


def _params(sem):
    return pltpu.CompilerParams(dimension_semantics=sem, vmem_limit_bytes=VMEM_LIMIT)


def _layer_norm_rows(z, g, b):
    mu = jnp.mean(z, axis=-1, keepdims=True)
    zc = z - mu
    var = jnp.mean(zc * zc, axis=-1, keepdims=True)
    return zc * lax.rsqrt(var + LN_EPS) * g + b


def _mm_kernel(*refs, act, has_bias):
    if has_bias:
        x_ref, w_ref, b_ref, o_ref = refs
    else:
        x_ref, w_ref, o_ref = refs
    acc = jnp.dot(x_ref[...].astype(BF16), w_ref[...].astype(BF16), preferred_element_type=F32)
    if has_bias:
        acc = acc + b_ref[...]
    if act == "gelu":
        acc = jax.nn.gelu(acc)
    o_ref[...] = acc.astype(o_ref.dtype)


def _mm(x, w, layer, bias=None, *, n=None, act=None, out_dtype=F32, tm=None, tn=512):
    m, k = x.shape
    n = w.shape[2] if n is None else n
    tn = min(tn, n)
    if tm is None:
        tm = MM_ROWS if m % MM_ROWS == 0 else m
    in_specs = [
        pl.BlockSpec((tm, k), lambda i, j: (i, 0)),
        pl.BlockSpec((None, k, tn), lambda i, j: (layer, 0, j)),
    ]
    args = [x, w]
    if bias is not None:
        in_specs.append(pl.BlockSpec((None, 1, tn), lambda i, j: (layer, 0, j)))
        args.append(bias)
    return pl.pallas_call(
        functools.partial(_mm_kernel, act=act, has_bias=bias is not None),
        grid=(m // tm, n // tn),
        in_specs=in_specs,
        out_specs=pl.BlockSpec((tm, tn), lambda i, j: (i, j)),
        out_shape=jax.ShapeDtypeStruct((m, n), out_dtype),
        compiler_params=_params(("parallel", "parallel")),
        name="mm",
    )(*args)


def _mm_ln_kernel(a_ref, w_ref, r_ref, g_ref, b_ref, of_ref, ob_ref):
    y = jnp.dot(a_ref[...].astype(BF16), w_ref[...], preferred_element_type=F32)
    o = _layer_norm_rows(ALPHA * r_ref[...] + y, g_ref[...], b_ref[...])
    of_ref[...] = o
    ob_ref[...] = o.astype(BF16)


def _ln_spec(n, li, s):
    return pl.BlockSpec((None, None, 1, n), lambda *_: (li, s, 0, 0))


def _mm_ln(a, w, layer, res, ln_g, ln_b, li, *, tm=RES_ROWS):
    m, k = a.shape
    n = w.shape[2]
    return pl.pallas_call(
        _mm_ln_kernel,
        grid=(m // tm,),
        in_specs=[
            pl.BlockSpec((tm, k), lambda i: (i, 0)),
            pl.BlockSpec((None, k, n), lambda i: (layer, 0, 0)),
            pl.BlockSpec((tm, n), lambda i: (i, 0)),
            _ln_spec(n, li, 0),
            _ln_spec(n, li, 0),
        ],
        out_specs=[pl.BlockSpec((tm, n), lambda i: (i, 0)), pl.BlockSpec((tm, n), lambda i: (i, 0))],
        out_shape=[jax.ShapeDtypeStruct((m, n), F32), jax.ShapeDtypeStruct((m, n), BF16)],
        compiler_params=_params(("parallel",)),
        name="mm_ln",
    )(a, w, res, ln_g, ln_b)


def _ple_kernel(xf_ref, xb_ref, gw_ref, p_ref, pw_ref, of_ref, ob_ref):
    gate = jax.nn.sigmoid(jnp.dot(xb_ref[...], gw_ref[...], preferred_element_type=F32))
    pe = jnp.dot(p_ref[...].astype(BF16), pw_ref[...], preferred_element_type=F32)
    o = xf_ref[...] + gate * pe
    of_ref[...] = o
    ob_ref[...] = o.astype(BF16)


def _ple(xf, xb, gw, p, pw, li, *, tm=RES_ROWS):
    m, n = xf.shape
    kp = p.shape[2]
    return pl.pallas_call(
        _ple_kernel,
        grid=(m // tm,),
        in_specs=[
            pl.BlockSpec((tm, n), lambda i: (i, 0)),
            pl.BlockSpec((tm, n), lambda i: (i, 0)),
            pl.BlockSpec((None, n, n), lambda i: (li, 0, 0)),
            pl.BlockSpec((None, tm, kp), lambda i: (li, i, 0)),
            pl.BlockSpec((None, kp, n), lambda i: (li, 0, 0)),
        ],
        out_specs=[pl.BlockSpec((tm, n), lambda i: (i, 0)), pl.BlockSpec((tm, n), lambda i: (i, 0))],
        out_shape=[jax.ShapeDtypeStruct((m, n), F32), jax.ShapeDtypeStruct((m, n), BF16)],
        compiler_params=_params(("parallel",)),
        name="ple",
    )(xf, xb, gw, p, pw)


def _group_causal(rr, cc, group):
    shift = group.bit_length() - 1
    same = (rr >> shift) == (cc >> shift)
    return jnp.where(rr >= cc, jnp.where(same, 1.0, 0.0), 0.0)


def _split3(x):
    hi = x.astype(BF16)
    r1 = x - hi.astype(F32)
    mid = r1.astype(BF16)
    lo = (r1 - mid.astype(F32)).astype(BF16)
    return hi, mid, lo


def _gate_prep_kernel(g_ref, gb_ref, x_ref, xt_ref, *, seg, valid):
    rows = g_ref.shape[0]
    g = g_ref[...] + gb_ref[...]
    lane = lax.broadcasted_iota(jnp.int32, (rows, LANES), 1)
    r = lax.broadcasted_iota(jnp.int32, (rows, LANES), 0)
    pad = (r & (seg - 1)) >= valid
    lf = jnp.where(pad, 0.0, jax.nn.log_sigmoid(g))
    ig = jnp.where(pad, NEG_INF, g)
    rr = lax.broadcasted_iota(jnp.int32, (rows, rows), 0)
    cc = lax.broadcasted_iota(jnp.int32, (rows, rows), 1)
    tri = _group_causal(rr, cc, seg).astype(BF16)
    hi, mid, lo = _split3(lf)
    bcum = (jnp.dot(tri, hi, preferred_element_type=F32)
            + jnp.dot(tri, mid, preferred_element_type=F32)
            + jnp.dot(tri, lo, preferred_element_type=F32))
    x = jnp.where(lane >= M_HEADS, bcum, ig)
    x_ref[...] = x
    xt_ref[...] = x.T


def _gate_prep(gates, gbias, layer, *, seg, valid, rows):
    m = gates.shape[0]
    return pl.pallas_call(
        functools.partial(_gate_prep_kernel, seg=seg, valid=valid),
        grid=(m // rows,),
        in_specs=[pl.BlockSpec((rows, LANES), lambda i: (i, 0)),
                  pl.BlockSpec((None, 1, LANES), lambda i: (layer, 0, 0))],
        out_specs=[pl.BlockSpec((rows, LANES), lambda i: (i, 0)), pl.BlockSpec((LANES, rows), lambda i: (0, i))],
        out_shape=[jax.ShapeDtypeStruct((m, LANES), F32), jax.ShapeDtypeStruct((LANES, m), F32)],
        compiler_params=_params(("parallel",)),
        name="gate_prep",
    )(gates, gbias)


def _mlstm_kernel(q_ref, k_ref, v_ref, o_ref, x_ref, irow_ref, brow_ref, nw_ref, c0_ref, n0_ref, m0_ref,
                  *rest, bb, seg, hp, n_prev):
    hg_ref, c_ref, n_ref, m_ref, cs, ns, ms = rest[n_prev:]
    h0 = pl.program_id(1) * hp
    c = pl.program_id(2)
    rows = bb * seg
    lane = lax.broadcasted_iota(jnp.int32, (rows, LANES), 1)

    @pl.when(c == 0)
    def _():
        cs[...] = c0_ref[...]
        ns[...] = n0_ref[...]
        ms[...] = jnp.broadcast_to(m0_ref[...], (bb, hp, 1, LANES))

    x = x_ref[...]
    rr = lax.broadcasted_iota(jnp.int32, (rows, rows), 0)
    cc = lax.broadcasted_iota(jnp.int32, (rows, rows), 1)
    visible = _group_causal(rr, cc, seg) > 0.0
    seg_id = lax.broadcasted_iota(jnp.int32, (rows, 1), 0) >> (seg.bit_length() - 1)

    def per_row(vals):
        out = vals[0]
        for bi in range(1, bb):
            out = jnp.where(seg_id == bi, vals[bi], out)
        return out

    ends = [(bi + 1) * seg - 1 for bi in range(bb)]
    for g in range(hp):
        h = h0 + g
        icol = jnp.sum(jnp.where(lane == h, x, 0.0), axis=1, keepdims=True)
        bcol = jnp.sum(jnp.where(lane == h + M_HEADS, x, 0.0), axis=1, keepdims=True)
        irow = irow_ref[g]
        brow = brow_ref[g]
        m_prev_b = [ms[bi, g][:, :1] for bi in range(bb)]
        b_last_b = [bcol[r:r + 1, :] for r in ends]
        m_prev = per_row(m_prev_b)
        b_last = per_row(b_last_b)
        n_rows = per_row([ns[bi, g] for bi in range(bb)])

        q = q_ref[:, g * M_DK:(g + 1) * M_DK].astype(F32) * (M_DK ** -0.5)
        k = k_ref[:, g * M_DK:(g + 1) * M_DK].astype(F32)
        qb = q.astype(BF16)
        kb = k.astype(BF16)
        vcols = slice(g * M_DV, (g + 1) * M_DV)
        vb = v_ref[:, vcols].astype(BF16)

        logd = jnp.where(visible, bcol - brow + irow, NEG_INF)
        inter = bcol + m_prev
        m_row = jnp.maximum(inter, jnp.max(logd, axis=1, keepdims=True))
        dmat = jnp.exp(logd - m_row)
        s = lax.dot_general(qb, kb, (((1,), (1,)), ((), ())), preferred_element_type=F32) * dmat
        w_inter = jnp.exp(inter - m_row)
        qc = [jnp.dot(qb[bi * seg:(bi + 1) * seg], cs[bi, g].astype(BF16), preferred_element_type=F32)
              for bi in range(bb)]
        qc = qc[0] if bb == 1 else jnp.concatenate(qc, axis=0)
        num = w_inter * qc + jnp.dot(s.astype(BF16), vb, preferred_element_type=F32)
        den = w_inter * jnp.sum(q * n_rows, axis=1, keepdims=True) + jnp.sum(s, axis=1, keepdims=True)
        hh = num / jnp.maximum(jnp.abs(den), jnp.exp(-m_row))
        mu = jnp.mean(hh, axis=1, keepdims=True)
        hc = hh - mu
        var = jnp.mean(hc * hc, axis=1, keepdims=True)
        hn = hc * lax.rsqrt(var + LN_EPS) * nw_ref[:, vcols]
        hg_ref[:, vcols] = (hn * jax.nn.sigmoid(o_ref[:, vcols].astype(F32))).astype(hg_ref.dtype)

        m_new_b = [m_row[r:r + 1, :] for r in ends]
        g_tok = jnp.exp(b_last - bcol + icol - per_row(m_new_b))
        kg = g_tok * k
        kgb = kg.astype(BF16)
        for bi in range(bb):
            lo_r, hi_r = bi * seg, (bi + 1) * seg
            g_state = jnp.exp(b_last_b[bi] + m_prev_b[bi] - m_new_b[bi])
            c_new = g_state * cs[bi, g] + lax.dot_general(
                kgb[lo_r:hi_r], vb[lo_r:hi_r], (((0,), (0,)), ((), ())), preferred_element_type=F32)
            n_new = g_state * ns[bi, g] + jnp.sum(kg[lo_r:hi_r], axis=0, keepdims=True)
            m_new = jnp.broadcast_to(m_new_b[bi], (1, LANES))
            cs[bi, g] = c_new
            ns[bi, g] = n_new
            ms[bi, g] = m_new
            c_ref[bi, g] = c_new
            n_ref[bi, g] = n_new
            m_ref[bi, g] = m_new


def _mlstm_scan(proj, gates, gbias, norm_w, c0, n0, m0, layer, prev, *, batch, nc, seg, valid, bb,
                hp=MLSTM_HEADS_PER_STEP):
    rows = bb * seg
    nb = batch // bb
    n_a = c0.shape[0]
    kq, kv = M_DK, M_DV
    xg, xgt = _gate_prep(gates, gbias, layer, seg=seg, valid=valid, rows=rows)
    xgt = xgt[:2 * M_HEADS].reshape(2 * M_HEADS, 1, -1)
    nh = M_HEADS // hp
    row_map = lambda col: (lambda i, h, c: (i * nc + c, col(h)))
    st_map = lambda i, h, c: (layer, i, h, 0, 0)
    in_specs = [
        pl.BlockSpec((rows, hp * kq), row_map(lambda h: h)),
        pl.BlockSpec((rows, hp * kq), row_map(lambda h: HK // (hp * kq) + h)),
        pl.BlockSpec((rows, hp * kv), row_map(lambda h: 2 * HK // (hp * kv) + h)),
        pl.BlockSpec((rows, hp * kv), row_map(lambda h: (2 * HK + HV) // (hp * kv) + h)),
        pl.BlockSpec((rows, LANES), lambda i, h, c: (i * nc + c, 0)),
        pl.BlockSpec((hp, 1, rows), lambda i, h, c: (h, 0, i * nc + c)),
        pl.BlockSpec((hp, 1, rows), lambda i, h, c: (h + nh, 0, i * nc + c)),
        pl.BlockSpec((None, 1, hp * kv), lambda i, h, c: (layer, 0, h)),
        pl.BlockSpec((None, bb, hp, kq, kv), st_map),
        pl.BlockSpec((None, bb, hp, 1, kq), st_map),
        pl.BlockSpec((None, bb, hp, 1, 1), st_map),
    ]
    args = [proj, proj, proj, proj, xg, xgt, xgt, norm_w, c0,
            n0.reshape(n_a, batch, M_HEADS, 1, kq), m0.reshape(n_a, batch, M_HEADS, 1, 1)]
    aliases = {}
    n_prev = len(prev)
    for t, arr in enumerate(prev):
        aliases[len(args)] = 1 + t
        in_specs.append(pl.BlockSpec(memory_space=pl.ANY))
        args.append(arr)
    hg, c_out, n_out, m_out = pl.pallas_call(
        functools.partial(_mlstm_kernel, bb=bb, seg=seg, hp=hp, n_prev=n_prev),
        grid=(nb, nh, nc),
        in_specs=in_specs,
        out_specs=[
            pl.BlockSpec((rows, hp * kv), lambda i, h, c: (i * nc + c, h)),
            pl.BlockSpec((None, bb, hp, kq, kv), st_map),
            pl.BlockSpec((None, bb, hp, 1, kq), st_map),
            pl.BlockSpec((None, bb, hp, 1, LANES), st_map),
        ],
        out_shape=[
            jax.ShapeDtypeStruct((batch * nc * seg, HV), BF16),
            jax.ShapeDtypeStruct((n_a, batch, M_HEADS, kq, kv), F32),
            jax.ShapeDtypeStruct((n_a, batch, M_HEADS, 1, kq), F32),
            jax.ShapeDtypeStruct((n_a, batch, M_HEADS, 1, LANES), F32),
        ],
        scratch_shapes=[
            pltpu.VMEM((bb, hp, kq, kv), F32),
            pltpu.VMEM((bb, hp, 1, kq), F32),
            pltpu.VMEM((bb, hp, 1, LANES), F32),
        ],
        input_output_aliases=aliases,
        compiler_params=_params(("parallel", "parallel", "arbitrary")),
        name="mlstm_scan",
    )(*args)
    return hg, (c_out, n_out, m_out)


def _mix_kernel(h_ref, w_ref, bias_ref, vw_ref, um_ref, *maybe_v_ref, lc):
    t = h_ref.shape[0]
    vraw = h_ref[:, S_WIDTH:].astype(F32)
    mu = jnp.mean(vraw, axis=-1, keepdims=True)
    vc = vraw - mu
    var = jnp.mean(vc * vc, axis=-1, keepdims=True)
    v = vc * lax.rsqrt(var + LN_EPS) * vw_ref[...]
    for v_ref in maybe_v_ref:
        v_ref[...] = v
    vb = v.astype(BF16)
    rr = lax.broadcasted_iota(jnp.int32, (t, t), 0)
    cc = lax.broadcasted_iota(jnp.int32, (t, t), 1)
    keep = _group_causal(rr, cc, lc)
    for g in range(S_GROUPS):
        sl = slice(g * S_DG, (g + 1) * S_DG)
        wg = (w_ref[g] * keep).astype(BF16)
        mixed = jnp.dot(wg, vb[:, sl], preferred_element_type=F32) + bias_ref[:, g:g + 1]
        um_ref[:, sl] = (h_ref[:, sl].astype(F32) * mixed).astype(BF16)


def _mix(hdn, wmix, bias_t, vnorm_w, layer, *, lc, emit_v):
    m = hdn.shape[0]
    t = S_CHUNK
    row_spec = pl.BlockSpec((t, S_WIDTH), lambda i: (i, 0))
    out_specs = [row_spec, row_spec] if emit_v else [row_spec]
    out_shape = [jax.ShapeDtypeStruct((m, S_WIDTH), BF16)]
    if emit_v:
        out_shape.append(jax.ShapeDtypeStruct((m, S_WIDTH), F32))
    outs = pl.pallas_call(
        functools.partial(_mix_kernel, lc=lc),
        grid=(m // t,),
        in_specs=[
            pl.BlockSpec((t, 2 * S_WIDTH), lambda i: (i, 0)),
            pl.BlockSpec((None, S_GROUPS, t, t), lambda i: (layer, 0, 0, 0)),
            pl.BlockSpec((None, t, S_GROUPS), lambda i: (layer, 0, 0)),
            pl.BlockSpec((None, 1, S_WIDTH), lambda i: (layer, 0, 0)),
        ],
        out_specs=out_specs,
        out_shape=out_shape,
        compiler_params=_params(("parallel",)),
        name="chunk_mix",
    )(hdn, wmix, bias_t, vnorm_w)
    return (outs[0], outs[1]) if emit_v else (outs[0], None)


def _route_kernel(xb_ref, wq_ref, keys_ref, r1_ref, e1_ref, nn_ref, c0_ref, q_s, sc_s, wk_s, rk_s, sv_s):
    tm = xb_ref.shape[0]
    q_s[...] = jnp.dot(xb_ref[...], wq_ref[...], preferred_element_type=F32)
    kidx = lax.broadcasted_iota(jnp.int32, (P_NKEYS, tm), 0).astype(F32)
    for h in range(P_HEADS):
        for c in range(2):
            hc = 2 * h + c
            qhc = q_s[:, hc * P_DH:(hc + 1) * P_DH].astype(BF16)
            sc = lax.dot_general(keys_ref[c], qhc, (((1,), (1,)), ((), ())), preferred_element_type=F32)
            sc_s[hc] = sc

        def top16(exact, h=h):
            for c in range(2):
                wk_s[c] = sc_s[2 * h + c]
                if exact:
                    rk_s[2 * h + c] = jnp.full((P_NKEYS, tm), float(P_NKEYS), F32)

            def body(kk, carry):
                kf = lax.convert_element_type(kk, F32)
                for c in range(2):
                    hc = 2 * h + c
                    s = wk_s[c]
                    mx = jnp.max(s, axis=0, keepdims=True)
                    sel = s == mx
                    if exact:
                        idx = jnp.min(jnp.where(sel, kidx, float(P_NKEYS)), axis=0, keepdims=True)
                        sel = kidx == idx
                        rk_s[hc] = jnp.where(sel, kf, rk_s[hc])
                        wk_s[c] = jnp.where(sel, NEG_INF, s)
                    else:
                        wk_s[c] = jnp.where(sel, (kf + 1.0) * -TOPK_MARK, s)
                    sv_s[c, kk, h:h + 1, :] = mx
                return carry

            lax.fori_loop(0, P_TOPK, body, 0)

        top16(exact=False)
        removed = jnp.zeros((1, tm), F32)
        for c in range(2):
            w = wk_s[c]
            gone = w <= -TOPK_MARK
            removed = removed + jnp.sum(jnp.where(gone, 1.0, 0.0), axis=0, keepdims=True)
            rk_s[2 * h + c] = jnp.where(gone, w * (-1.0 / TOPK_MARK) - 1.0, float(P_NKEYS))
        tied = jnp.max(jnp.abs(removed - 2.0 * P_TOPK)) > 0.0

        @pl.when(tied)
        def _(top16=top16):
            top16(exact=True)

    sv0 = [sv_s[0, k] for k in range(P_TOPK)]
    sv1 = [sv_s[1, k] for k in range(P_TOPK)]
    cnt = [jnp.zeros((P_HEADS, tm), F32) for _ in range(P_TOPK)]
    front = [sv0[k] + sv1[0] for k in range(P_TOPK)]
    top = front[0]
    z = jnp.zeros((P_HEADS, tm), F32)
    for r in range(P_TOPK):
        live = min(r + 1, P_TOPK)
        mx = front[0]
        for k in range(1, live):
            mx = jnp.maximum(mx, front[k])
        pick = jnp.full((P_HEADS, tm), float(P_TOPK), F32)
        for k in reversed(range(live)):
            pick = jnp.where(front[k] == mx, float(k), pick)
        z = z + jnp.exp(mx - top)
        hits = [pick == float(k) for k in range(live)]
        newcnt = jnp.zeros((P_HEADS, tm), F32)
        for k in range(live):
            cnt[k] = cnt[k] + jnp.where(hits[k], 1.0, 0.0)
            newcnt = jnp.where(hits[k], cnt[k], newcnt)
        nxt = jnp.full((P_HEADS, tm), NEG_INF, F32)
        for j in range(1, min(r + 2, P_TOPK)):
            nxt = jnp.where(newcnt == float(j), sv1[j], nxt)
        for k in range(live):
            front[k] = jnp.where(hits[k], sv0[k] + nxt, front[k])
    zinv = 1.0 / z

    for h in range(P_HEADS):
        rank0 = rk_s[2 * h]
        nn = jnp.zeros((P_NKEYS, tm), F32)
        for k in range(P_TOPK):
            nn = jnp.where(rank0 == float(k), cnt[k][h:h + 1, :], nn)
        nn_ref[h] = nn
        c0_ref[h] = jnp.exp(sc_s[2 * h] - sv0[0][h:h + 1, :]) * zinv[h:h + 1, :]
        r1_ref[h] = pltpu.bitcast(rk_s[2 * h + 1].astype(BF16), jnp.int32)
        e1_ref[h] = pltpu.bitcast(jnp.exp(sc_s[2 * h + 1] - sv1[0][h:h + 1, :]).astype(BF16), jnp.int32)


def _route(xb, wq, keys, li, *, tm=512):
    m, k = xb.shape
    tab = jax.ShapeDtypeStruct((P_HEADS, P_NKEYS, m), F32)
    tab16 = jax.ShapeDtypeStruct((P_HEADS, P_NKEYS // 2, m), jnp.int32)
    tab_spec = pl.BlockSpec((P_HEADS, P_NKEYS, tm), lambda i: (0, 0, i))
    tab16_spec = pl.BlockSpec((P_HEADS, P_NKEYS // 2, tm), lambda i: (0, 0, i))
    return pl.pallas_call(
        _route_kernel,
        grid=(m // tm,),
        in_specs=[
            pl.BlockSpec((tm, k), lambda i: (i, 0)),
            pl.BlockSpec((None, k, P_HEADS * 2 * P_DH), lambda i: (li, 0, 0)),
            pl.BlockSpec((None, 2, P_NKEYS, P_DH), lambda i: (li, 0, 0, 0)),
        ],
        out_specs=[tab16_spec, tab16_spec, tab_spec, tab_spec],
        out_shape=[tab16, tab16, tab, tab],
        scratch_shapes=[
            pltpu.VMEM((tm, P_HEADS * 2 * P_DH), F32),
            pltpu.VMEM((2 * P_HEADS, P_NKEYS, tm), F32),
            pltpu.VMEM((2, P_NKEYS, tm), F32),
            pltpu.VMEM((2 * P_HEADS, P_NKEYS, tm), F32),
            pltpu.VMEM((2, P_TOPK, P_HEADS, tm), F32),
        ],
        compiler_params=_params(("parallel",)),
        name="peer_route",
    )(xb, wq, keys)


def _gelu_tanh(x):
    c = 0.7978845608028654
    inner = x + x * x * x * 0.044715
    return x * 0.5 * (jnp.tanh(inner * c) + 1.0)


def _peer_kernel(xb_ref, u_ref, v_ref, r1_ref, e1_ref, nn_ref, c0_ref, xf_ref, g_ref, b_ref,
                 of_ref, ob_ref, acc_s, st_s, act_s, xt_s, *, te):
    e = pl.program_id(1)
    last = pl.num_programs(1) - 1
    n_i1 = te // P_NKEYS
    tb = st_s.shape[1]

    @pl.when(e == 0)
    def _():
        acc_s[...] = jnp.zeros_like(acc_s)
        xt_s[...] = xb_ref[...].T

    st_s[...] = jnp.dot(u_ref[...], xt_s[...], preferred_element_type=F32)
    base = e * n_i1
    shape3 = (P_NKEYS // BF16_ROWS, BF16_ROWS, LANES)
    zero = jnp.zeros(shape3, BF16)
    for j in range(n_i1):
        rows = slice(j * P_NKEYS, (j + 1) * P_NKEYS)
        for t in range(tb // LANES):
            cols = slice(t * LANES, (t + 1) * LANES)
            gate = zero
            for h in range(P_HEADS):
                nn = nn_ref[h, pl.ds(base + j, 1), :][:, cols]
                c0 = c0_ref[h, pl.ds(base + j, 1), :][:, cols]
                nn = jnp.broadcast_to(nn, (BF16_ROWS, LANES)).astype(BF16)[None]
                c0 = jnp.broadcast_to(c0, (BF16_ROWS, LANES)).astype(BF16)[None]
                r1 = pltpu.bitcast(r1_ref[h, :, cols], BF16).reshape(shape3)
                e1 = pltpu.bitcast(e1_ref[h, :, cols], BF16).reshape(shape3)
                gate = gate + jnp.where(r1 < nn, e1, zero) * c0
            act = _gelu_tanh(st_s[rows, cols].astype(BF16).reshape(shape3)) * gate
            act_s[rows, cols] = act.reshape(P_NKEYS, LANES)
    acc_s[...] += lax.dot_general(act_s[...], v_ref[...], (((0,), (0,)), ((), ())),
                                  preferred_element_type=F32)

    @pl.when(e == last)
    def _():
        o = _layer_norm_rows(ALPHA * xf_ref[...] + acc_s[...], g_ref[...], b_ref[...])
        of_ref[...] = o
        ob_ref[...] = o.astype(BF16)


def _peer(xb, xf, u, v, r1, e1, nn, c0, ln_g, ln_b, li, *, tb=PEER_TB, te=PEER_TE):
    m, d = xf.shape
    ne = P_NEXP // te
    once = pl.Buffered(1)
    tab_spec = pl.BlockSpec((P_HEADS, P_NKEYS, tb), lambda i, e: (0, 0, i), pipeline_mode=once)
    tab16_spec = pl.BlockSpec((P_HEADS, P_NKEYS // 2, tb), lambda i, e: (0, 0, i), pipeline_mode=once)
    return pl.pallas_call(
        functools.partial(_peer_kernel, te=te),
        grid=(m // tb, ne),
        in_specs=[
            pl.BlockSpec((tb, d), lambda i, e: (i, 0)),
            pl.BlockSpec((None, te, d), lambda i, e: (li, e, 0)),
            pl.BlockSpec((None, te, d), lambda i, e: (li, e, 0)),
            tab16_spec, tab16_spec, tab_spec, tab_spec,
            pl.BlockSpec((tb, d), lambda i, e: (i, 0)),
            _ln_spec(d, li, 1),
            _ln_spec(d, li, 1),
        ],
        out_specs=[pl.BlockSpec((tb, d), lambda i, e: (i, 0), pipeline_mode=once),
                   pl.BlockSpec((tb, d), lambda i, e: (i, 0), pipeline_mode=once)],
        out_shape=[jax.ShapeDtypeStruct((m, d), F32), jax.ShapeDtypeStruct((m, d), BF16)],
        scratch_shapes=[
            pltpu.VMEM((tb, d), F32),
            pltpu.VMEM((te, tb), F32),
            pltpu.VMEM((te, tb), BF16),
            pltpu.VMEM((d, tb), BF16),
        ],
        compiler_params=_params(("parallel", "arbitrary")),
        name="peer_dense",
    )(xb, u, v, r1, e1, nn, c0, xf, ln_g, ln_b)


def _trunk(x, p, c_in, n_in, m_in, *, batch, seq, wts, want_v):
    m = batch * seq
    xf = x
    xb = x.astype(BF16)
    decode = seq < MLSTM_PAD
    lc = min(seq, S_CHUNK)
    n_a = c_in.shape[0]
    states = (jnp.zeros((n_a, batch, M_HEADS, M_DK, M_DV), F32),
              jnp.zeros((n_a, batch, M_HEADS, 1, M_DK), F32),
              jnp.zeros((n_a, batch, M_HEADS, 1, LANES), F32))
    vs = []
    for i in range(DEPTH):
        j = i // N_MIXERS
        if i % N_MIXERS == 0:
            if decode:
                seg, valid, nc, bb = MLSTM_PAD, seq, 1, 8
                xin = jnp.pad(xb.reshape(batch, seq, D_MODEL), ((0, 0), (0, seg - seq), (0, 0)))
                xin = xin.reshape(batch * seg, D_MODEL)
            else:
                seg, valid, nc, bb = MLSTM_CHUNK, MLSTM_CHUNK, seq // MLSTM_CHUNK, 1
                xin = xb
            proj = _mm(xin, wts["a_in"], j, n=A_MAIN, out_dtype=BF16)
            gates = _mm(xin, wts["a_gate"], j)
            hg, states = _mlstm_scan(proj, gates, wts["a_gbias"], wts["a_norm"], c_in, n_in, m_in, j, states,
                                     batch=batch, nc=nc, seg=seg, valid=valid, bb=bb)
            if decode:
                hg = hg.reshape(batch, seg, HV)[:, :seq].reshape(m, HV)
            w_out = wts["a_out"]
        else:
            hdn = _mm(xb, wts["b_in"], j, wts["b_in_bias"], act="gelu", out_dtype=BF16)
            hg, v = _mix(hdn, wts["b_mix"][lc], wts["b_mix_bias"][lc], wts["b_norm"], j, lc=lc,
                         emit_v=want_v)
            vs.append(v)
            w_out = wts["b_out"]
        x1f, x1b = _mm_ln(hg, w_out, j, xf, wts["ln_g"], wts["ln_b"], i)
        r1, e1, nn, c0 = _route(x1b, wts["peer_wq"], wts["peer_keys"], i)
        x2f, x2b = _peer(x1b, x1f, wts["peer_u"], wts["peer_v"], r1, e1, nn, c0, wts["ln_g"], wts["ln_b"], i)
        xf, xb = _ple(x2f, x2b, wts["ple_gate"], p, wts["ple_w"], i)
    c_out, n_out, m_out = states
    return xf, c_out, n_out[:, :, :, 0, :], m_out[:, :, :, 0, 0], vs


def _prep_weights(w_a_in, b_a_gate, a_norm_w, w_a_out, w_b_in, b_b_in, b_norm_w, w_b_s, b_b_s, w_b_out,
                  ln_g, ln_b, peer_wq, peer_keys, peer_u, peer_v, ple_w, ple_gate_w, chunk_lens):
    n_a = w_a_in.shape[0]
    n_b = w_b_in.shape[0]
    gpad = LANES - 2 * M_HEADS
    wts = {
        "a_in": w_a_in[:, :, :A_MAIN].astype(BF16),
        "a_gate": jnp.pad(w_a_in[:, :, A_MAIN:], ((0, 0), (0, 0), (0, gpad))),
        "a_gbias": jnp.pad(b_a_gate, ((0, 0), (0, gpad))).reshape(n_a, 1, LANES).astype(F32),
        "a_norm": a_norm_w.reshape(n_a, 1, HV).astype(F32),
        "a_out": w_a_out.astype(BF16),
        "b_in": w_b_in,
        "b_in_bias": b_b_in.reshape(n_b, 1, -1).astype(F32),
        "b_norm": b_norm_w.reshape(n_b, 1, S_WIDTH).astype(F32),
        "b_out": w_b_out.astype(BF16),
        "b_mix": {},
        "b_mix_bias": {},
        "ln_g": ln_g.reshape(DEPTH, 2, 1, D_MODEL).astype(F32),
        "ln_b": ln_b.reshape(DEPTH, 2, 1, D_MODEL).astype(F32),
        "peer_wq": peer_wq.astype(BF16),
        "peer_keys": peer_keys.astype(BF16),
        "peer_u": peer_u.astype(BF16),
        "peer_v": peer_v.astype(BF16),
        "ple_w": ple_w.astype(BF16),
        "ple_gate": ple_gate_w.astype(BF16),
    }
    for lc in chunk_lens:
        rep = S_CHUNK // lc
        if rep == 1:
            wts["b_mix"][lc] = w_b_s.astype(F32)
        else:
            wts["b_mix"][lc] = jnp.tile(w_b_s[:, :, :lc, :lc], (1, 1, rep, rep)).astype(F32)
        wts["b_mix_bias"][lc] = jnp.swapaxes(jnp.tile(b_b_s[:, :, :lc], (1, 1, rep)), 1, 2).astype(F32)
    return wts


def kernel(x_prompt, x_sample, state_C, state_n, state_m, p_prompt, p_sample, w_a_in, b_a_gate, a_norm_w, w_a_out, w_b_in, b_b_in, b_norm_w, w_b_s, b_b_s, w_b_out, ln_g, ln_b, peer_wq, peer_keys, peer_u, peer_v, ple_w, ple_gate_w):
    bp, sp, d = x_prompt.shape
    bs, ss, _ = x_sample.shape
    n_a = state_C.shape[0]
    chunk_lens = sorted({min(sp, S_CHUNK), min(ss, S_CHUNK)})
    wts = _prep_weights(w_a_in, b_a_gate, a_norm_w, w_a_out, w_b_in, b_b_in, b_norm_w, w_b_s, b_b_s, w_b_out,
                        ln_g, ln_b, peer_wq, peer_keys, peer_u, peer_v, ple_w, ple_gate_w, chunk_lens)

    c0 = jnp.zeros((n_a, bp, M_HEADS, M_DK, M_DV), F32)
    n0 = jnp.zeros((n_a, bp, M_HEADS, M_DK), F32)
    m0 = jnp.zeros((n_a, bp, M_HEADS), F32)
    yp, pc, pn, pm, _ = _trunk(x_prompt.reshape(bp * sp, d), p_prompt.reshape(DEPTH, bp * sp, -1),
                               c0, n0, m0, batch=bp, seq=sp, wts=wts, want_v=False)
    ys, sc, sn, sm, sv = _trunk(x_sample.reshape(bs * ss, d), p_sample.reshape(DEPTH, bs * ss, -1),
                                state_C, state_n, state_m, batch=bs, seq=ss, wts=wts, want_v=True)
    return (yp.reshape(bp, sp, d), ys.reshape(bs, ss, d), pc, pn, pm, sc, sn, sm,
            jnp.stack([v.reshape(bs, ss, S_WIDTH) for v in sv]))
```

```python
import functools

import jax
import jax.numpy as jnp
from jax import lax
from jax.experimental import pallas as pl
from jax.experimental.pallas import tpu as pltpu

F32 = jnp.float32
BF16 = jnp.bfloat16

D_MODEL = 2048
DEPTH = 4
N_MIXERS = 2
M_HEADS = 8
M_DV = D_MODEL // M_HEADS
M_DK = M_DV // 2
HK = M_HEADS * M_DK
HV = M_HEADS * M_DV
A_MAIN = 2 * HK + 2 * HV
S_GROUPS = 8
S_CHUNK = 128
S_WIDTH = D_MODEL
S_DG = S_WIDTH // S_GROUPS
P_HEADS = 8
P_NKEYS = 128
P_NEXP = P_NKEYS * P_NKEYS
P_DH = 128
P_TOPK = 16
ALPHA = (2 * DEPTH) ** 0.25
LN_EPS = 1e-5

LANES = 128
BF16_ROWS = 16
VMEM_LIMIT = 56 * 1024 * 1024
NEG_INF = float("-inf")
TOPK_MARK = 2.0 ** 120

MLSTM_CHUNK = 256
MLSTM_PAD = 16
MLSTM_HEADS_PER_STEP = 4
PEER_TB = 512
PEER_TE = 1024
MM_ROWS = 1024
RES_ROWS = 512


def _params(sem):
    return pltpu.CompilerParams(dimension_semantics=sem, vmem_limit_bytes=VMEM_LIMIT)


def _layer_norm_rows(z, g, b):
    mu = jnp.mean(z, axis=-1, keepdims=True)
    zc = z - mu
    var = jnp.mean(zc * zc, axis=-1, keepdims=True)
    return zc * lax.rsqrt(var + LN_EPS) * g + b


def _mm_kernel(*refs, act, has_bias):
    if has_bias:
        x_ref, w_ref, b_ref, o_ref = refs
    else:
        x_ref, w_ref, o_ref = refs
    acc = jnp.dot(x_ref[...].astype(BF16), w_ref[...].astype(BF16), preferred_element_type=F32)
    if has_bias:
        acc = acc + b_ref[...]
    if act == "gelu":
        acc = jax.nn.gelu(acc)
    o_ref[...] = acc.astype(o_ref.dtype)


def _mm(x, w, layer, bias=None, *, n=None, act=None, out_dtype=F32, tm=None, tn=512):
    m, k = x.shape
    n = w.shape[2] if n is None else n
    tn = min(tn, n)
    if tm is None:
        tm = MM_ROWS if m % MM_ROWS == 0 else m
    in_specs = [
        pl.BlockSpec((tm, k), lambda i, j: (i, 0)),
        pl.BlockSpec((None, k, tn), lambda i, j: (layer, 0, j)),
    ]
    args = [x, w]
    if bias is not None:
        in_specs.append(pl.BlockSpec((None, 1, tn), lambda i, j: (layer, 0, j)))
        args.append(bias)
    return pl.pallas_call(
        functools.partial(_mm_kernel, act=act, has_bias=bias is not None),
        grid=(m // tm, n // tn),
        in_specs=in_specs,
        out_specs=pl.BlockSpec((tm, tn), lambda i, j: (i, j)),
        out_shape=jax.ShapeDtypeStruct((m, n), out_dtype),
        compiler_params=_params(("parallel", "parallel")),
        name="mm",
    )(*args)


def _mm_ln_kernel(a_ref, w_ref, r_ref, g_ref, b_ref, of_ref, ob_ref):
    y = jnp.dot(a_ref[...].astype(BF16), w_ref[...], preferred_element_type=F32)
    o = _layer_norm_rows(ALPHA * r_ref[...] + y, g_ref[...], b_ref[...])
    of_ref[...] = o
    ob_ref[...] = o.astype(BF16)


def _ln_spec(n, li, s):
    return pl.BlockSpec((None, None, 1, n), lambda *_: (li, s, 0, 0))


def _mm_ln(a, w, layer, res, ln_g, ln_b, li, *, tm=RES_ROWS):
    m, k = a.shape
    n = w.shape[2]
    return pl.pallas_call(
        _mm_ln_kernel,
        grid=(m // tm,),
        in_specs=[
            pl.BlockSpec((tm, k), lambda i: (i, 0)),
            pl.BlockSpec((None, k, n), lambda i: (layer, 0, 0)),
            pl.BlockSpec((tm, n), lambda i: (i, 0)),
            _ln_spec(n, li, 0),
            _ln_spec(n, li, 0),
        ],
        out_specs=[pl.BlockSpec((tm, n), lambda i: (i, 0)), pl.BlockSpec((tm, n), lambda i: (i, 0))],
        out_shape=[jax.ShapeDtypeStruct((m, n), F32), jax.ShapeDtypeStruct((m, n), BF16)],
        compiler_params=_params(("parallel",)),
        name="mm_ln",
    )(a, w, res, ln_g, ln_b)


def _ple_kernel(xf_ref, xb_ref, gw_ref, p_ref, pw_ref, of_ref, ob_ref):
    gate = jax.nn.sigmoid(jnp.dot(xb_ref[...], gw_ref[...], preferred_element_type=F32))
    pe = jnp.dot(p_ref[...].astype(BF16), pw_ref[...], preferred_element_type=F32)
    o = xf_ref[...] + gate * pe
    of_ref[...] = o
    ob_ref[...] = o.astype(BF16)


def _ple(xf, xb, gw, p, pw, li, *, tm=RES_ROWS):
    m, n = xf.shape
    kp = p.shape[2]
    return pl.pallas_call(
        _ple_kernel,
        grid=(m // tm,),
        in_specs=[
            pl.BlockSpec((tm, n), lambda i: (i, 0)),
            pl.BlockSpec((tm, n), lambda i: (i, 0)),
            pl.BlockSpec((None, n, n), lambda i: (li, 0, 0)),
            pl.BlockSpec((None, tm, kp), lambda i: (li, i, 0)),
            pl.BlockSpec((None, kp, n), lambda i: (li, 0, 0)),
        ],
        out_specs=[pl.BlockSpec((tm, n), lambda i: (i, 0)), pl.BlockSpec((tm, n), lambda i: (i, 0))],
        out_shape=[jax.ShapeDtypeStruct((m, n), F32), jax.ShapeDtypeStruct((m, n), BF16)],
        compiler_params=_params(("parallel",)),
        name="ple",
    )(xf, xb, gw, p, pw)


def _group_causal(rr, cc, group):
    shift = group.bit_length() - 1
    same = (rr >> shift) == (cc >> shift)
    return jnp.where(rr >= cc, jnp.where(same, 1.0, 0.0), 0.0)


def _split3(x):
    hi = x.astype(BF16)
    r1 = x - hi.astype(F32)
    mid = r1.astype(BF16)
    lo = (r1 - mid.astype(F32)).astype(BF16)
    return hi, mid, lo


def _gate_prep_kernel(g_ref, gb_ref, x_ref, xt_ref, *, seg, valid):
    rows = g_ref.shape[0]
    g = g_ref[...] + gb_ref[...]
    lane = lax.broadcasted_iota(jnp.int32, (rows, LANES), 1)
    r = lax.broadcasted_iota(jnp.int32, (rows, LANES), 0)
    pad = (r & (seg - 1)) >= valid
    lf = jnp.where(pad, 0.0, jax.nn.log_sigmoid(g))
    ig = jnp.where(pad, NEG_INF, g)
    rr = lax.broadcasted_iota(jnp.int32, (rows, rows), 0)
    cc = lax.broadcasted_iota(jnp.int32, (rows, rows), 1)
    tri = _group_causal(rr, cc, seg).astype(BF16)
    hi, mid, lo = _split3(lf)
    bcum = (jnp.dot(tri, hi, preferred_element_type=F32)
            + jnp.dot(tri, mid, preferred_element_type=F32)
            + jnp.dot(tri, lo, preferred_element_type=F32))
    x = jnp.where(lane >= M_HEADS, bcum, ig)
    x_ref[...] = x
    xt_ref[...] = x.T


def _gate_prep(gates, gbias, layer, *, seg, valid, rows):
    m = gates.shape[0]
    return pl.pallas_call(
        functools.partial(_gate_prep_kernel, seg=seg, valid=valid),
        grid=(m // rows,),
        in_specs=[pl.BlockSpec((rows, LANES), lambda i: (i, 0)),
                  pl.BlockSpec((None, 1, LANES), lambda i: (layer, 0, 0))],
        out_specs=[pl.BlockSpec((rows, LANES), lambda i: (i, 0)), pl.BlockSpec((LANES, rows), lambda i: (0, i))],
        out_shape=[jax.ShapeDtypeStruct((m, LANES), F32), jax.ShapeDtypeStruct((LANES, m), F32)],
        compiler_params=_params(("parallel",)),
        name="gate_prep",
    )(gates, gbias)


def _mlstm_kernel(q_ref, k_ref, v_ref, o_ref, x_ref, irow_ref, brow_ref, nw_ref, c0_ref, n0_ref, m0_ref,
                  *rest, bb, seg, hp, n_prev):
    hg_ref, c_ref, n_ref, m_ref, cs, ns, ms = rest[n_prev:]
    h0 = pl.program_id(1) * hp
    c = pl.program_id(2)
    rows = bb * seg
    lane = lax.broadcasted_iota(jnp.int32, (rows, LANES), 1)

    @pl.when(c == 0)
    def _():
        cs[...] = c0_ref[...]
        ns[...] = n0_ref[...]
        ms[...] = jnp.broadcast_to(m0_ref[...], (bb, hp, 1, LANES))

    x = x_ref[...]
    rr = lax.broadcasted_iota(jnp.int32, (rows, rows), 0)
    cc = lax.broadcasted_iota(jnp.int32, (rows, rows), 1)
    visible = _group_causal(rr, cc, seg) > 0.0
    seg_id = lax.broadcasted_iota(jnp.int32, (rows, 1), 0) >> (seg.bit_length() - 1)

    def per_row(vals):
        out = vals[0]
        for bi in range(1, bb):
            out = jnp.where(seg_id == bi, vals[bi], out)
        return out

    ends = [(bi + 1) * seg - 1 for bi in range(bb)]
    for g in range(hp):
        h = h0 + g
        icol = jnp.sum(jnp.where(lane == h, x, 0.0), axis=1, keepdims=True)
        bcol = jnp.sum(jnp.where(lane == h + M_HEADS, x, 0.0), axis=1, keepdims=True)
        irow = irow_ref[g]
        brow = brow_ref[g]
        m_prev_b = [ms[bi, g][:, :1] for bi in range(bb)]
        b_last_b = [bcol[r:r + 1, :] for r in ends]
        m_prev = per_row(m_prev_b)
        b_last = per_row(b_last_b)
        n_rows = per_row([ns[bi, g] for bi in range(bb)])

        q = q_ref[:, g * M_DK:(g + 1) * M_DK].astype(F32) * (M_DK ** -0.5)
        k = k_ref[:, g * M_DK:(g + 1) * M_DK].astype(F32)
        qb = q.astype(BF16)
        kb = k.astype(BF16)
        vcols = slice(g * M_DV, (g + 1) * M_DV)
        vb = v_ref[:, vcols].astype(BF16)

        logd = jnp.where(visible, bcol - brow + irow, NEG_INF)
        inter = bcol + m_prev
        m_row = jnp.maximum(inter, jnp.max(logd, axis=1, keepdims=True))
        dmat = jnp.exp(logd - m_row)
        s = lax.dot_general(qb, kb, (((1,), (1,)), ((), ())), preferred_element_type=F32) * dmat
        w_inter = jnp.exp(inter - m_row)
        qc = [jnp.dot(qb[bi * seg:(bi + 1) * seg], cs[bi, g].astype(BF16), preferred_element_type=F32)
              for bi in range(bb)]
        qc = qc[0] if bb == 1 else jnp.concatenate(qc, axis=0)
        num = w_inter * qc + jnp.dot(s.astype(BF16), vb, preferred_element_type=F32)
        den = w_inter * jnp.sum(q * n_rows, axis=1, keepdims=True) + jnp.sum(s, axis=1, keepdims=True)
        hh = num / jnp.maximum(jnp.abs(den), jnp.exp(-m_row))
        mu = jnp.mean(hh, axis=1, keepdims=True)
        hc = hh - mu
        var = jnp.mean(hc * hc, axis=1, keepdims=True)
        hn = hc * lax.rsqrt(var + LN_EPS) * nw_ref[:, vcols]
        hg_ref[:, vcols] = (hn * jax.nn.sigmoid(o_ref[:, vcols].astype(F32))).astype(hg_ref.dtype)

        m_new_b = [m_row[r:r + 1, :] for r in ends]
        g_tok = jnp.exp(b_last - bcol + icol - per_row(m_new_b))
        kg = g_tok * k
        kgb = kg.astype(BF16)
        for bi in range(bb):
            lo_r, hi_r = bi * seg, (bi + 1) * seg
            g_state = jnp.exp(b_last_b[bi] + m_prev_b[bi] - m_new_b[bi])
            c_new = g_state * cs[bi, g] + lax.dot_general(
                kgb[lo_r:hi_r], vb[lo_r:hi_r], (((0,), (0,)), ((), ())), preferred_element_type=F32)
            n_new = g_state * ns[bi, g] + jnp.sum(kg[lo_r:hi_r], axis=0, keepdims=True)
            m_new = jnp.broadcast_to(m_new_b[bi], (1, LANES))
            cs[bi, g] = c_new
            ns[bi, g] = n_new
            ms[bi, g] = m_new
            c_ref[bi, g] = c_new
            n_ref[bi, g] = n_new
            m_ref[bi, g] = m_new


def _mlstm_scan(proj, gates, gbias, norm_w, c0, n0, m0, layer, prev, *, batch, nc, seg, valid, bb,
                hp=MLSTM_HEADS_PER_STEP):
    rows = bb * seg
    nb = batch // bb
    n_a = c0.shape[0]
    kq, kv = M_DK, M_DV
    xg, xgt = _gate_prep(gates, gbias, layer, seg=seg, valid=valid, rows=rows)
    xgt = xgt[:2 * M_HEADS].reshape(2 * M_HEADS, 1, -1)
    nh = M_HEADS // hp
    row_map = lambda col: (lambda i, h, c: (i * nc + c, col(h)))
    st_map = lambda i, h, c: (layer, i, h, 0, 0)
    in_specs = [
        pl.BlockSpec((rows, hp * kq), row_map(lambda h: h)),
        pl.BlockSpec((rows, hp * kq), row_map(lambda h: HK // (hp * kq) + h)),
        pl.BlockSpec((rows, hp * kv), row_map(lambda h: 2 * HK // (hp * kv) + h)),
        pl.BlockSpec((rows, hp * kv), row_map(lambda h: (2 * HK + HV) // (hp * kv) + h)),
        pl.BlockSpec((rows, LANES), lambda i, h, c: (i * nc + c, 0)),
        pl.BlockSpec((hp, 1, rows), lambda i, h, c: (h, 0, i * nc + c)),
        pl.BlockSpec((hp, 1, rows), lambda i, h, c: (h + nh, 0, i * nc + c)),
        pl.BlockSpec((None, 1, hp * kv), lambda i, h, c: (layer, 0, h)),
        pl.BlockSpec((None, bb, hp, kq, kv), st_map),
        pl.BlockSpec((None, bb, hp, 1, kq), st_map),
        pl.BlockSpec((None, bb, hp, 1, 1), st_map),
    ]
    args = [proj, proj, proj, proj, xg, xgt, xgt, norm_w, c0,
            n0.reshape(n_a, batch, M_HEADS, 1, kq), m0.reshape(n_a, batch, M_HEADS, 1, 1)]
    aliases = {}
    n_prev = len(prev)
    for t, arr in enumerate(prev):
        aliases[len(args)] = 1 + t
        in_specs.append(pl.BlockSpec(memory_space=pl.ANY))
        args.append(arr)
    hg, c_out, n_out, m_out = pl.pallas_call(
        functools.partial(_mlstm_kernel, bb=bb, seg=seg, hp=hp, n_prev=n_prev),
        grid=(nb, nh, nc),
        in_specs=in_specs,
        out_specs=[
            pl.BlockSpec((rows, hp * kv), lambda i, h, c: (i * nc + c, h)),
            pl.BlockSpec((None, bb, hp, kq, kv), st_map),
            pl.BlockSpec((None, bb, hp, 1, kq), st_map),
            pl.BlockSpec((None, bb, hp, 1, LANES), st_map),
        ],
        out_shape=[
            jax.ShapeDtypeStruct((batch * nc * seg, HV), BF16),
            jax.ShapeDtypeStruct((n_a, batch, M_HEADS, kq, kv), F32),
            jax.ShapeDtypeStruct((n_a, batch, M_HEADS, 1, kq), F32),
            jax.ShapeDtypeStruct((n_a, batch, M_HEADS, 1, LANES), F32),
        ],
        scratch_shapes=[
            pltpu.VMEM((bb, hp, kq, kv), F32),
            pltpu.VMEM((bb, hp, 1, kq), F32),
            pltpu.VMEM((bb, hp, 1, LANES), F32),
        ],
        input_output_aliases=aliases,
        compiler_params=_params(("parallel", "parallel", "arbitrary")),
        name="mlstm_scan",
    )(*args)
    return hg, (c_out, n_out, m_out)


def _mix_kernel(h_ref, w_ref, bias_ref, vw_ref, um_ref, *maybe_v_ref, lc):
    t = h_ref.shape[0]
    vraw = h_ref[:, S_WIDTH:].astype(F32)
    mu = jnp.mean(vraw, axis=-1, keepdims=True)
    vc = vraw - mu
    var = jnp.mean(vc * vc, axis=-1, keepdims=True)
    v = vc * lax.rsqrt(var + LN_EPS) * vw_ref[...]
    for v_ref in maybe_v_ref:
        v_ref[...] = v
    vb = v.astype(BF16)
    rr = lax.broadcasted_iota(jnp.int32, (t, t), 0)
    cc = lax.broadcasted_iota(jnp.int32, (t, t), 1)
    keep = _group_causal(rr, cc, lc)
    for g in range(S_GROUPS):
        sl = slice(g * S_DG, (g + 1) * S_DG)
        wg = (w_ref[g] * keep).astype(BF16)
        mixed = jnp.dot(wg, vb[:, sl], preferred_element_type=F32) + bias_ref[:, g:g + 1]
        um_ref[:, sl] = (h_ref[:, sl].astype(F32) * mixed).astype(BF16)


def _mix(hdn, wmix, bias_t, vnorm_w, layer, *, lc, emit_v):
    m = hdn.shape[0]
    t = S_CHUNK
    row_spec = pl.BlockSpec((t, S_WIDTH), lambda i: (i, 0))
    out_specs = [row_spec, row_spec] if emit_v else [row_spec]
    out_shape = [jax.ShapeDtypeStruct((m, S_WIDTH), BF16)]
    if emit_v:
        out_shape.append(jax.ShapeDtypeStruct((m, S_WIDTH), F32))
    outs = pl.pallas_call(
        functools.partial(_mix_kernel, lc=lc),
        grid=(m // t,),
        in_specs=[
            pl.BlockSpec((t, 2 * S_WIDTH), lambda i: (i, 0)),
            pl.BlockSpec((None, S_GROUPS, t, t), lambda i: (layer, 0, 0, 0)),
            pl.BlockSpec((None, t, S_GROUPS), lambda i: (layer, 0, 0)),
            pl.BlockSpec((None, 1, S_WIDTH), lambda i: (layer, 0, 0)),
        ],
        out_specs=out_specs,
        out_shape=out_shape,
        compiler_params=_params(("parallel",)),
        name="chunk_mix",
    )(hdn, wmix, bias_t, vnorm_w)
    return (outs[0], outs[1]) if emit_v else (outs[0], None)


def _route_kernel(xb_ref, wq_ref, keys_ref, r1_ref, e1_ref, nn_ref, c0_ref, q_s, sc_s, wk_s, rk_s, sv_s):
    tm = xb_ref.shape[0]
    q_s[...] = jnp.dot(xb_ref[...], wq_ref[...], preferred_element_type=F32)
    kidx = lax.broadcasted_iota(jnp.int32, (P_NKEYS, tm), 0).astype(F32)
    for h in range(P_HEADS):
        for c in range(2):
            hc = 2 * h + c
            qhc = q_s[:, hc * P_DH:(hc + 1) * P_DH].astype(BF16)
            sc = lax.dot_general(keys_ref[c], qhc, (((1,), (1,)), ((), ())), preferred_element_type=F32)
            sc_s[hc] = sc

        def top16(exact, h=h):
            for c in range(2):
                wk_s[c] = sc_s[2 * h + c]
                if exact:
                    rk_s[2 * h + c] = jnp.full((P_NKEYS, tm), float(P_NKEYS), F32)

            def body(kk, carry):
                kf = lax.convert_element_type(kk, F32)
                for c in range(2):
                    hc = 2 * h + c
                    s = wk_s[c]
                    mx = jnp.max(s, axis=0, keepdims=True)
                    sel = s == mx
                    if exact:
                        idx = jnp.min(jnp.where(sel, kidx, float(P_NKEYS)), axis=0, keepdims=True)
                        sel = kidx == idx
                        rk_s[hc] = jnp.where(sel, kf, rk_s[hc])
                        wk_s[c] = jnp.where(sel, NEG_INF, s)
                    else:
                        wk_s[c] = jnp.where(sel, (kf + 1.0) * -TOPK_MARK, s)
                    sv_s[c, kk, h:h + 1, :] = mx
                return carry

            lax.fori_loop(0, P_TOPK, body, 0)

        top16(exact=False)
        removed = jnp.zeros((1, tm), F32)
        for c in range(2):
            w = wk_s[c]
            gone = w <= -TOPK_MARK
            removed = removed + jnp.sum(jnp.where(gone, 1.0, 0.0), axis=0, keepdims=True)
            rk_s[2 * h + c] = jnp.where(gone, w * (-1.0 / TOPK_MARK) - 1.0, float(P_NKEYS))
        tied = jnp.max(jnp.abs(removed - 2.0 * P_TOPK)) > 0.0

        @pl.when(tied)
        def _(top16=top16):
            top16(exact=True)

    sv0 = [sv_s[0, k] for k in range(P_TOPK)]
    sv1 = [sv_s[1, k] for k in range(P_TOPK)]
    cnt = [jnp.zeros((P_HEADS, tm), F32) for _ in range(P_TOPK)]
    front = [sv0[k] + sv1[0] for k in range(P_TOPK)]
    top = front[0]
    z = jnp.zeros((P_HEADS, tm), F32)
    for r in range(P_TOPK):
        live = min(r + 1, P_TOPK)
        mx = front[0]
        for k in range(1, live):
            mx = jnp.maximum(mx, front[k])
        pick = jnp.full((P_HEADS, tm), float(P_TOPK), F32)
        for k in reversed(range(live)):
            pick = jnp.where(front[k] == mx, float(k), pick)
        z = z + jnp.exp(mx - top)
        hits = [pick == float(k) for k in range(live)]
        newcnt = jnp.zeros((P_HEADS, tm), F32)
        for k in range(live):
            cnt[k] = cnt[k] + jnp.where(hits[k], 1.0, 0.0)
            newcnt = jnp.where(hits[k], cnt[k], newcnt)
        nxt = jnp.full((P_HEADS, tm), NEG_INF, F32)
        for j in range(1, min(r + 2, P_TOPK)):
            nxt = jnp.where(newcnt == float(j), sv1[j], nxt)
        for k in range(live):
            front[k] = jnp.where(hits[k], sv0[k] + nxt, front[k])
    zinv = 1.0 / z

    for h in range(P_HEADS):
        rank0 = rk_s[2 * h]
        nn = jnp.zeros((P_NKEYS, tm), F32)
        for k in range(P_TOPK):
            nn = jnp.where(rank0 == float(k), cnt[k][h:h + 1, :], nn)
        nn_ref[h] = nn
        c0_ref[h] = jnp.exp(sc_s[2 * h] - sv0[0][h:h + 1, :]) * zinv[h:h + 1, :]
        r1_ref[h] = pltpu.bitcast(rk_s[2 * h + 1].astype(BF16), jnp.int32)
        e1_ref[h] = pltpu.bitcast(jnp.exp(sc_s[2 * h + 1] - sv1[0][h:h + 1, :]).astype(BF16), jnp.int32)


def _route(xb, wq, keys, li, *, tm=512):
    m, k = xb.shape
    tab = jax.ShapeDtypeStruct((P_HEADS, P_NKEYS, m), F32)
    tab16 = jax.ShapeDtypeStruct((P_HEADS, P_NKEYS // 2, m), jnp.int32)
    tab_spec = pl.BlockSpec((P_HEADS, P_NKEYS, tm), lambda i: (0, 0, i))
    tab16_spec = pl.BlockSpec((P_HEADS, P_NKEYS // 2, tm), lambda i: (0, 0, i))
    return pl.pallas_call(
        _route_kernel,
        grid=(m // tm,),
        in_specs=[
            pl.BlockSpec((tm, k), lambda i: (i, 0)),
            pl.BlockSpec((None, k, P_HEADS * 2 * P_DH), lambda i: (li, 0, 0)),
            pl.BlockSpec((None, 2, P_NKEYS, P_DH), lambda i: (li, 0, 0, 0)),
        ],
        out_specs=[tab16_spec, tab16_spec, tab_spec, tab_spec],
        out_shape=[tab16, tab16, tab, tab],
        scratch_shapes=[
            pltpu.VMEM((tm, P_HEADS * 2 * P_DH), F32),
            pltpu.VMEM((2 * P_HEADS, P_NKEYS, tm), F32),
            pltpu.VMEM((2, P_NKEYS, tm), F32),
            pltpu.VMEM((2 * P_HEADS, P_NKEYS, tm), F32),
            pltpu.VMEM((2, P_TOPK, P_HEADS, tm), F32),
        ],
        compiler_params=_params(("parallel",)),
        name="peer_route",
    )(xb, wq, keys)


def _gelu_tanh(x):
    c = 0.7978845608028654
    inner = x + x * x * x * 0.044715
    return x * 0.5 * (jnp.tanh(inner * c) + 1.0)


def _peer_kernel(xb_ref, u_ref, v_ref, r1_ref, e1_ref, nn_ref, c0_ref, xf_ref, g_ref, b_ref,
                 of_ref, ob_ref, acc_s, st_s, act_s, xt_s, *, te):
    e = pl.program_id(1)
    last = pl.num_programs(1) - 1
    n_i1 = te // P_NKEYS
    tb = st_s.shape[1]

    @pl.when(e == 0)
    def _():
        acc_s[...] = jnp.zeros_like(acc_s)
        xt_s[...] = xb_ref[...].T

    st_s[...] = jnp.dot(u_ref[...], xt_s[...], preferred_element_type=F32)
    base = e * n_i1
    shape3 = (P_NKEYS // BF16_ROWS, BF16_ROWS, LANES)
    zero = jnp.zeros(shape3, BF16)
    for j in range(n_i1):
        rows = slice(j * P_NKEYS, (j + 1) * P_NKEYS)
        for t in range(tb // LANES):
            cols = slice(t * LANES, (t + 1) * LANES)
            gate = zero
            for h in range(P_HEADS):
                nn = nn_ref[h, pl.ds(base + j, 1), :][:, cols]
                c0 = c0_ref[h, pl.ds(base + j, 1), :][:, cols]
                nn = jnp.broadcast_to(nn, (BF16_ROWS, LANES)).astype(BF16)[None]
                c0 = jnp.broadcast_to(c0, (BF16_ROWS, LANES)).astype(BF16)[None]
                r1 = pltpu.bitcast(r1_ref[h, :, cols], BF16).reshape(shape3)
                e1 = pltpu.bitcast(e1_ref[h, :, cols], BF16).reshape(shape3)
                gate = gate + jnp.where(r1 < nn, e1, zero) * c0
            act = _gelu_tanh(st_s[rows, cols].astype(BF16).reshape(shape3)) * gate
            act_s[rows, cols] = act.reshape(P_NKEYS, LANES)
    acc_s[...] += lax.dot_general(act_s[...], v_ref[...], (((0,), (0,)), ((), ())),
                                  preferred_element_type=F32)

    @pl.when(e == last)
    def _():
        o = _layer_norm_rows(ALPHA * xf_ref[...] + acc_s[...], g_ref[...], b_ref[...])
        of_ref[...] = o
        ob_ref[...] = o.astype(BF16)


def _peer(xb, xf, u, v, r1, e1, nn, c0, ln_g, ln_b, li, *, tb=PEER_TB, te=PEER_TE):
    m, d = xf.shape
    ne = P_NEXP // te
    once = pl.Buffered(1)
    tab_spec = pl.BlockSpec((P_HEADS, P_NKEYS, tb), lambda i, e: (0, 0, i), pipeline_mode=once)
    tab16_spec = pl.BlockSpec((P_HEADS, P_NKEYS // 2, tb), lambda i, e: (0, 0, i), pipeline_mode=once)
    return pl.pallas_call(
        functools.partial(_peer_kernel, te=te),
        grid=(m // tb, ne),
        in_specs=[
            pl.BlockSpec((tb, d), lambda i, e: (i, 0)),
            pl.BlockSpec((None, te, d), lambda i, e: (li, e, 0)),
            pl.BlockSpec((None, te, d), lambda i, e: (li, e, 0)),
            tab16_spec, tab16_spec, tab_spec, tab_spec,
            pl.BlockSpec((tb, d), lambda i, e: (i, 0)),
            _ln_spec(d, li, 1),
            _ln_spec(d, li, 1),
        ],
        out_specs=[pl.BlockSpec((tb, d), lambda i, e: (i, 0), pipeline_mode=once),
                   pl.BlockSpec((tb, d), lambda i, e: (i, 0), pipeline_mode=once)],
        out_shape=[jax.ShapeDtypeStruct((m, d), F32), jax.ShapeDtypeStruct((m, d), BF16)],
        scratch_shapes=[
            pltpu.VMEM((tb, d), F32),
            pltpu.VMEM((te, tb), F32),
            pltpu.VMEM((te, tb), BF16),
            pltpu.VMEM((d, tb), BF16),
        ],
        compiler_params=_params(("parallel", "arbitrary")),
        name="peer_dense",
    )(xb, u, v, r1, e1, nn, c0, xf, ln_g, ln_b)


def _trunk(x, p, c_in, n_in, m_in, *, batch, seq, wts, want_v):
    m = batch * seq
    xf = x
    xb = x.astype(BF16)
    decode = seq < MLSTM_PAD
    lc = min(seq, S_CHUNK)
    n_a = c_in.shape[0]
    states = (jnp.zeros((n_a, batch, M_HEADS, M_DK, M_DV), F32),
              jnp.zeros((n_a, batch, M_HEADS, 1, M_DK), F32),
              jnp.zeros((n_a, batch, M_HEADS, 1, LANES), F32))
    vs = []
    for i in range(DEPTH):
        j = i // N_MIXERS
        if i % N_MIXERS == 0:
            proj = _mm(xb, wts["a_in"], j, n=A_MAIN, out_dtype=BF16)
            gates = _mm(xb, wts["a_gate"], j)
            if decode:
                seg, valid, nc, bb = MLSTM_PAD, seq, 1, 8
                proj, gates = (
                    jnp.pad(a.reshape(batch, seq, -1), ((0, 0), (0, seg - seq), (0, 0))).reshape(batch * seg, -1)
                    for a in (proj, gates))
            else:
                seg, valid, nc, bb = MLSTM_CHUNK, MLSTM_CHUNK, seq // MLSTM_CHUNK, 1
            hg, states = _mlstm_scan(proj, gates, wts["a_gbias"], wts["a_norm"], c_in, n_in, m_in, j, states,
                                     batch=batch, nc=nc, seg=seg, valid=valid, bb=bb)
            if decode:
                hg = hg.reshape(batch, seg, HV)[:, :seq].reshape(m, HV)
            w_out = wts["a_out"]
        else:
            hdn = _mm(xb, wts["b_in"], j, wts["b_in_bias"], act="gelu", out_dtype=BF16)
            hg, v = _mix(hdn, wts["b_mix"][lc], wts["b_mix_bias"][lc], wts["b_norm"], j, lc=lc,
                         emit_v=want_v)
            vs.append(v)
            w_out = wts["b_out"]
        x1f, x1b = _mm_ln(hg, w_out, j, xf, wts["ln_g"], wts["ln_b"], i)
        r1, e1, nn, c0 = _route(x1b, wts["peer_wq"], wts["peer_keys"], i)
        x2f, x2b = _peer(x1b, x1f, wts["peer_u"], wts["peer_v"], r1, e1, nn, c0, wts["ln_g"], wts["ln_b"], i)
        xf, xb = _ple(x2f, x2b, wts["ple_gate"], p, wts["ple_w"], i)
    c_out, n_out, m_out = states
    return xf, c_out, n_out[:, :, :, 0, :], m_out[:, :, :, 0, 0], vs


def _prep_weights(w_a_in, b_a_gate, a_norm_w, w_a_out, w_b_in, b_b_in, b_norm_w, w_b_s, b_b_s, w_b_out,
                  ln_g, ln_b, peer_wq, peer_keys, peer_u, peer_v, ple_w, ple_gate_w, chunk_lens):
    n_a = w_a_in.shape[0]
    n_b = w_b_in.shape[0]
    gpad = LANES - 2 * M_HEADS
    wts = {
        "a_in": w_a_in[:, :, :A_MAIN].astype(BF16),
        "a_gate": jnp.pad(w_a_in[:, :, A_MAIN:], ((0, 0), (0, 0), (0, gpad))),
        "a_gbias": jnp.pad(b_a_gate, ((0, 0), (0, gpad))).reshape(n_a, 1, LANES).astype(F32),
        "a_norm": a_norm_w.reshape(n_a, 1, HV).astype(F32),
        "a_out": w_a_out.astype(BF16),
        "b_in": w_b_in,
        "b_in_bias": b_b_in.reshape(n_b, 1, -1).astype(F32),
        "b_norm": b_norm_w.reshape(n_b, 1, S_WIDTH).astype(F32),
        "b_out": w_b_out.astype(BF16),
        "b_mix": {},
        "b_mix_bias": {},
        "ln_g": ln_g.reshape(DEPTH, 2, 1, D_MODEL).astype(F32),
        "ln_b": ln_b.reshape(DEPTH, 2, 1, D_MODEL).astype(F32),
        "peer_wq": peer_wq.astype(BF16),
        "peer_keys": peer_keys.astype(BF16),
        "peer_u": peer_u.astype(BF16),
        "peer_v": peer_v.astype(BF16),
        "ple_w": ple_w.astype(BF16),
        "ple_gate": ple_gate_w.astype(BF16),
    }
    for lc in chunk_lens:
        rep = S_CHUNK // lc
        if rep == 1:
            wts["b_mix"][lc] = w_b_s.astype(F32)
        else:
            wts["b_mix"][lc] = jnp.tile(w_b_s[:, :, :lc, :lc], (1, 1, rep, rep)).astype(F32)
        wts["b_mix_bias"][lc] = jnp.swapaxes(jnp.tile(b_b_s[:, :, :lc], (1, 1, rep)), 1, 2).astype(F32)
    return wts


def kernel(x_prompt, x_sample, state_C, state_n, state_m, p_prompt, p_sample, w_a_in, b_a_gate, a_norm_w, w_a_out, w_b_in, b_b_in, b_norm_w, w_b_s, b_b_s, w_b_out, ln_g, ln_b, peer_wq, peer_keys, peer_u, peer_v, ple_w, ple_gate_w):
    bp, sp, d = x_prompt.shape
    bs, ss, _ = x_sample.shape
    n_a = state_C.shape[0]
    chunk_lens = sorted({min(sp, S_CHUNK), min(ss, S_CHUNK)})
    wts = _prep_weights(w_a_in, b_a_gate, a_norm_w, w_a_out, w_b_in, b_b_in, b_norm_w, w_b_s, b_b_s, w_b_out,
                        ln_g, ln_b, peer_wq, peer_keys, peer_u, peer_v, ple_w, ple_gate_w, chunk_lens)

    c0 = jnp.zeros((n_a, bp, M_HEADS, M_DK, M_DV), F32)
    n0 = jnp.zeros((n_a, bp, M_HEADS, M_DK), F32)
    m0 = jnp.zeros((n_a, bp, M_HEADS), F32)
    yp, pc, pn, pm, _ = _trunk(x_prompt.reshape(bp * sp, d), p_prompt.reshape(DEPTH, bp * sp, -1),
                               c0, n0, m0, batch=bp, seq=sp, wts=wts, want_v=False)
    ys, sc, sn, sm, sv = _trunk(x_sample.reshape(bs * ss, d), p_sample.reshape(DEPTH, bs * ss, -1),
                                state_C, state_n, state_m, batch=bs, seq=ss, wts=wts, want_v=True)
    return (yp.reshape(bp, sp, d), ys.reshape(bs, ss, d), pc, pn, pm, sc, sn, sm,
            jnp.stack([v.reshape(bs, ss, S_WIDTH) for v in sv]))
```

```python
import functools

import jax
import jax.numpy as jnp
from jax import lax
from jax.experimental import pallas as pl
from jax.experimental.pallas import tpu as pltpu

F32 = jnp.float32
BF16 = jnp.bfloat16

D_MODEL = 2048
DEPTH = 4
N_MIXERS = 2
M_HEADS = 8
M_DV = D_MODEL // M_HEADS
M_DK = M_DV // 2
HK = M_HEADS * M_DK
HV = M_HEADS * M_DV
A_MAIN = 2 * HK + 2 * HV
S_GROUPS = 8
S_CHUNK = 128
S_WIDTH = D_MODEL
S_DG = S_WIDTH // S_GROUPS
P_HEADS = 8
P_NKEYS = 128
P_NEXP = P_NKEYS * P_NKEYS
P_DH = 128
P_TOPK = 16
ALPHA = (2 * DEPTH) ** 0.25
LN_EPS = 1e-5

LANES = 128
BF16_ROWS = 16
VMEM_LIMIT = 56 * 1024 * 1024
NEG_INF = float("-inf")
TOPK_MARK = 2.0 ** 120

MLSTM_CHUNK = 512
MLSTM_PAD = 16
MLSTM_HEADS_PER_STEP = 4
PEER_TB = 512
PEER_TE = 1024
MM_ROWS = 1024
RES_ROWS = 512


def _params(sem):
    return pltpu.CompilerParams(dimension_semantics=sem, vmem_limit_bytes=VMEM_LIMIT)


def _layer_norm_rows(z, g, b):
    mu = jnp.mean(z, axis=-1, keepdims=True)
    zc = z - mu
    var = jnp.mean(zc * zc, axis=-1, keepdims=True)
    return zc * lax.rsqrt(var + LN_EPS) * g + b


def _mm_kernel(*refs, act, has_bias):
    if has_bias:
        x_ref, w_ref, b_ref, o_ref = refs
    else:
        x_ref, w_ref, o_ref = refs
    acc = jnp.dot(x_ref[...].astype(BF16), w_ref[...].astype(BF16), preferred_element_type=F32)
    if has_bias:
        acc = acc + b_ref[...]
    if act == "gelu":
        acc = jax.nn.gelu(acc)
    o_ref[...] = acc.astype(o_ref.dtype)


def _mm(x, w, layer, bias=None, *, n=None, act=None, out_dtype=F32, tm=None, tn=512):
    m, k = x.shape
    n = w.shape[2] if n is None else n
    tn = min(tn, n)
    if tm is None:
        tm = MM_ROWS if m % MM_ROWS == 0 else m
    in_specs = [
        pl.BlockSpec((tm, k), lambda i, j: (i, 0)),
        pl.BlockSpec((None, k, tn), lambda i, j: (layer, 0, j)),
    ]
    args = [x, w]
    if bias is not None:
        in_specs.append(pl.BlockSpec((None, 1, tn), lambda i, j: (layer, 0, j)))
        args.append(bias)
    return pl.pallas_call(
        functools.partial(_mm_kernel, act=act, has_bias=bias is not None),
        grid=(m // tm, n // tn),
        in_specs=in_specs,
        out_specs=pl.BlockSpec((tm, tn), lambda i, j: (i, j)),
        out_shape=jax.ShapeDtypeStruct((m, n), out_dtype),
        compiler_params=_params(("parallel", "parallel")),
        name="mm",
    )(*args)


def _mm_ln_kernel(a_ref, w_ref, r_ref, g_ref, b_ref, of_ref, ob_ref):
    y = jnp.dot(a_ref[...].astype(BF16), w_ref[...], preferred_element_type=F32)
    o = _layer_norm_rows(ALPHA * r_ref[...] + y, g_ref[...], b_ref[...])
    of_ref[...] = o
    ob_ref[...] = o.astype(BF16)


def _ln_spec(n, li, s):
    return pl.BlockSpec((None, None, 1, n), lambda *_: (li, s, 0, 0))


def _mm_ln(a, w, layer, res, ln_g, ln_b, li, *, tm=RES_ROWS):
    m, k = a.shape
    n = w.shape[2]
    return pl.pallas_call(
        _mm_ln_kernel,
        grid=(m // tm,),
        in_specs=[
            pl.BlockSpec((tm, k), lambda i: (i, 0)),
            pl.BlockSpec((None, k, n), lambda i: (layer, 0, 0)),
            pl.BlockSpec((tm, n), lambda i: (i, 0)),
            _ln_spec(n, li, 0),
            _ln_spec(n, li, 0),
        ],
        out_specs=[pl.BlockSpec((tm, n), lambda i: (i, 0)), pl.BlockSpec((tm, n), lambda i: (i, 0))],
        out_shape=[jax.ShapeDtypeStruct((m, n), F32), jax.ShapeDtypeStruct((m, n), BF16)],
        compiler_params=_params(("parallel",)),
        name="mm_ln",
    )(a, w, res, ln_g, ln_b)


def _ple_kernel(xf_ref, xb_ref, gw_ref, p_ref, pw_ref, of_ref, ob_ref):
    gate = jax.nn.sigmoid(jnp.dot(xb_ref[...], gw_ref[...], preferred_element_type=F32))
    pe = jnp.dot(p_ref[...].astype(BF16), pw_ref[...], preferred_element_type=F32)
    o = xf_ref[...] + gate * pe
    of_ref[...] = o
    ob_ref[...] = o.astype(BF16)


def _ple(xf, xb, gw, p, pw, li, *, tm=RES_ROWS):
    m, n = xf.shape
    kp = p.shape[2]
    return pl.pallas_call(
        _ple_kernel,
        grid=(m // tm,),
        in_specs=[
            pl.BlockSpec((tm, n), lambda i: (i, 0)),
            pl.BlockSpec((tm, n), lambda i: (i, 0)),
            pl.BlockSpec((None, n, n), lambda i: (li, 0, 0)),
            pl.BlockSpec((None, tm, kp), lambda i: (li, i, 0)),
            pl.BlockSpec((None, kp, n), lambda i: (li, 0, 0)),
        ],
        out_specs=[pl.BlockSpec((tm, n), lambda i: (i, 0)), pl.BlockSpec((tm, n), lambda i: (i, 0))],
        out_shape=[jax.ShapeDtypeStruct((m, n), F32), jax.ShapeDtypeStruct((m, n), BF16)],
        compiler_params=_params(("parallel",)),
        name="ple",
    )(xf, xb, gw, p, pw)


def _group_causal(rr, cc, group):
    shift = group.bit_length() - 1
    same = (rr >> shift) == (cc >> shift)
    return jnp.where(rr >= cc, jnp.where(same, 1.0, 0.0), 0.0)


def _split3(x):
    hi = x.astype(BF16)
    r1 = x - hi.astype(F32)
    mid = r1.astype(BF16)
    lo = (r1 - mid.astype(F32)).astype(BF16)
    return hi, mid, lo


def _gate_prep_kernel(g_ref, gb_ref, x_ref, xt_ref, *, seg, valid):
    rows = g_ref.shape[0]
    g = g_ref[...] + gb_ref[...]
    lane = lax.broadcasted_iota(jnp.int32, (rows, LANES), 1)
    r = lax.broadcasted_iota(jnp.int32, (rows, LANES), 0)
    pad = (r & (seg - 1)) >= valid
    lf = jnp.where(pad, 0.0, jax.nn.log_sigmoid(g))
    ig = jnp.where(pad, NEG_INF, g)
    rr = lax.broadcasted_iota(jnp.int32, (rows, rows), 0)
    cc = lax.broadcasted_iota(jnp.int32, (rows, rows), 1)
    tri = _group_causal(rr, cc, seg).astype(BF16)
    hi, mid, lo = _split3(lf)
    bcum = (jnp.dot(tri, hi, preferred_element_type=F32)
            + jnp.dot(tri, mid, preferred_element_type=F32)
            + jnp.dot(tri, lo, preferred_element_type=F32))
    x = jnp.where(lane >= M_HEADS, bcum, ig)
    x_ref[...] = x
    xt_ref[...] = x.T


def _gate_prep(gates, gbias, layer, *, seg, valid, rows):
    m = gates.shape[0]
    return pl.pallas_call(
        functools.partial(_gate_prep_kernel, seg=seg, valid=valid),
        grid=(m // rows,),
        in_specs=[pl.BlockSpec((rows, LANES), lambda i: (i, 0)),
                  pl.BlockSpec((None, 1, LANES), lambda i: (layer, 0, 0))],
        out_specs=[pl.BlockSpec((rows, LANES), lambda i: (i, 0)), pl.BlockSpec((LANES, rows), lambda i: (0, i))],
        out_shape=[jax.ShapeDtypeStruct((m, LANES), F32), jax.ShapeDtypeStruct((LANES, m), F32)],
        compiler_params=_params(("parallel",)),
        name="gate_prep",
    )(gates, gbias)


def _mlstm_kernel(q_ref, k_ref, v_ref, o_ref, x_ref, irow_ref, brow_ref, nw_ref, c0_ref, n0_ref, m0_ref,
                  *rest, bb, seg, hp, n_prev):
    hg_ref, c_ref, n_ref, m_ref, cs, ns, ms = rest[n_prev:]
    h0 = pl.program_id(1) * hp
    c = pl.program_id(2)
    rows = bb * seg
    lane = lax.broadcasted_iota(jnp.int32, (rows, LANES), 1)

    @pl.when(c == 0)
    def _():
        cs[...] = c0_ref[...]
        ns[...] = n0_ref[...]
        ms[...] = jnp.broadcast_to(m0_ref[...], (bb, hp, 1, LANES))

    x = x_ref[...]
    rr = lax.broadcasted_iota(jnp.int32, (rows, rows), 0)
    cc = lax.broadcasted_iota(jnp.int32, (rows, rows), 1)
    visible = _group_causal(rr, cc, seg) > 0.0
    seg_id = lax.broadcasted_iota(jnp.int32, (rows, 1), 0) >> (seg.bit_length() - 1)

    def per_row(vals):
        out = vals[0]
        for bi in range(1, bb):
            out = jnp.where(seg_id == bi, vals[bi], out)
        return out

    ends = [(bi + 1) * seg - 1 for bi in range(bb)]
    for g in range(hp):
        h = h0 + g
        icol = jnp.sum(jnp.where(lane == h, x, 0.0), axis=1, keepdims=True)
        bcol = jnp.sum(jnp.where(lane == h + M_HEADS, x, 0.0), axis=1, keepdims=True)
        irow = irow_ref[g]
        brow = brow_ref[g]
        m_prev_b = [ms[bi, g][:, :1] for bi in range(bb)]
        b_last_b = [bcol[r:r + 1, :] for r in ends]
        m_prev = per_row(m_prev_b)
        b_last = per_row(b_last_b)
        n_rows = per_row([ns[bi, g] for bi in range(bb)])

        q = q_ref[:, g * M_DK:(g + 1) * M_DK].astype(F32) * (M_DK ** -0.5)
        k = k_ref[:, g * M_DK:(g + 1) * M_DK].astype(F32)
        qb = q.astype(BF16)
        kb = k.astype(BF16)
        vcols = slice(g * M_DV, (g + 1) * M_DV)
        vb = v_ref[:, vcols].astype(BF16)

        logd = jnp.where(visible, bcol - brow + irow, NEG_INF)
        inter = bcol + m_prev
        m_row = jnp.maximum(inter, jnp.max(logd, axis=1, keepdims=True))
        dmat = jnp.exp(logd - m_row)
        s = lax.dot_general(qb, kb, (((1,), (1,)), ((), ())), preferred_element_type=F32) * dmat
        w_inter = jnp.exp(inter - m_row)
        qc = [jnp.dot(qb[bi * seg:(bi + 1) * seg], cs[bi, g].astype(BF16), preferred_element_type=F32)
              for bi in range(bb)]
        qc = qc[0] if bb == 1 else jnp.concatenate(qc, axis=0)
        num = w_inter * qc + jnp.dot(s.astype(BF16), vb, preferred_element_type=F32)
        den = w_inter * jnp.sum(q * n_rows, axis=1, keepdims=True) + jnp.sum(s, axis=1, keepdims=True)
        hh = num / jnp.maximum(jnp.abs(den), jnp.exp(-m_row))
        mu = jnp.mean(hh, axis=1, keepdims=True)
        hc = hh - mu
        var = jnp.mean(hc * hc, axis=1, keepdims=True)
        hn = hc * lax.rsqrt(var + LN_EPS) * nw_ref[:, vcols]
        hg_ref[:, vcols] = (hn * jax.nn.sigmoid(o_ref[:, vcols].astype(F32))).astype(hg_ref.dtype)

        m_new_b = [m_row[r:r + 1, :] for r in ends]
        g_tok = jnp.exp(b_last - bcol + icol - per_row(m_new_b))
        kg = g_tok * k
        kgb = kg.astype(BF16)
        for bi in range(bb):
            lo_r, hi_r = bi * seg, (bi + 1) * seg
            g_state = jnp.exp(b_last_b[bi] + m_prev_b[bi] - m_new_b[bi])
            c_new = g_state * cs[bi, g] + lax.dot_general(
                kgb[lo_r:hi_r], vb[lo_r:hi_r], (((0,), (0,)), ((), ())), preferred_element_type=F32)
            n_new = g_state * ns[bi, g] + jnp.sum(kg[lo_r:hi_r], axis=0, keepdims=True)
            m_new = jnp.broadcast_to(m_new_b[bi], (1, LANES))
            cs[bi, g] = c_new
            ns[bi, g] = n_new
            ms[bi, g] = m_new
            c_ref[bi, g] = c_new
            n_ref[bi, g] = n_new
            m_ref[bi, g] = m_new


def _mlstm_scan(proj, gates, gbias, norm_w, c0, n0, m0, layer, prev, *, batch, nc, seg, valid, bb,
                hp=MLSTM_HEADS_PER_STEP):
    rows = bb * seg
    nb = batch // bb
    n_a = c0.shape[0]
    kq, kv = M_DK, M_DV
    xg, xgt = _gate_prep(gates, gbias, layer, seg=seg, valid=valid, rows=rows)
    xgt = xgt[:2 * M_HEADS].reshape(2 * M_HEADS, 1, -1)
    nh = M_HEADS // hp
    row_map = lambda col: (lambda i, h, c: (i * nc + c, col(h)))
    st_map = lambda i, h, c: (layer, i, h, 0, 0)
    in_specs = [
        pl.BlockSpec((rows, hp * kq), row_map(lambda h: h)),
        pl.BlockSpec((rows, hp * kq), row_map(lambda h: HK // (hp * kq) + h)),
        pl.BlockSpec((rows, hp * kv), row_map(lambda h: 2 * HK // (hp * kv) + h)),
        pl.BlockSpec((rows, hp * kv), row_map(lambda h: (2 * HK + HV) // (hp * kv) + h)),
        pl.BlockSpec((rows, LANES), lambda i, h, c: (i * nc + c, 0)),
        pl.BlockSpec((hp, 1, rows), lambda i, h, c: (h, 0, i * nc + c)),
        pl.BlockSpec((hp, 1, rows), lambda i, h, c: (h + nh, 0, i * nc + c)),
        pl.BlockSpec((None, 1, hp * kv), lambda i, h, c: (layer, 0, h)),
        pl.BlockSpec((None, bb, hp, kq, kv), st_map),
        pl.BlockSpec((None, bb, hp, 1, kq), st_map),
        pl.BlockSpec((None, bb, hp, 1, 1), st_map),
    ]
    args = [proj, proj, proj, proj, xg, xgt, xgt, norm_w, c0,
            n0.reshape(n_a, batch, M_HEADS, 1, kq), m0.reshape(n_a, batch, M_HEADS, 1, 1)]
    aliases = {}
    n_prev = len(prev)
    for t, arr in enumerate(prev):
        aliases[len(args)] = 1 + t
        in_specs.append(pl.BlockSpec(memory_space=pl.ANY))
        args.append(arr)
    hg, c_out, n_out, m_out = pl.pallas_call(
        functools.partial(_mlstm_kernel, bb=bb, seg=seg, hp=hp, n_prev=n_prev),
        grid=(nb, nh, nc),
        in_specs=in_specs,
        out_specs=[
            pl.BlockSpec((rows, hp * kv), lambda i, h, c: (i * nc + c, h)),
            pl.BlockSpec((None, bb, hp, kq, kv), st_map),
            pl.BlockSpec((None, bb, hp, 1, kq), st_map),
            pl.BlockSpec((None, bb, hp, 1, LANES), st_map),
        ],
        out_shape=[
            jax.ShapeDtypeStruct((batch * nc * seg, HV), BF16),
            jax.ShapeDtypeStruct((n_a, batch, M_HEADS, kq, kv), F32),
            jax.ShapeDtypeStruct((n_a, batch, M_HEADS, 1, kq), F32),
            jax.ShapeDtypeStruct((n_a, batch, M_HEADS, 1, LANES), F32),
        ],
        scratch_shapes=[
            pltpu.VMEM((bb, hp, kq, kv), F32),
            pltpu.VMEM((bb, hp, 1, kq), F32),
            pltpu.VMEM((bb, hp, 1, LANES), F32),
        ],
        input_output_aliases=aliases,
        compiler_params=_params(("parallel", "parallel", "arbitrary")),
        name="mlstm_scan",
    )(*args)
    return hg, (c_out, n_out, m_out)


def _mix_kernel(h_ref, w_ref, bias_ref, vw_ref, um_ref, *maybe_v_ref, lc):
    t = h_ref.shape[0]
    vraw = h_ref[:, S_WIDTH:].astype(F32)
    mu = jnp.mean(vraw, axis=-1, keepdims=True)
    vc = vraw - mu
    var = jnp.mean(vc * vc, axis=-1, keepdims=True)
    v = vc * lax.rsqrt(var + LN_EPS) * vw_ref[...]
    for v_ref in maybe_v_ref:
        v_ref[...] = v
    vb = v.astype(BF16)
    rr = lax.broadcasted_iota(jnp.int32, (t, t), 0)
    cc = lax.broadcasted_iota(jnp.int32, (t, t), 1)
    keep = _group_causal(rr, cc, lc)
    for g in range(S_GROUPS):
        sl = slice(g * S_DG, (g + 1) * S_DG)
        wg = (w_ref[g] * keep).astype(BF16)
        mixed = jnp.dot(wg, vb[:, sl], preferred_element_type=F32) + bias_ref[:, g:g + 1]
        um_ref[:, sl] = (h_ref[:, sl].astype(F32) * mixed).astype(BF16)


def _mix(hdn, wmix, bias_t, vnorm_w, layer, *, lc, emit_v):
    m = hdn.shape[0]
    t = S_CHUNK
    row_spec = pl.BlockSpec((t, S_WIDTH), lambda i: (i, 0))
    out_specs = [row_spec, row_spec] if emit_v else [row_spec]
    out_shape = [jax.ShapeDtypeStruct((m, S_WIDTH), BF16)]
    if emit_v:
        out_shape.append(jax.ShapeDtypeStruct((m, S_WIDTH), F32))
    outs = pl.pallas_call(
        functools.partial(_mix_kernel, lc=lc),
        grid=(m // t,),
        in_specs=[
            pl.BlockSpec((t, 2 * S_WIDTH), lambda i: (i, 0)),
            pl.BlockSpec((None, S_GROUPS, t, t), lambda i: (layer, 0, 0, 0)),
            pl.BlockSpec((None, t, S_GROUPS), lambda i: (layer, 0, 0)),
            pl.BlockSpec((None, 1, S_WIDTH), lambda i: (layer, 0, 0)),
        ],
        out_specs=out_specs,
        out_shape=out_shape,
        compiler_params=_params(("parallel",)),
        name="chunk_mix",
    )(hdn, wmix, bias_t, vnorm_w)
    return (outs[0], outs[1]) if emit_v else (outs[0], None)


def _route_kernel(xb_ref, wq_ref, keys_ref, r1_ref, e1_ref, nn_ref, c0_ref, q_s, sc_s, wk_s, rk_s, sv_s):
    tm = xb_ref.shape[0]
    q_s[...] = jnp.dot(xb_ref[...], wq_ref[...], preferred_element_type=F32)
    kidx = lax.broadcasted_iota(jnp.int32, (P_NKEYS, tm), 0).astype(F32)
    for h in range(P_HEADS):
        for c in range(2):
            hc = 2 * h + c
            qhc = q_s[:, hc * P_DH:(hc + 1) * P_DH].astype(BF16)
            sc = lax.dot_general(keys_ref[c], qhc, (((1,), (1,)), ((), ())), preferred_element_type=F32)
            sc_s[hc] = sc

        def top16(exact, h=h):
            for c in range(2):
                wk_s[c] = sc_s[2 * h + c]
                if exact:
                    rk_s[2 * h + c] = jnp.full((P_NKEYS, tm), float(P_NKEYS), F32)

            def body(kk, carry):
                kf = lax.convert_element_type(kk, F32)
                for c in range(2):
                    hc = 2 * h + c
                    s = wk_s[c]
                    mx = jnp.max(s, axis=0, keepdims=True)
                    sel = s == mx
                    if exact:
                        idx = jnp.min(jnp.where(sel, kidx, float(P_NKEYS)), axis=0, keepdims=True)
                        sel = kidx == idx
                        rk_s[hc] = jnp.where(sel, kf, rk_s[hc])
                        wk_s[c] = jnp.where(sel, NEG_INF, s)
                    else:
                        wk_s[c] = jnp.where(sel, (kf + 1.0) * -TOPK_MARK, s)
                    sv_s[c, kk, h:h + 1, :] = mx
                return carry

            lax.fori_loop(0, P_TOPK, body, 0)

        top16(exact=False)
        removed = jnp.zeros((1, tm), F32)
        for c in range(2):
            w = wk_s[c]
            gone = w <= -TOPK_MARK
            removed = removed + jnp.sum(jnp.where(gone, 1.0, 0.0), axis=0, keepdims=True)
            rk_s[2 * h + c] = jnp.where(gone, w * (-1.0 / TOPK_MARK) - 1.0, float(P_NKEYS))
        tied = jnp.max(jnp.abs(removed - 2.0 * P_TOPK)) > 0.0

        @pl.when(tied)
        def _(top16=top16):
            top16(exact=True)

    sv0 = [sv_s[0, k] for k in range(P_TOPK)]
    sv1 = [sv_s[1, k] for k in range(P_TOPK)]
    cnt = [jnp.zeros((P_HEADS, tm), F32) for _ in range(P_TOPK)]
    front = [sv0[k] + sv1[0] for k in range(P_TOPK)]
    top = front[0]
    z = jnp.zeros((P_HEADS, tm), F32)
    for r in range(P_TOPK):
        live = min(r + 1, P_TOPK)
        mx = front[0]
        for k in range(1, live):
            mx = jnp.maximum(mx, front[k])
        pick = jnp.full((P_HEADS, tm), float(P_TOPK), F32)
        for k in reversed(range(live)):
            pick = jnp.where(front[k] == mx, float(k), pick)
        z = z + jnp.exp(mx - top)
        hits = [pick == float(k) for k in range(live)]
        newcnt = jnp.zeros((P_HEADS, tm), F32)
        for k in range(live):
            cnt[k] = cnt[k] + jnp.where(hits[k], 1.0, 0.0)
            newcnt = jnp.where(hits[k], cnt[k], newcnt)
        nxt = jnp.full((P_HEADS, tm), NEG_INF, F32)
        for j in range(1, min(r + 2, P_TOPK)):
            nxt = jnp.where(newcnt == float(j), sv1[j], nxt)
        for k in range(live):
            front[k] = jnp.where(hits[k], sv0[k] + nxt, front[k])
    zinv = 1.0 / z

    for h in range(P_HEADS):
        rank0 = rk_s[2 * h]
        nn = jnp.zeros((P_NKEYS, tm), F32)
        for k in range(P_TOPK):
            nn = jnp.where(rank0 == float(k), cnt[k][h:h + 1, :], nn)
        nn_ref[h] = nn
        c0_ref[h] = jnp.exp(sc_s[2 * h] - sv0[0][h:h + 1, :]) * zinv[h:h + 1, :]
        r1_ref[h] = pltpu.bitcast(rk_s[2 * h + 1].astype(BF16), jnp.int32)
        e1_ref[h] = pltpu.bitcast(jnp.exp(sc_s[2 * h + 1] - sv1[0][h:h + 1, :]).astype(BF16), jnp.int32)


def _route(xb, wq, keys, li, *, tm=512):
    m, k = xb.shape
    tab = jax.ShapeDtypeStruct((P_HEADS, P_NKEYS, m), F32)
    tab16 = jax.ShapeDtypeStruct((P_HEADS, P_NKEYS // 2, m), jnp.int32)
    tab_spec = pl.BlockSpec((P_HEADS, P_NKEYS, tm), lambda i: (0, 0, i))
    tab16_spec = pl.BlockSpec((P_HEADS, P_NKEYS // 2, tm), lambda i: (0, 0, i))
    return pl.pallas_call(
        _route_kernel,
        grid=(m // tm,),
        in_specs=[
            pl.BlockSpec((tm, k), lambda i: (i, 0)),
            pl.BlockSpec((None, k, P_HEADS * 2 * P_DH), lambda i: (li, 0, 0)),
            pl.BlockSpec((None, 2, P_NKEYS, P_DH), lambda i: (li, 0, 0, 0)),
        ],
        out_specs=[tab16_spec, tab16_spec, tab_spec, tab_spec],
        out_shape=[tab16, tab16, tab, tab],
        scratch_shapes=[
            pltpu.VMEM((tm, P_HEADS * 2 * P_DH), F32),
            pltpu.VMEM((2 * P_HEADS, P_NKEYS, tm), F32),
            pltpu.VMEM((2, P_NKEYS, tm), F32),
            pltpu.VMEM((2 * P_HEADS, P_NKEYS, tm), F32),
            pltpu.VMEM((2, P_TOPK, P_HEADS, tm), F32),
        ],
        compiler_params=_params(("parallel",)),
        name="peer_route",
    )(xb, wq, keys)


def _gelu_tanh(x):
    c = 0.7978845608028654
    inner = x + x * x * x * 0.044715
    return x * 0.5 * (jnp.tanh(inner * c) + 1.0)


def _peer_kernel(xb_ref, u_ref, v_ref, r1_ref, e1_ref, nn_ref, c0_ref, xf_ref, g_ref, b_ref,
                 of_ref, ob_ref, acc_s, st_s, act_s, xt_s, *, te):
    e = pl.program_id(1)
    last = pl.num_programs(1) - 1
    n_i1 = te // P_NKEYS
    tb = st_s.shape[1]

    @pl.when(e == 0)
    def _():
        acc_s[...] = jnp.zeros_like(acc_s)
        xt_s[...] = xb_ref[...].T

    st_s[...] = jnp.dot(u_ref[...], xt_s[...], preferred_element_type=F32)
    base = e * n_i1
    shape3 = (P_NKEYS // BF16_ROWS, BF16_ROWS, LANES)
    zero = jnp.zeros(shape3, BF16)
    for j in range(n_i1):
        rows = slice(j * P_NKEYS, (j + 1) * P_NKEYS)
        for t in range(tb // LANES):
            cols = slice(t * LANES, (t + 1) * LANES)
            gate = zero
            for h in range(P_HEADS):
                nn = nn_ref[h, pl.ds(base + j, 1), :][:, cols]
                c0 = c0_ref[h, pl.ds(base + j, 1), :][:, cols]
                nn = jnp.broadcast_to(nn, (BF16_ROWS, LANES)).astype(BF16)[None]
                c0 = jnp.broadcast_to(c0, (BF16_ROWS, LANES)).astype(BF16)[None]
                r1 = pltpu.bitcast(r1_ref[h, :, cols], BF16).reshape(shape3)
                e1 = pltpu.bitcast(e1_ref[h, :, cols], BF16).reshape(shape3)
                gate = gate + jnp.where(r1 < nn, e1, zero) * c0
            act = _gelu_tanh(st_s[rows, cols].astype(BF16).reshape(shape3)) * gate
            act_s[rows, cols] = act.reshape(P_NKEYS, LANES)
    acc_s[...] += lax.dot_general(act_s[...], v_ref[...], (((0,), (0,)), ((), ())),
                                  preferred_element_type=F32)

    @pl.when(e == last)
    def _():
        o = _layer_norm_rows(ALPHA * xf_ref[...] + acc_s[...], g_ref[...], b_ref[...])
        of_ref[...] = o
        ob_ref[...] = o.astype(BF16)


def _peer(xb, xf, u, v, r1, e1, nn, c0, ln_g, ln_b, li, *, tb=PEER_TB, te=PEER_TE):
    m, d = xf.shape
    ne = P_NEXP // te
    once = pl.Buffered(1)
    tab_spec = pl.BlockSpec((P_HEADS, P_NKEYS, tb), lambda i, e: (0, 0, i), pipeline_mode=once)
    tab16_spec = pl.BlockSpec((P_HEADS, P_NKEYS // 2, tb), lambda i, e: (0, 0, i), pipeline_mode=once)
    return pl.pallas_call(
        functools.partial(_peer_kernel, te=te),
        grid=(m // tb, ne),
        in_specs=[
            pl.BlockSpec((tb, d), lambda i, e: (i, 0)),
            pl.BlockSpec((None, te, d), lambda i, e: (li, e, 0)),
            pl.BlockSpec((None, te, d), lambda i, e: (li, e, 0)),
            tab16_spec, tab16_spec, tab_spec, tab_spec,
            pl.BlockSpec((tb, d), lambda i, e: (i, 0)),
            _ln_spec(d, li, 1),
            _ln_spec(d, li, 1),
        ],
        out_specs=[pl.BlockSpec((tb, d), lambda i, e: (i, 0), pipeline_mode=once),
                   pl.BlockSpec((tb, d), lambda i, e: (i, 0), pipeline_mode=once)],
        out_shape=[jax.ShapeDtypeStruct((m, d), F32), jax.ShapeDtypeStruct((m, d), BF16)],
        scratch_shapes=[
            pltpu.VMEM((tb, d), F32),
            pltpu.VMEM((te, tb), F32),
            pltpu.VMEM((te, tb), BF16),
            pltpu.VMEM((d, tb), BF16),
        ],
        compiler_params=_params(("parallel", "arbitrary")),
        name="peer_dense",
    )(xb, u, v, r1, e1, nn, c0, xf, ln_g, ln_b)


def _trunk(x, p, c_in, n_in, m_in, *, batch, seq, wts, want_v):
    m = batch * seq
    xf = x
    xb = x.astype(BF16)
    decode = seq < MLSTM_PAD
    lc = min(seq, S_CHUNK)
    n_a = c_in.shape[0]
    states = (jnp.zeros((n_a, batch, M_HEADS, M_DK, M_DV), F32),
              jnp.zeros((n_a, batch, M_HEADS, 1, M_DK), F32),
              jnp.zeros((n_a, batch, M_HEADS, 1, LANES), F32))
    vs = []
    for i in range(DEPTH):
        j = i // N_MIXERS
        if i % N_MIXERS == 0:
            proj = _mm(xb, wts["a_in"], j, n=A_MAIN, out_dtype=BF16)
            gates = _mm(xb, wts["a_gate"], j)
            if decode:
                seg, valid, nc, bb = MLSTM_PAD, seq, 1, 8
                proj, gates = (
                    jnp.pad(a.reshape(batch, seq, -1), ((0, 0), (0, seg - seq), (0, 0))).reshape(batch * seg, -1)
                    for a in (proj, gates))
            else:
                seg, valid, nc, bb = MLSTM_CHUNK, MLSTM_CHUNK, seq // MLSTM_CHUNK, 1
            hg, states = _mlstm_scan(proj, gates, wts["a_gbias"], wts["a_norm"], c_in, n_in, m_in, j, states,
                                     batch=batch, nc=nc, seg=seg, valid=valid, bb=bb)
            if decode:
                hg = hg.reshape(batch, seg, HV)[:, :seq].reshape(m, HV)
            w_out = wts["a_out"]
        else:
            hdn = _mm(xb, wts["b_in"], j, wts["b_in_bias"], act="gelu", out_dtype=BF16)
            hg, v = _mix(hdn, wts["b_mix"][lc], wts["b_mix_bias"][lc], wts["b_norm"], j, lc=lc,
                         emit_v=want_v)
            vs.append(v)
            w_out = wts["b_out"]
        x1f, x1b = _mm_ln(hg, w_out, j, xf, wts["ln_g"], wts["ln_b"], i)
        r1, e1, nn, c0 = _route(x1b, wts["peer_wq"], wts["peer_keys"], i)
        x2f, x2b = _peer(x1b, x1f, wts["peer_u"], wts["peer_v"], r1, e1, nn, c0, wts["ln_g"], wts["ln_b"], i)
        xf, xb = _ple(x2f, x2b, wts["ple_gate"], p, wts["ple_w"], i)
    c_out, n_out, m_out = states
    return xf, c_out, n_out[:, :, :, 0, :], m_out[:, :, :, 0, 0], vs


def _prep_weights(w_a_in, b_a_gate, a_norm_w, w_a_out, w_b_in, b_b_in, b_norm_w, w_b_s, b_b_s, w_b_out,
                  ln_g, ln_b, peer_wq, peer_keys, peer_u, peer_v, ple_w, ple_gate_w, chunk_lens):
    n_a = w_a_in.shape[0]
    n_b = w_b_in.shape[0]
    gpad = LANES - 2 * M_HEADS
    wts = {
        "a_in": w_a_in[:, :, :A_MAIN].astype(BF16),
        "a_gate": jnp.pad(w_a_in[:, :, A_MAIN:], ((0, 0), (0, 0), (0, gpad))),
        "a_gbias": jnp.pad(b_a_gate, ((0, 0), (0, gpad))).reshape(n_a, 1, LANES).astype(F32),
        "a_norm": a_norm_w.reshape(n_a, 1, HV).astype(F32),
        "a_out": w_a_out.astype(BF16),
        "b_in": w_b_in,
        "b_in_bias": b_b_in.reshape(n_b, 1, -1).astype(F32),
        "b_norm": b_norm_w.reshape(n_b, 1, S_WIDTH).astype(F32),
        "b_out": w_b_out.astype(BF16),
        "b_mix": {},
        "b_mix_bias": {},
        "ln_g": ln_g.reshape(DEPTH, 2, 1, D_MODEL).astype(F32),
        "ln_b": ln_b.reshape(DEPTH, 2, 1, D_MODEL).astype(F32),
        "peer_wq": peer_wq.astype(BF16),
        "peer_keys": peer_keys.astype(BF16),
        "peer_u": peer_u.astype(BF16),
        "peer_v": peer_v.astype(BF16),
        "ple_w": ple_w.astype(BF16),
        "ple_gate": ple_gate_w.astype(BF16),
    }
    for lc in chunk_lens:
        rep = S_CHUNK // lc
        if rep == 1:
            wts["b_mix"][lc] = w_b_s.astype(F32)
        else:
            wts["b_mix"][lc] = jnp.tile(w_b_s[:, :, :lc, :lc], (1, 1, rep, rep)).astype(F32)
        wts["b_mix_bias"][lc] = jnp.swapaxes(jnp.tile(b_b_s[:, :, :lc], (1, 1, rep)), 1, 2).astype(F32)
    return wts


def kernel(x_prompt, x_sample, state_C, state_n, state_m, p_prompt, p_sample, w_a_in, b_a_gate, a_norm_w, w_a_out, w_b_in, b_b_in, b_norm_w, w_b_s, b_b_s, w_b_out, ln_g, ln_b, peer_wq, peer_keys, peer_u, peer_v, ple_w, ple_gate_w):
    bp, sp, d = x_prompt.shape
    bs, ss, _ = x_sample.shape
    n_a = state_C.shape[0]
    chunk_lens = sorted({min(sp, S_CHUNK), min(ss, S_CHUNK)})
    wts = _prep_weights(w_a_in, b_a_gate, a_norm_w, w_a_out, w_b_in, b_b_in, b_norm_w, w_b_s, b_b_s, w_b_out,
                        ln_g, ln_b, peer_wq, peer_keys, peer_u, peer_v, ple_w, ple_gate_w, chunk_lens)

    c0 = jnp.zeros((n_a, bp, M_HEADS, M_DK, M_DV), F32)
    n0 = jnp.zeros((n_a, bp, M_HEADS, M_DK), F32)
    m0 = jnp.zeros((n_a, bp, M_HEADS), F32)
    yp, pc, pn, pm, _ = _trunk(x_prompt.reshape(bp * sp, d), p_prompt.reshape(DEPTH, bp * sp, -1),
                               c0, n0, m0, batch=bp, seq=sp, wts=wts, want_v=False)
    ys, sc, sn, sm, sv = _trunk(x_sample.reshape(bs * ss, d), p_sample.reshape(DEPTH, bs * ss, -1),
                                state_C, state_n, state_m, batch=bs, seq=ss, wts=wts, want_v=True)
    return (yp.reshape(bp, sp, d), ys.reshape(bs, ss, d), pc, pn, pm, sc, sn, sm,
            jnp.stack([v.reshape(bs, ss, S_WIDTH) for v in sv]))
```

```python
import functools

import jax
import jax.numpy as jnp
from jax import lax
from jax.experimental import pallas as pl
from jax.experimental.pallas import tpu as pltpu

F32 = jnp.float32
BF16 = jnp.bfloat16

D_MODEL = 2048
DEPTH = 4
N_MIXERS = 2
M_HEADS = 8
M_DV = D_MODEL // M_HEADS
M_DK = M_DV // 2
HK = M_HEADS * M_DK
HV = M_HEADS * M_DV
A_MAIN = 2 * HK + 2 * HV
S_GROUPS = 8
S_CHUNK = 128
S_WIDTH = D_MODEL
S_DG = S_WIDTH // S_GROUPS
P_HEADS = 8
P_NKEYS = 128
P_NEXP = P_NKEYS * P_NKEYS
P_DH = 128
P_TOPK = 16
ALPHA = (2 * DEPTH) ** 0.25
LN_EPS = 1e-5

LANES = 128
BF16_ROWS = 16
VMEM_LIMIT = 56 * 1024 * 1024
NEG_INF = float("-inf")
TOPK_MARK = 2.0 ** 120

MLSTM_CHUNK = 512
MLSTM_PAD = 16
MLSTM_HEADS_PER_STEP = 4
PEER_TB = 512
PEER_TE = 1024
MM_ROWS = 1024
RES_ROWS = 512


def _params(sem):
    return pltpu.CompilerParams(dimension_semantics=sem, vmem_limit_bytes=VMEM_LIMIT)


def _layer_norm_rows(z, g, b):
    mu = jnp.mean(z, axis=-1, keepdims=True)
    zc = z - mu
    var = jnp.mean(zc * zc, axis=-1, keepdims=True)
    return zc * lax.rsqrt(var + LN_EPS) * g + b


def _mm_kernel(*refs, act, has_bias):
    if has_bias:
        x_ref, w_ref, b_ref, o_ref = refs
    else:
        x_ref, w_ref, o_ref = refs
    acc = jnp.dot(x_ref[...].astype(BF16), w_ref[...].astype(BF16), preferred_element_type=F32)
    if has_bias:
        acc = acc + b_ref[...]
    if act == "gelu":
        acc = jax.nn.gelu(acc)
    o_ref[...] = acc.astype(o_ref.dtype)


def _mm(x, w, layer, bias=None, *, n=None, act=None, out_dtype=F32, tm=None, tn=1024):
    m, k = x.shape
    n = w.shape[2] if n is None else n
    tn = min(tn, n)
    if tm is None:
        tm = MM_ROWS if m % MM_ROWS == 0 else m
    in_specs = [
        pl.BlockSpec((tm, k), lambda i, j: (i, 0)),
        pl.BlockSpec((None, k, tn), lambda i, j: (layer, 0, j)),
    ]
    args = [x, w]
    if bias is not None:
        in_specs.append(pl.BlockSpec((None, 1, tn), lambda i, j: (layer, 0, j)))
        args.append(bias)
    return pl.pallas_call(
        functools.partial(_mm_kernel, act=act, has_bias=bias is not None),
        grid=(m // tm, n // tn),
        in_specs=in_specs,
        out_specs=pl.BlockSpec((tm, tn), lambda i, j: (i, j)),
        out_shape=jax.ShapeDtypeStruct((m, n), out_dtype),
        compiler_params=_params(("parallel", "parallel")),
        name="mm",
    )(*args)


def _mm_ln_kernel(a_ref, w_ref, r_ref, g_ref, b_ref, of_ref, ob_ref):
    y = jnp.dot(a_ref[...].astype(BF16), w_ref[...], preferred_element_type=F32)
    o = _layer_norm_rows(ALPHA * r_ref[...] + y, g_ref[...], b_ref[...])
    of_ref[...] = o
    ob_ref[...] = o.astype(BF16)


def _ln_spec(n, li, s):
    return pl.BlockSpec((None, None, 1, n), lambda *_: (li, s, 0, 0))


def _mm_ln(a, w, layer, res, ln_g, ln_b, li, *, tm=RES_ROWS):
    m, k = a.shape
    n = w.shape[2]
    return pl.pallas_call(
        _mm_ln_kernel,
        grid=(m // tm,),
        in_specs=[
            pl.BlockSpec((tm, k), lambda i: (i, 0)),
            pl.BlockSpec((None, k, n), lambda i: (layer, 0, 0)),
            pl.BlockSpec((tm, n), lambda i: (i, 0)),
            _ln_spec(n, li, 0),
            _ln_spec(n, li, 0),
        ],
        out_specs=[pl.BlockSpec((tm, n), lambda i: (i, 0)), pl.BlockSpec((tm, n), lambda i: (i, 0))],
        out_shape=[jax.ShapeDtypeStruct((m, n), F32), jax.ShapeDtypeStruct((m, n), BF16)],
        compiler_params=_params(("parallel",)),
        name="mm_ln",
    )(a, w, res, ln_g, ln_b)


def _ple_kernel(xf_ref, xb_ref, gw_ref, p_ref, pw_ref, of_ref, ob_ref):
    gate = jax.nn.sigmoid(jnp.dot(xb_ref[...], gw_ref[...], preferred_element_type=F32))
    pe = jnp.dot(p_ref[...].astype(BF16), pw_ref[...], preferred_element_type=F32)
    o = xf_ref[...] + gate * pe
    of_ref[...] = o
    ob_ref[...] = o.astype(BF16)


def _ple(xf, xb, gw, p, pw, li, *, tm=RES_ROWS):
    m, n = xf.shape
    kp = p.shape[2]
    return pl.pallas_call(
        _ple_kernel,
        grid=(m // tm,),
        in_specs=[
            pl.BlockSpec((tm, n), lambda i: (i, 0)),
            pl.BlockSpec((tm, n), lambda i: (i, 0)),
            pl.BlockSpec((None, n, n), lambda i: (li, 0, 0)),
            pl.BlockSpec((None, tm, kp), lambda i: (li, i, 0)),
            pl.BlockSpec((None, kp, n), lambda i: (li, 0, 0)),
        ],
        out_specs=[pl.BlockSpec((tm, n), lambda i: (i, 0)), pl.BlockSpec((tm, n), lambda i: (i, 0))],
        out_shape=[jax.ShapeDtypeStruct((m, n), F32), jax.ShapeDtypeStruct((m, n), BF16)],
        compiler_params=_params(("parallel",)),
        name="ple",
    )(xf, xb, gw, p, pw)


def _group_causal(rr, cc, group):
    shift = group.bit_length() - 1
    same = (rr >> shift) == (cc >> shift)
    return jnp.where(rr >= cc, jnp.where(same, 1.0, 0.0), 0.0)


def _split3(x):
    hi = x.astype(BF16)
    r1 = x - hi.astype(F32)
    mid = r1.astype(BF16)
    lo = (r1 - mid.astype(F32)).astype(BF16)
    return hi, mid, lo


def _gate_prep_kernel(g_ref, gb_ref, x_ref, xt_ref, *, seg, valid):
    rows = g_ref.shape[0]
    g = g_ref[...] + gb_ref[...]
    lane = lax.broadcasted_iota(jnp.int32, (rows, LANES), 1)
    r = lax.broadcasted_iota(jnp.int32, (rows, LANES), 0)
    pad = (r & (seg - 1)) >= valid
    lf = jnp.where(pad, 0.0, jax.nn.log_sigmoid(g))
    ig = jnp.where(pad, NEG_INF, g)
    rr = lax.broadcasted_iota(jnp.int32, (rows, rows), 0)
    cc = lax.broadcasted_iota(jnp.int32, (rows, rows), 1)
    tri = _group_causal(rr, cc, seg).astype(BF16)
    hi, mid, lo = _split3(lf)
    bcum = (jnp.dot(tri, hi, preferred_element_type=F32)
            + jnp.dot(tri, mid, preferred_element_type=F32)
            + jnp.dot(tri, lo, preferred_element_type=F32))
    x = jnp.where(lane >= M_HEADS, bcum, ig)
    x_ref[...] = x
    xt_ref[...] = x.T


def _gate_prep(gates, gbias, layer, *, seg, valid, rows):
    m = gates.shape[0]
    return pl.pallas_call(
        functools.partial(_gate_prep_kernel, seg=seg, valid=valid),
        grid=(m // rows,),
        in_specs=[pl.BlockSpec((rows, LANES), lambda i: (i, 0)),
                  pl.BlockSpec((None, 1, LANES), lambda i: (layer, 0, 0))],
        out_specs=[pl.BlockSpec((rows, LANES), lambda i: (i, 0)), pl.BlockSpec((LANES, rows), lambda i: (0, i))],
        out_shape=[jax.ShapeDtypeStruct((m, LANES), F32), jax.ShapeDtypeStruct((LANES, m), F32)],
        compiler_params=_params(("parallel",)),
        name="gate_prep",
    )(gates, gbias)


def _mlstm_kernel(q_ref, k_ref, v_ref, o_ref, x_ref, irow_ref, brow_ref, nw_ref, c0_ref, n0_ref, m0_ref,
                  *rest, bb, seg, hp, n_prev):
    hg_ref, c_ref, n_ref, m_ref, cs, ns, ms = rest[n_prev:]
    h0 = pl.program_id(1) * hp
    c = pl.program_id(2)
    rows = bb * seg
    lane = lax.broadcasted_iota(jnp.int32, (rows, LANES), 1)

    @pl.when(c == 0)
    def _():
        cs[...] = c0_ref[...]
        ns[...] = n0_ref[...]
        ms[...] = jnp.broadcast_to(m0_ref[...], (bb, hp, 1, LANES))

    x = x_ref[...]
    rr = lax.broadcasted_iota(jnp.int32, (rows, rows), 0)
    cc = lax.broadcasted_iota(jnp.int32, (rows, rows), 1)
    visible = _group_causal(rr, cc, seg) > 0.0
    seg_id = lax.broadcasted_iota(jnp.int32, (rows, 1), 0) >> (seg.bit_length() - 1)

    def per_row(vals):
        out = vals[0]
        for bi in range(1, bb):
            out = jnp.where(seg_id == bi, vals[bi], out)
        return out

    ends = [(bi + 1) * seg - 1 for bi in range(bb)]
    for g in range(hp):
        h = h0 + g
        icol = jnp.sum(jnp.where(lane == h, x, 0.0), axis=1, keepdims=True)
        bcol = jnp.sum(jnp.where(lane == h + M_HEADS, x, 0.0), axis=1, keepdims=True)
        irow = irow_ref[g]
        brow = brow_ref[g]
        m_prev_b = [ms[bi, g][:, :1] for bi in range(bb)]
        b_last_b = [bcol[r:r + 1, :] for r in ends]
        m_prev = per_row(m_prev_b)
        b_last = per_row(b_last_b)
        n_rows = per_row([ns[bi, g] for bi in range(bb)])

        q = q_ref[:, g * M_DK:(g + 1) * M_DK].astype(F32) * (M_DK ** -0.5)
        k = k_ref[:, g * M_DK:(g + 1) * M_DK].astype(F32)
        qb = q.astype(BF16)
        kb = k.astype(BF16)
        vcols = slice(g * M_DV, (g + 1) * M_DV)
        vb = v_ref[:, vcols].astype(BF16)

        logd = jnp.where(visible, bcol - brow + irow, NEG_INF)
        inter = bcol + m_prev
        m_row = jnp.maximum(inter, jnp.max(logd, axis=1, keepdims=True))
        dmat = jnp.exp(logd - m_row)
        s = lax.dot_general(qb, kb, (((1,), (1,)), ((), ())), preferred_element_type=F32) * dmat
        w_inter = jnp.exp(inter - m_row)
        qc = [jnp.dot(qb[bi * seg:(bi + 1) * seg], cs[bi, g].astype(BF16), preferred_element_type=F32)
              for bi in range(bb)]
        qc = qc[0] if bb == 1 else jnp.concatenate(qc, axis=0)
        num = w_inter * qc + jnp.dot(s.astype(BF16), vb, preferred_element_type=F32)
        den = w_inter * jnp.sum(q * n_rows, axis=1, keepdims=True) + jnp.sum(s, axis=1, keepdims=True)
        hh = num / jnp.maximum(jnp.abs(den), jnp.exp(-m_row))
        mu = jnp.mean(hh, axis=1, keepdims=True)
        hc = hh - mu
        var = jnp.mean(hc * hc, axis=1, keepdims=True)
        hn = hc * lax.rsqrt(var + LN_EPS) * nw_ref[:, vcols]
        hg_ref[:, vcols] = (hn * jax.nn.sigmoid(o_ref[:, vcols].astype(F32))).astype(hg_ref.dtype)

        m_new_b = [m_row[r:r + 1, :] for r in ends]
        g_tok = jnp.exp(b_last - bcol + icol - per_row(m_new_b))
        kg = g_tok * k
        kgb = kg.astype(BF16)
        for bi in range(bb):
            lo_r, hi_r = bi * seg, (bi + 1) * seg
            g_state = jnp.exp(b_last_b[bi] + m_prev_b[bi] - m_new_b[bi])
            c_new = g_state * cs[bi, g] + lax.dot_general(
                kgb[lo_r:hi_r], vb[lo_r:hi_r], (((0,), (0,)), ((), ())), preferred_element_type=F32)
            n_new = g_state * ns[bi, g] + jnp.sum(kg[lo_r:hi_r], axis=0, keepdims=True)
            m_new = jnp.broadcast_to(m_new_b[bi], (1, LANES))
            cs[bi, g] = c_new
            ns[bi, g] = n_new
            ms[bi, g] = m_new
            c_ref[bi, g] = c_new
            n_ref[bi, g] = n_new
            m_ref[bi, g] = m_new


def _mlstm_scan(proj, gates, gbias, norm_w, c0, n0, m0, layer, prev, *, batch, nc, seg, valid, bb,
                hp=MLSTM_HEADS_PER_STEP):
    rows = bb * seg
    nb = batch // bb
    n_a = c0.shape[0]
    kq, kv = M_DK, M_DV
    xg, xgt = _gate_prep(gates, gbias, layer, seg=seg, valid=valid, rows=rows)
    xgt = xgt[:2 * M_HEADS].reshape(2 * M_HEADS, 1, -1)
    nh = M_HEADS // hp
    row_map = lambda col: (lambda i, h, c: (i * nc + c, col(h)))
    st_map = lambda i, h, c: (layer, i, h, 0, 0)
    in_specs = [
        pl.BlockSpec((rows, hp * kq), row_map(lambda h: h)),
        pl.BlockSpec((rows, hp * kq), row_map(lambda h: HK // (hp * kq) + h)),
        pl.BlockSpec((rows, hp * kv), row_map(lambda h: 2 * HK // (hp * kv) + h)),
        pl.BlockSpec((rows, hp * kv), row_map(lambda h: (2 * HK + HV) // (hp * kv) + h)),
        pl.BlockSpec((rows, LANES), lambda i, h, c: (i * nc + c, 0)),
        pl.BlockSpec((hp, 1, rows), lambda i, h, c: (h, 0, i * nc + c)),
        pl.BlockSpec((hp, 1, rows), lambda i, h, c: (h + nh, 0, i * nc + c)),
        pl.BlockSpec((None, 1, hp * kv), lambda i, h, c: (layer, 0, h)),
        pl.BlockSpec((None, bb, hp, kq, kv), st_map),
        pl.BlockSpec((None, bb, hp, 1, kq), st_map),
        pl.BlockSpec((None, bb, hp, 1, 1), st_map),
    ]
    args = [proj, proj, proj, proj, xg, xgt, xgt, norm_w, c0,
            n0.reshape(n_a, batch, M_HEADS, 1, kq), m0.reshape(n_a, batch, M_HEADS, 1, 1)]
    aliases = {}
    n_prev = len(prev)
    for t, arr in enumerate(prev):
        aliases[len(args)] = 1 + t
        in_specs.append(pl.BlockSpec(memory_space=pl.ANY))
        args.append(arr)
    hg, c_out, n_out, m_out = pl.pallas_call(
        functools.partial(_mlstm_kernel, bb=bb, seg=seg, hp=hp, n_prev=n_prev),
        grid=(nb, nh, nc),
        in_specs=in_specs,
        out_specs=[
            pl.BlockSpec((rows, hp * kv), lambda i, h, c: (i * nc + c, h)),
            pl.BlockSpec((None, bb, hp, kq, kv), st_map),
            pl.BlockSpec((None, bb, hp, 1, kq), st_map),
            pl.BlockSpec((None, bb, hp, 1, LANES), st_map),
        ],
        out_shape=[
            jax.ShapeDtypeStruct((batch * nc * seg, HV), BF16),
            jax.ShapeDtypeStruct((n_a, batch, M_HEADS, kq, kv), F32),
            jax.ShapeDtypeStruct((n_a, batch, M_HEADS, 1, kq), F32),
            jax.ShapeDtypeStruct((n_a, batch, M_HEADS, 1, LANES), F32),
        ],
        scratch_shapes=[
            pltpu.VMEM((bb, hp, kq, kv), F32),
            pltpu.VMEM((bb, hp, 1, kq), F32),
            pltpu.VMEM((bb, hp, 1, LANES), F32),
        ],
        input_output_aliases=aliases,
        compiler_params=_params(("parallel", "parallel", "arbitrary")),
        name="mlstm_scan",
    )(*args)
    return hg, (c_out, n_out, m_out)


def _mix_kernel(h_ref, w_ref, bias_ref, vw_ref, um_ref, *maybe_v_ref, lc):
    t = h_ref.shape[0]
    vraw = h_ref[:, S_WIDTH:].astype(F32)
    mu = jnp.mean(vraw, axis=-1, keepdims=True)
    vc = vraw - mu
    var = jnp.mean(vc * vc, axis=-1, keepdims=True)
    v = vc * lax.rsqrt(var + LN_EPS) * vw_ref[...]
    for v_ref in maybe_v_ref:
        v_ref[...] = v
    vb = v.astype(BF16)
    rr = lax.broadcasted_iota(jnp.int32, (t, t), 0)
    cc = lax.broadcasted_iota(jnp.int32, (t, t), 1)
    keep = _group_causal(rr, cc, lc)
    for g in range(S_GROUPS):
        sl = slice(g * S_DG, (g + 1) * S_DG)
        wg = (w_ref[g] * keep).astype(BF16)
        mixed = jnp.dot(wg, vb[:, sl], preferred_element_type=F32) + bias_ref[:, g:g + 1]
        um_ref[:, sl] = (h_ref[:, sl].astype(F32) * mixed).astype(BF16)


def _mix(hdn, wmix, bias_t, vnorm_w, layer, *, lc, emit_v):
    m = hdn.shape[0]
    t = S_CHUNK
    row_spec = pl.BlockSpec((t, S_WIDTH), lambda i: (i, 0))
    out_specs = [row_spec, row_spec] if emit_v else [row_spec]
    out_shape = [jax.ShapeDtypeStruct((m, S_WIDTH), BF16)]
    if emit_v:
        out_shape.append(jax.ShapeDtypeStruct((m, S_WIDTH), F32))
    outs = pl.pallas_call(
        functools.partial(_mix_kernel, lc=lc),
        grid=(m // t,),
        in_specs=[
            pl.BlockSpec((t, 2 * S_WIDTH), lambda i: (i, 0)),
            pl.BlockSpec((None, S_GROUPS, t, t), lambda i: (layer, 0, 0, 0)),
            pl.BlockSpec((None, t, S_GROUPS), lambda i: (layer, 0, 0)),
            pl.BlockSpec((None, 1, S_WIDTH), lambda i: (layer, 0, 0)),
        ],
        out_specs=out_specs,
        out_shape=out_shape,
        compiler_params=_params(("parallel",)),
        name="chunk_mix",
    )(hdn, wmix, bias_t, vnorm_w)
    return (outs[0], outs[1]) if emit_v else (outs[0], None)


def _route_kernel(xb_ref, wq_ref, keys_ref, r1_ref, e1_ref, nn_ref, c0_ref, q_s, sc_s, wk_s, rk_s, sv_s):
    tm = xb_ref.shape[0]
    q_s[...] = jnp.dot(xb_ref[...], wq_ref[...], preferred_element_type=F32)
    kidx = lax.broadcasted_iota(jnp.int32, (P_NKEYS, tm), 0).astype(F32)
    for h in range(P_HEADS):
        for c in range(2):
            hc = 2 * h + c
            qhc = q_s[:, hc * P_DH:(hc + 1) * P_DH].astype(BF16)
            sc = lax.dot_general(keys_ref[c], qhc, (((1,), (1,)), ((), ())), preferred_element_type=F32)
            sc_s[hc] = sc

        def top16(exact, h=h):
            for c in range(2):
                wk_s[c] = sc_s[2 * h + c]
                if exact:
                    rk_s[2 * h + c] = jnp.full((P_NKEYS, tm), float(P_NKEYS), F32)

            def body(kk, carry):
                kf = lax.convert_element_type(kk, F32)
                for c in range(2):
                    hc = 2 * h + c
                    s = wk_s[c]
                    mx = jnp.max(s, axis=0, keepdims=True)
                    sel = s == mx
                    if exact:
                        idx = jnp.min(jnp.where(sel, kidx, float(P_NKEYS)), axis=0, keepdims=True)
                        sel = kidx == idx
                        rk_s[hc] = jnp.where(sel, kf, rk_s[hc])
                        wk_s[c] = jnp.where(sel, NEG_INF, s)
                    else:
                        wk_s[c] = jnp.where(sel, (kf + 1.0) * -TOPK_MARK, s)
                    sv_s[c, kk, h:h + 1, :] = mx
                return carry

            lax.fori_loop(0, P_TOPK, body, 0)

        top16(exact=False)
        removed = jnp.zeros((1, tm), F32)
        for c in range(2):
            w = wk_s[c]
            gone = w <= -TOPK_MARK
            removed = removed + jnp.sum(jnp.where(gone, 1.0, 0.0), axis=0, keepdims=True)
            rk_s[2 * h + c] = jnp.where(gone, w * (-1.0 / TOPK_MARK) - 1.0, float(P_NKEYS))
        tied = jnp.max(jnp.abs(removed - 2.0 * P_TOPK)) > 0.0

        @pl.when(tied)
        def _(top16=top16):
            top16(exact=True)

    sv0 = [sv_s[0, k] for k in range(P_TOPK)]
    sv1 = [sv_s[1, k] for k in range(P_TOPK)]
    cnt = [jnp.zeros((P_HEADS, tm), F32) for _ in range(P_TOPK)]
    front = [sv0[k] + sv1[0] for k in range(P_TOPK)]
    top = front[0]
    z = jnp.zeros((P_HEADS, tm), F32)
    for r in range(P_TOPK):
        live = min(r + 1, P_TOPK)
        mx = front[0]
        for k in range(1, live):
            mx = jnp.maximum(mx, front[k])
        pick = jnp.full((P_HEADS, tm), float(P_TOPK), F32)
        for k in reversed(range(live)):
            pick = jnp.where(front[k] == mx, float(k), pick)
        z = z + jnp.exp(mx - top)
        hits = [pick == float(k) for k in range(live)]
        newcnt = jnp.zeros((P_HEADS, tm), F32)
        for k in range(live):
            cnt[k] = cnt[k] + jnp.where(hits[k], 1.0, 0.0)
            newcnt = jnp.where(hits[k], cnt[k], newcnt)
        nxt = jnp.full((P_HEADS, tm), NEG_INF, F32)
        for j in range(1, min(r + 2, P_TOPK)):
            nxt = jnp.where(newcnt == float(j), sv1[j], nxt)
        for k in range(live):
            front[k] = jnp.where(hits[k], sv0[k] + nxt, front[k])
    zinv = 1.0 / z

    for h in range(P_HEADS):
        rank0 = rk_s[2 * h]
        nn = jnp.zeros((P_NKEYS, tm), F32)
        for k in range(P_TOPK):
            nn = jnp.where(rank0 == float(k), cnt[k][h:h + 1, :], nn)
        nn_ref[h] = nn
        c0_ref[h] = jnp.exp(sc_s[2 * h] - sv0[0][h:h + 1, :]) * zinv[h:h + 1, :]
        r1_ref[h] = pltpu.bitcast(rk_s[2 * h + 1].astype(BF16), jnp.int32)
        e1_ref[h] = pltpu.bitcast(jnp.exp(sc_s[2 * h + 1] - sv1[0][h:h + 1, :]).astype(BF16), jnp.int32)


def _route(xb, wq, keys, li, *, tm=512):
    m, k = xb.shape
    tab = jax.ShapeDtypeStruct((P_HEADS, P_NKEYS, m), F32)
    tab16 = jax.ShapeDtypeStruct((P_HEADS, P_NKEYS // 2, m), jnp.int32)
    tab_spec = pl.BlockSpec((P_HEADS, P_NKEYS, tm), lambda i: (0, 0, i))
    tab16_spec = pl.BlockSpec((P_HEADS, P_NKEYS // 2, tm), lambda i: (0, 0, i))
    return pl.pallas_call(
        _route_kernel,
        grid=(m // tm,),
        in_specs=[
            pl.BlockSpec((tm, k), lambda i: (i, 0)),
            pl.BlockSpec((None, k, P_HEADS * 2 * P_DH), lambda i: (li, 0, 0)),
            pl.BlockSpec((None, 2, P_NKEYS, P_DH), lambda i: (li, 0, 0, 0)),
        ],
        out_specs=[tab16_spec, tab16_spec, tab_spec, tab_spec],
        out_shape=[tab16, tab16, tab, tab],
        scratch_shapes=[
            pltpu.VMEM((tm, P_HEADS * 2 * P_DH), F32),
            pltpu.VMEM((2 * P_HEADS, P_NKEYS, tm), F32),
            pltpu.VMEM((2, P_NKEYS, tm), F32),
            pltpu.VMEM((2 * P_HEADS, P_NKEYS, tm), F32),
            pltpu.VMEM((2, P_TOPK, P_HEADS, tm), F32),
        ],
        compiler_params=_params(("parallel",)),
        name="peer_route",
    )(xb, wq, keys)


def _gelu_tanh(x):
    c = 0.7978845608028654
    inner = x + x * x * x * 0.044715
    return x * 0.5 * (jnp.tanh(inner * c) + 1.0)


def _peer_kernel(xb_ref, u_ref, v_ref, r1_ref, e1_ref, nn_ref, c0_ref, xf_ref, g_ref, b_ref,
                 of_ref, ob_ref, acc_s, st_s, act_s, xt_s, *, te):
    e = pl.program_id(1)
    last = pl.num_programs(1) - 1
    n_i1 = te // P_NKEYS
    tb = st_s.shape[1]

    @pl.when(e == 0)
    def _():
        acc_s[...] = jnp.zeros_like(acc_s)
        xt_s[...] = xb_ref[...].T

    st_s[...] = jnp.dot(u_ref[...], xt_s[...], preferred_element_type=F32)
    base = e * n_i1
    shape3 = (P_NKEYS // BF16_ROWS, BF16_ROWS, LANES)
    zero = jnp.zeros(shape3, BF16)
    for j in range(n_i1):
        rows = slice(j * P_NKEYS, (j + 1) * P_NKEYS)
        for t in range(tb // LANES):
            cols = slice(t * LANES, (t + 1) * LANES)
            gate = zero
            for h in range(P_HEADS):
                nn = nn_ref[h, pl.ds(base + j, 1), :][:, cols]
                c0 = c0_ref[h, pl.ds(base + j, 1), :][:, cols]
                nn = jnp.broadcast_to(nn, (BF16_ROWS, LANES)).astype(BF16)[None]
                c0 = jnp.broadcast_to(c0, (BF16_ROWS, LANES)).astype(BF16)[None]
                r1 = pltpu.bitcast(r1_ref[h, :, cols], BF16).reshape(shape3)
                e1 = pltpu.bitcast(e1_ref[h, :, cols], BF16).reshape(shape3)
                gate = gate + jnp.where(r1 < nn, e1, zero) * c0
            act = _gelu_tanh(st_s[rows, cols].astype(BF16).reshape(shape3)) * gate
            act_s[rows, cols] = act.reshape(P_NKEYS, LANES)
    acc_s[...] += lax.dot_general(act_s[...], v_ref[...], (((0,), (0,)), ((), ())),
                                  preferred_element_type=F32)

    @pl.when(e == last)
    def _():
        o = _layer_norm_rows(ALPHA * xf_ref[...] + acc_s[...], g_ref[...], b_ref[...])
        of_ref[...] = o
        ob_ref[...] = o.astype(BF16)


def _peer(xb, xf, u, v, r1, e1, nn, c0, ln_g, ln_b, li, *, tb=PEER_TB, te=PEER_TE):
    m, d = xf.shape
    ne = P_NEXP // te
    once = pl.Buffered(1)
    tab_spec = pl.BlockSpec((P_HEADS, P_NKEYS, tb), lambda i, e: (0, 0, i), pipeline_mode=once)
    tab16_spec = pl.BlockSpec((P_HEADS, P_NKEYS // 2, tb), lambda i, e: (0, 0, i), pipeline_mode=once)
    return pl.pallas_call(
        functools.partial(_peer_kernel, te=te),
        grid=(m // tb, ne),
        in_specs=[
            pl.BlockSpec((tb, d), lambda i, e: (i, 0)),
            pl.BlockSpec((None, te, d), lambda i, e: (li, e, 0)),
            pl.BlockSpec((None, te, d), lambda i, e: (li, e, 0)),
            tab16_spec, tab16_spec, tab_spec, tab_spec,
            pl.BlockSpec((tb, d), lambda i, e: (i, 0)),
            _ln_spec(d, li, 1),
            _ln_spec(d, li, 1),
        ],
        out_specs=[pl.BlockSpec((tb, d), lambda i, e: (i, 0), pipeline_mode=once),
                   pl.BlockSpec((tb, d), lambda i, e: (i, 0), pipeline_mode=once)],
        out_shape=[jax.ShapeDtypeStruct((m, d), F32), jax.ShapeDtypeStruct((m, d), BF16)],
        scratch_shapes=[
            pltpu.VMEM((tb, d), F32),
            pltpu.VMEM((te, tb), F32),
            pltpu.VMEM((te, tb), BF16),
            pltpu.VMEM((d, tb), BF16),
        ],
        compiler_params=_params(("parallel", "arbitrary")),
        name="peer_dense",
    )(xb, u, v, r1, e1, nn, c0, xf, ln_g, ln_b)


def _trunk(x, p, c_in, n_in, m_in, *, batch, seq, wts, want_v):
    m = batch * seq
    xf = x
    xb = x.astype(BF16)
    decode = seq < MLSTM_PAD
    lc = min(seq, S_CHUNK)
    n_a = c_in.shape[0]
    states = (jnp.zeros((n_a, batch, M_HEADS, M_DK, M_DV), F32),
              jnp.zeros((n_a, batch, M_HEADS, 1, M_DK), F32),
              jnp.zeros((n_a, batch, M_HEADS, 1, LANES), F32))
    vs = []
    for i in range(DEPTH):
        j = i // N_MIXERS
        if i % N_MIXERS == 0:
            proj = _mm(xb, wts["a_in"], j, n=A_MAIN, out_dtype=BF16)
            gates = _mm(xb, wts["a_gate"], j)
            if decode:
                seg, valid, nc, bb = MLSTM_PAD, seq, 1, 8
                proj, gates = (
                    jnp.pad(a.reshape(batch, seq, -1), ((0, 0), (0, seg - seq), (0, 0))).reshape(batch * seg, -1)
                    for a in (proj, gates))
            else:
                seg, valid, nc, bb = MLSTM_CHUNK, MLSTM_CHUNK, seq // MLSTM_CHUNK, 1
            hg, states = _mlstm_scan(proj, gates, wts["a_gbias"], wts["a_norm"], c_in, n_in, m_in, j, states,
                                     batch=batch, nc=nc, seg=seg, valid=valid, bb=bb)
            if decode:
                hg = hg.reshape(batch, seg, HV)[:, :seq].reshape(m, HV)
            w_out = wts["a_out"]
        else:
            hdn = _mm(xb, wts["b_in"], j, wts["b_in_bias"], act="gelu", out_dtype=BF16)
            hg, v = _mix(hdn, wts["b_mix"][lc], wts["b_mix_bias"][lc], wts["b_norm"], j, lc=lc,
                         emit_v=want_v)
            vs.append(v)
            w_out = wts["b_out"]
        x1f, x1b = _mm_ln(hg, w_out, j, xf, wts["ln_g"], wts["ln_b"], i)
        r1, e1, nn, c0 = _route(x1b, wts["peer_wq"], wts["peer_keys"], i)
        x2f, x2b = _peer(x1b, x1f, wts["peer_u"], wts["peer_v"], r1, e1, nn, c0, wts["ln_g"], wts["ln_b"], i)
        xf, xb = _ple(x2f, x2b, wts["ple_gate"], p, wts["ple_w"], i)
    c_out, n_out, m_out = states
    return xf, c_out, n_out[:, :, :, 0, :], m_out[:, :, :, 0, 0], vs


def _prep_weights(w_a_in, b_a_gate, a_norm_w, w_a_out, w_b_in, b_b_in, b_norm_w, w_b_s, b_b_s, w_b_out,
                  ln_g, ln_b, peer_wq, peer_keys, peer_u, peer_v, ple_w, ple_gate_w, chunk_lens):
    n_a = w_a_in.shape[0]
    n_b = w_b_in.shape[0]
    gpad = LANES - 2 * M_HEADS
    wts = {
        "a_in": w_a_in[:, :, :A_MAIN].astype(BF16),
        "a_gate": jnp.pad(w_a_in[:, :, A_MAIN:], ((0, 0), (0, 0), (0, gpad))),
        "a_gbias": jnp.pad(b_a_gate, ((0, 0), (0, gpad))).reshape(n_a, 1, LANES).astype(F32),
        "a_norm": a_norm_w.reshape(n_a, 1, HV).astype(F32),
        "a_out": w_a_out.astype(BF16),
        "b_in": w_b_in,
        "b_in_bias": b_b_in.reshape(n_b, 1, -1).astype(F32),
        "b_norm": b_norm_w.reshape(n_b, 1, S_WIDTH).astype(F32),
        "b_out": w_b_out.astype(BF16),
        "b_mix": {},
        "b_mix_bias": {},
        "ln_g": ln_g.reshape(DEPTH, 2, 1, D_MODEL).astype(F32),
        "ln_b": ln_b.reshape(DEPTH, 2, 1, D_MODEL).astype(F32),
        "peer_wq": peer_wq.astype(BF16),
        "peer_keys": peer_keys.astype(BF16),
        "peer_u": peer_u.astype(BF16),
        "peer_v": peer_v.astype(BF16),
        "ple_w": ple_w.astype(BF16),
        "ple_gate": ple_gate_w.astype(BF16),
    }
    for lc in chunk_lens:
        rep = S_CHUNK // lc
        if rep == 1:
            wts["b_mix"][lc] = w_b_s.astype(F32)
        else:
            wts["b_mix"][lc] = jnp.tile(w_b_s[:, :, :lc, :lc], (1, 1, rep, rep)).astype(F32)
        wts["b_mix_bias"][lc] = jnp.swapaxes(jnp.tile(b_b_s[:, :, :lc], (1, 1, rep)), 1, 2).astype(F32)
    return wts


def kernel(x_prompt, x_sample, state_C, state_n, state_m, p_prompt, p_sample, w_a_in, b_a_gate, a_norm_w, w_a_out, w_b_in, b_b_in, b_norm_w, w_b_s, b_b_s, w_b_out, ln_g, ln_b, peer_wq, peer_keys, peer_u, peer_v, ple_w, ple_gate_w):
    bp, sp, d = x_prompt.shape
    bs, ss, _ = x_sample.shape
    n_a = state_C.shape[0]
    chunk_lens = sorted({min(sp, S_CHUNK), min(ss, S_CHUNK)})
    wts = _prep_weights(w_a_in, b_a_gate, a_norm_w, w_a_out, w_b_in, b_b_in, b_norm_w, w_b_s, b_b_s, w_b_out,
                        ln_g, ln_b, peer_wq, peer_keys, peer_u, peer_v, ple_w, ple_gate_w, chunk_lens)

    c0 = jnp.zeros((n_a, bp, M_HEADS, M_DK, M_DV), F32)
    n0 = jnp.zeros((n_a, bp, M_HEADS, M_DK), F32)
    m0 = jnp.zeros((n_a, bp, M_HEADS), F32)
    yp, pc, pn, pm, _ = _trunk(x_prompt.reshape(bp * sp, d), p_prompt.reshape(DEPTH, bp * sp, -1),
                               c0, n0, m0, batch=bp, seq=sp, wts=wts, want_v=False)
    ys, sc, sn, sm, sv = _trunk(x_sample.reshape(bs * ss, d), p_sample.reshape(DEPTH, bs * ss, -1),
                                state_C, state_n, state_m, batch=bs, seq=ss, wts=wts, want_v=True)
    return (yp.reshape(bp, sp, d), ys.reshape(bs, ss, d), pc, pn, pm, sc, sn, sm,
            jnp.stack([v.reshape(bs, ss, S_WIDTH) for v in sv]))
```

```python
import functools

import jax
import jax.numpy as jnp
from jax import lax
from jax.experimental import pallas as pl
from jax.experimental.pallas import tpu as pltpu

F32 = jnp.float32
BF16 = jnp.bfloat16

D_MODEL = 2048
DEPTH = 4
N_MIXERS = 2
M_HEADS = 8
M_DV = D_MODEL // M_HEADS
M_DK = M_DV // 2
HK = M_HEADS * M_DK
HV = M_HEADS * M_DV
A_MAIN = 2 * HK + 2 * HV
S_GROUPS = 8
S_CHUNK = 128
S_WIDTH = D_MODEL
S_DG = S_WIDTH // S_GROUPS
P_HEADS = 8
P_NKEYS = 128
P_NEXP = P_NKEYS * P_NKEYS
P_DH = 128
P_TOPK = 16
ALPHA = (2 * DEPTH) ** 0.25
LN_EPS = 1e-5

LANES = 128
BF16_ROWS = 16
VMEM_LIMIT = 56 * 1024 * 1024
NEG_INF = float("-inf")
TOPK_MARK = 2.0 ** 120

MLSTM_CHUNK = 512
MLSTM_PAD = 16
MLSTM_HEADS_PER_STEP = 4
PEER_TB = 512
PEER_TE = 1024
MM_ROWS = 1024
RES_ROWS = 512


def _params(sem):
    return pltpu.CompilerParams(dimension_semantics=sem, vmem_limit_bytes=VMEM_LIMIT)


def _layer_norm_rows(z, g, b):
    mu = jnp.mean(z, axis=-1, keepdims=True)
    zc = z - mu
    var = jnp.mean(zc * zc, axis=-1, keepdims=True)
    return zc * lax.rsqrt(var + LN_EPS) * g + b


def _mm_kernel(*refs, act, has_bias):
    if has_bias:
        x_ref, w_ref, b_ref, o_ref = refs
    else:
        x_ref, w_ref, o_ref = refs
    acc = jnp.dot(x_ref[...].astype(BF16), w_ref[...].astype(BF16), preferred_element_type=F32)
    if has_bias:
        acc = acc + b_ref[...]
    if act == "gelu":
        acc = jax.nn.gelu(acc)
    o_ref[...] = acc.astype(o_ref.dtype)


def _mm(x, w, layer, bias=None, *, n=None, act=None, out_dtype=F32, tm=None, tn=1024):
    m, k = x.shape
    n = w.shape[2] if n is None else n
    tn = min(tn, n)
    if tm is None:
        tm = MM_ROWS if m % MM_ROWS == 0 else m
    in_specs = [
        pl.BlockSpec((tm, k), lambda i, j: (i, 0)),
        pl.BlockSpec((None, k, tn), lambda i, j: (layer, 0, j)),
    ]
    args = [x, w]
    if bias is not None:
        in_specs.append(pl.BlockSpec((None, 1, tn), lambda i, j: (layer, 0, j)))
        args.append(bias)
    return pl.pallas_call(
        functools.partial(_mm_kernel, act=act, has_bias=bias is not None),
        grid=(m // tm, n // tn),
        in_specs=in_specs,
        out_specs=pl.BlockSpec((tm, tn), lambda i, j: (i, j)),
        out_shape=jax.ShapeDtypeStruct((m, n), out_dtype),
        compiler_params=_params(("parallel", "parallel")),
        name="mm",
    )(*args)


def _mm_ln_kernel(a_ref, w_ref, r_ref, g_ref, b_ref, of_ref, ob_ref):
    y = jnp.dot(a_ref[...].astype(BF16), w_ref[...], preferred_element_type=F32)
    o = _layer_norm_rows(ALPHA * r_ref[...] + y, g_ref[...], b_ref[...])
    of_ref[...] = o
    ob_ref[...] = o.astype(BF16)


def _ln_spec(n, li, s):
    return pl.BlockSpec((None, None, 1, n), lambda *_: (li, s, 0, 0))


def _mm_ln(a, w, layer, res, ln_g, ln_b, li, *, tm=RES_ROWS):
    m, k = a.shape
    n = w.shape[2]
    return pl.pallas_call(
        _mm_ln_kernel,
        grid=(m // tm,),
        in_specs=[
            pl.BlockSpec((tm, k), lambda i: (i, 0)),
            pl.BlockSpec((None, k, n), lambda i: (layer, 0, 0)),
            pl.BlockSpec((tm, n), lambda i: (i, 0)),
            _ln_spec(n, li, 0),
            _ln_spec(n, li, 0),
        ],
        out_specs=[pl.BlockSpec((tm, n), lambda i: (i, 0)), pl.BlockSpec((tm, n), lambda i: (i, 0))],
        out_shape=[jax.ShapeDtypeStruct((m, n), F32), jax.ShapeDtypeStruct((m, n), BF16)],
        compiler_params=_params(("parallel",)),
        name="mm_ln",
    )(a, w, res, ln_g, ln_b)


def _ple_kernel(xf_ref, xb_ref, gw_ref, p_ref, pw_ref, of_ref, ob_ref):
    gate = jax.nn.sigmoid(jnp.dot(xb_ref[...], gw_ref[...], preferred_element_type=F32))
    pe = jnp.dot(p_ref[...].astype(BF16), pw_ref[...], preferred_element_type=F32)
    o = xf_ref[...] + gate * pe
    of_ref[...] = o
    ob_ref[...] = o.astype(BF16)


def _ple(xf, xb, gw, p, pw, li, *, tm=RES_ROWS):
    m, n = xf.shape
    kp = p.shape[2]
    return pl.pallas_call(
        _ple_kernel,
        grid=(m // tm,),
        in_specs=[
            pl.BlockSpec((tm, n), lambda i: (i, 0)),
            pl.BlockSpec((tm, n), lambda i: (i, 0)),
            pl.BlockSpec((None, n, n), lambda i: (li, 0, 0)),
            pl.BlockSpec((None, tm, kp), lambda i: (li, i, 0)),
            pl.BlockSpec((None, kp, n), lambda i: (li, 0, 0)),
        ],
        out_specs=[pl.BlockSpec((tm, n), lambda i: (i, 0)), pl.BlockSpec((tm, n), lambda i: (i, 0))],
        out_shape=[jax.ShapeDtypeStruct((m, n), F32), jax.ShapeDtypeStruct((m, n), BF16)],
        compiler_params=_params(("parallel",)),
        name="ple",
    )(xf, xb, gw, p, pw)


def _group_causal(rr, cc, group):
    shift = group.bit_length() - 1
    same = (rr >> shift) == (cc >> shift)
    return jnp.where(rr >= cc, jnp.where(same, 1.0, 0.0), 0.0)


def _split3(x):
    hi = x.astype(BF16)
    r1 = x - hi.astype(F32)
    mid = r1.astype(BF16)
    lo = (r1 - mid.astype(F32)).astype(BF16)
    return hi, mid, lo


def _gate_prep_kernel(g_ref, gb_ref, x_ref, xt_ref, *, seg, valid):
    rows = g_ref.shape[0]
    g = g_ref[...] + gb_ref[...]
    lane = lax.broadcasted_iota(jnp.int32, (rows, LANES), 1)
    r = lax.broadcasted_iota(jnp.int32, (rows, LANES), 0)
    pad = (r & (seg - 1)) >= valid
    lf = jnp.where(pad, 0.0, jax.nn.log_sigmoid(g))
    ig = jnp.where(pad, NEG_INF, g)
    rr = lax.broadcasted_iota(jnp.int32, (rows, rows), 0)
    cc = lax.broadcasted_iota(jnp.int32, (rows, rows), 1)
    tri = _group_causal(rr, cc, seg).astype(BF16)
    hi, mid, lo = _split3(lf)
    bcum = (jnp.dot(tri, hi, preferred_element_type=F32)
            + jnp.dot(tri, mid, preferred_element_type=F32)
            + jnp.dot(tri, lo, preferred_element_type=F32))
    x = jnp.where(lane >= M_HEADS, bcum, ig)
    x_ref[...] = x
    xt_ref[...] = x.T


def _gate_prep(gates, gbias, layer, *, seg, valid, rows):
    m = gates.shape[0]
    return pl.pallas_call(
        functools.partial(_gate_prep_kernel, seg=seg, valid=valid),
        grid=(m // rows,),
        in_specs=[pl.BlockSpec((rows, LANES), lambda i: (i, 0)),
                  pl.BlockSpec((None, 1, LANES), lambda i: (layer, 0, 0))],
        out_specs=[pl.BlockSpec((rows, LANES), lambda i: (i, 0)), pl.BlockSpec((LANES, rows), lambda i: (0, i))],
        out_shape=[jax.ShapeDtypeStruct((m, LANES), F32), jax.ShapeDtypeStruct((LANES, m), F32)],
        compiler_params=_params(("parallel",)),
        name="gate_prep",
    )(gates, gbias)


def _mlstm_kernel(q_ref, k_ref, v_ref, o_ref, x_ref, irow_ref, brow_ref, nw_ref, c0_ref, n0_ref, m0_ref,
                  *rest, bb, seg, hp, n_prev):
    hg_ref, c_ref, n_ref, m_ref, cs, ns, ms = rest[n_prev:]
    h0 = pl.program_id(1) * hp
    c = pl.program_id(2)
    rows = bb * seg
    lane = lax.broadcasted_iota(jnp.int32, (rows, LANES), 1)

    @pl.when(c == 0)
    def _():
        cs[...] = c0_ref[...]
        ns[...] = n0_ref[...]
        ms[...] = jnp.broadcast_to(m0_ref[...], (bb, hp, 1, LANES))

    x = x_ref[...]
    rr = lax.broadcasted_iota(jnp.int32, (rows, rows), 0)
    cc = lax.broadcasted_iota(jnp.int32, (rows, rows), 1)
    visible = _group_causal(rr, cc, seg) > 0.0
    seg_id = lax.broadcasted_iota(jnp.int32, (rows, 1), 0) >> (seg.bit_length() - 1)

    def per_row(vals):
        out = vals[0]
        for bi in range(1, bb):
            out = jnp.where(seg_id == bi, vals[bi], out)
        return out

    ends = [(bi + 1) * seg - 1 for bi in range(bb)]
    for g in range(hp):
        h = h0 + g
        icol = jnp.sum(jnp.where(lane == h, x, 0.0), axis=1, keepdims=True)
        bcol = jnp.sum(jnp.where(lane == h + M_HEADS, x, 0.0), axis=1, keepdims=True)
        irow = irow_ref[g]
        brow = brow_ref[g]
        m_prev_b = [ms[bi, g][:, :1] for bi in range(bb)]
        b_last_b = [bcol[r:r + 1, :] for r in ends]
        m_prev = per_row(m_prev_b)
        b_last = per_row(b_last_b)
        n_rows = per_row([ns[bi, g] for bi in range(bb)])

        q = q_ref[:, g * M_DK:(g + 1) * M_DK].astype(F32) * (M_DK ** -0.5)
        k = k_ref[:, g * M_DK:(g + 1) * M_DK].astype(F32)
        qb = q.astype(BF16)
        kb = k.astype(BF16)
        vcols = slice(g * M_DV, (g + 1) * M_DV)
        vb = v_ref[:, vcols].astype(BF16)

        logd = jnp.where(visible, bcol - brow + irow, NEG_INF)
        inter = bcol + m_prev
        m_row = jnp.maximum(inter, jnp.max(logd, axis=1, keepdims=True))
        dmat = jnp.exp(logd - m_row)
        s = lax.dot_general(qb, kb, (((1,), (1,)), ((), ())), preferred_element_type=F32) * dmat
        w_inter = jnp.exp(inter - m_row)
        qc = [jnp.dot(qb[bi * seg:(bi + 1) * seg], cs[bi, g].astype(BF16), preferred_element_type=F32)
              for bi in range(bb)]
        qc = qc[0] if bb == 1 else jnp.concatenate(qc, axis=0)
        num = w_inter * qc + jnp.dot(s.astype(BF16), vb, preferred_element_type=F32)
        den = w_inter * jnp.sum(q * n_rows, axis=1, keepdims=True) + jnp.sum(s, axis=1, keepdims=True)
        hh = num / jnp.maximum(jnp.abs(den), jnp.exp(-m_row))
        mu = jnp.mean(hh, axis=1, keepdims=True)
        hc = hh - mu
        var = jnp.mean(hc * hc, axis=1, keepdims=True)
        hn = hc * lax.rsqrt(var + LN_EPS) * nw_ref[:, vcols]
        hg_ref[:, vcols] = (hn * jax.nn.sigmoid(o_ref[:, vcols].astype(F32))).astype(hg_ref.dtype)

        m_new_b = [m_row[r:r + 1, :] for r in ends]
        g_tok = jnp.exp(b_last - bcol + icol - per_row(m_new_b))
        kg = g_tok * k
        kgb = kg.astype(BF16)
        for bi in range(bb):
            lo_r, hi_r = bi * seg, (bi + 1) * seg
            g_state = jnp.exp(b_last_b[bi] + m_prev_b[bi] - m_new_b[bi])
            c_new = g_state * cs[bi, g] + lax.dot_general(
                kgb[lo_r:hi_r], vb[lo_r:hi_r], (((0,), (0,)), ((), ())), preferred_element_type=F32)
            n_new = g_state * ns[bi, g] + jnp.sum(kg[lo_r:hi_r], axis=0, keepdims=True)
            m_new = jnp.broadcast_to(m_new_b[bi], (1, LANES))
            cs[bi, g] = c_new
            ns[bi, g] = n_new
            ms[bi, g] = m_new
            c_ref[bi, g] = c_new
            n_ref[bi, g] = n_new
            m_ref[bi, g] = m_new


def _mlstm_scan(proj, gates, gbias, norm_w, c0, n0, m0, layer, prev, *, batch, nc, seg, valid, bb,
                hp=MLSTM_HEADS_PER_STEP):
    rows = bb * seg
    nb = batch // bb
    n_a = c0.shape[0]
    kq, kv = M_DK, M_DV
    xg, xgt = _gate_prep(gates, gbias, layer, seg=seg, valid=valid, rows=rows)
    xgt = xgt[:2 * M_HEADS].reshape(2 * M_HEADS, 1, -1)
    nh = M_HEADS // hp
    row_map = lambda col: (lambda i, h, c: (i * nc + c, col(h)))
    st_map = lambda i, h, c: (layer, i, h, 0, 0)
    in_specs = [
        pl.BlockSpec((rows, hp * kq), row_map(lambda h: h)),
        pl.BlockSpec((rows, hp * kq), row_map(lambda h: HK // (hp * kq) + h)),
        pl.BlockSpec((rows, hp * kv), row_map(lambda h: 2 * HK // (hp * kv) + h)),
        pl.BlockSpec((rows, hp * kv), row_map(lambda h: (2 * HK + HV) // (hp * kv) + h)),
        pl.BlockSpec((rows, LANES), lambda i, h, c: (i * nc + c, 0)),
        pl.BlockSpec((hp, 1, rows), lambda i, h, c: (h, 0, i * nc + c)),
        pl.BlockSpec((hp, 1, rows), lambda i, h, c: (h + nh, 0, i * nc + c)),
        pl.BlockSpec((None, 1, hp * kv), lambda i, h, c: (layer, 0, h)),
        pl.BlockSpec((None, bb, hp, kq, kv), st_map),
        pl.BlockSpec((None, bb, hp, 1, kq), st_map),
        pl.BlockSpec((None, bb, hp, 1, 1), st_map),
    ]
    args = [proj, proj, proj, proj, xg, xgt, xgt, norm_w, c0,
            n0.reshape(n_a, batch, M_HEADS, 1, kq), m0.reshape(n_a, batch, M_HEADS, 1, 1)]
    aliases = {}
    n_prev = len(prev)
    for t, arr in enumerate(prev):
        aliases[len(args)] = 1 + t
        in_specs.append(pl.BlockSpec(memory_space=pl.ANY))
        args.append(arr)
    hg, c_out, n_out, m_out = pl.pallas_call(
        functools.partial(_mlstm_kernel, bb=bb, seg=seg, hp=hp, n_prev=n_prev),
        grid=(nb, nh, nc),
        in_specs=in_specs,
        out_specs=[
            pl.BlockSpec((rows, hp * kv), lambda i, h, c: (i * nc + c, h)),
            pl.BlockSpec((None, bb, hp, kq, kv), st_map),
            pl.BlockSpec((None, bb, hp, 1, kq), st_map),
            pl.BlockSpec((None, bb, hp, 1, LANES), st_map),
        ],
        out_shape=[
            jax.ShapeDtypeStruct((batch * nc * seg, HV), BF16),
            jax.ShapeDtypeStruct((n_a, batch, M_HEADS, kq, kv), F32),
            jax.ShapeDtypeStruct((n_a, batch, M_HEADS, 1, kq), F32),
            jax.ShapeDtypeStruct((n_a, batch, M_HEADS, 1, LANES), F32),
        ],
        scratch_shapes=[
            pltpu.VMEM((bb, hp, kq, kv), F32),
            pltpu.VMEM((bb, hp, 1, kq), F32),
            pltpu.VMEM((bb, hp, 1, LANES), F32),
        ],
        input_output_aliases=aliases,
        compiler_params=_params(("parallel", "parallel", "arbitrary")),
        name="mlstm_scan",
    )(*args)
    return hg, (c_out, n_out, m_out)


def _mix_kernel(h_ref, w_ref, bias_ref, vw_ref, um_ref, *maybe_v_ref, lc):
    t = h_ref.shape[0]
    vraw = h_ref[:, S_WIDTH:].astype(F32)
    mu = jnp.mean(vraw, axis=-1, keepdims=True)
    vc = vraw - mu
    var = jnp.mean(vc * vc, axis=-1, keepdims=True)
    v = vc * lax.rsqrt(var + LN_EPS) * vw_ref[...]
    for v_ref in maybe_v_ref:
        v_ref[...] = v
    vb = v.astype(BF16)
    rr = lax.broadcasted_iota(jnp.int32, (t, t), 0)
    cc = lax.broadcasted_iota(jnp.int32, (t, t), 1)
    keep = _group_causal(rr, cc, lc)
    for g in range(S_GROUPS):
        sl = slice(g * S_DG, (g + 1) * S_DG)
        wg = (w_ref[g] * keep).astype(BF16)
        mixed = jnp.dot(wg, vb[:, sl], preferred_element_type=F32) + bias_ref[:, g:g + 1]
        um_ref[:, sl] = (h_ref[:, sl].astype(F32) * mixed).astype(BF16)


def _mix(hdn, wmix, bias_t, vnorm_w, layer, *, lc, emit_v):
    m = hdn.shape[0]
    t = S_CHUNK
    row_spec = pl.BlockSpec((t, S_WIDTH), lambda i: (i, 0))
    out_specs = [row_spec, row_spec] if emit_v else [row_spec]
    out_shape = [jax.ShapeDtypeStruct((m, S_WIDTH), BF16)]
    if emit_v:
        out_shape.append(jax.ShapeDtypeStruct((m, S_WIDTH), F32))
    outs = pl.pallas_call(
        functools.partial(_mix_kernel, lc=lc),
        grid=(m // t,),
        in_specs=[
            pl.BlockSpec((t, 2 * S_WIDTH), lambda i: (i, 0)),
            pl.BlockSpec((None, S_GROUPS, t, t), lambda i: (layer, 0, 0, 0)),
            pl.BlockSpec((None, t, S_GROUPS), lambda i: (layer, 0, 0)),
            pl.BlockSpec((None, 1, S_WIDTH), lambda i: (layer, 0, 0)),
        ],
        out_specs=out_specs,
        out_shape=out_shape,
        compiler_params=_params(("parallel",)),
        name="chunk_mix",
    )(hdn, wmix, bias_t, vnorm_w)
    return (outs[0], outs[1]) if emit_v else (outs[0], None)


def _route_kernel(xb_ref, wq_ref, keys_ref, r1_ref, e1_ref, nn_ref, c0_ref, q_s, sc_s, wk_s, rk_s, sv_s):
    tm = xb_ref.shape[0]
    q_s[...] = jnp.dot(xb_ref[...], wq_ref[...], preferred_element_type=F32)
    kidx = lax.broadcasted_iota(jnp.int32, (P_NKEYS, tm), 0).astype(F32)
    for h in range(P_HEADS):
        for c in range(2):
            hc = 2 * h + c
            qhc = q_s[:, hc * P_DH:(hc + 1) * P_DH].astype(BF16)
            sc = lax.dot_general(keys_ref[c], qhc, (((1,), (1,)), ((), ())), preferred_element_type=F32)
            sc_s[hc] = sc

        def top16(exact, h=h):
            for c in range(2):
                wk_s[c] = sc_s[2 * h + c]
                if exact:
                    rk_s[2 * h + c] = jnp.full((P_NKEYS, tm), float(P_NKEYS), F32)

            def body(kk, carry):
                kf = lax.convert_element_type(kk, F32)
                for c in range(2):
                    hc = 2 * h + c
                    s = wk_s[c]
                    mx = jnp.max(s, axis=0, keepdims=True)
                    sel = s == mx
                    if exact:
                        idx = jnp.min(jnp.where(sel, kidx, float(P_NKEYS)), axis=0, keepdims=True)
                        sel = kidx == idx
                        rk_s[hc] = jnp.where(sel, kf, rk_s[hc])
                        wk_s[c] = jnp.where(sel, NEG_INF, s)
                    else:
                        wk_s[c] = jnp.where(sel, (kf + 1.0) * -TOPK_MARK, s)
                    sv_s[c, kk, h:h + 1, :] = mx
                return carry

            lax.fori_loop(0, P_TOPK, body, 0)

        top16(exact=False)
        removed = jnp.zeros((1, tm), F32)
        for c in range(2):
            w = wk_s[c]
            gone = w <= -TOPK_MARK
            removed = removed + jnp.sum(jnp.where(gone, 1.0, 0.0), axis=0, keepdims=True)
            rk_s[2 * h + c] = jnp.where(gone, w * (-1.0 / TOPK_MARK) - 1.0, float(P_NKEYS))
        tied = jnp.max(jnp.abs(removed - 2.0 * P_TOPK)) > 0.0

        @pl.when(tied)
        def _(top16=top16):
            top16(exact=True)

    sv0 = [sv_s[0, k] for k in range(P_TOPK)]
    sv1 = [sv_s[1, k] for k in range(P_TOPK)]
    cnt = [jnp.zeros((P_HEADS, tm), F32) for _ in range(P_TOPK)]
    front = [sv0[k] + sv1[0] for k in range(P_TOPK)]
    top = front[0]
    z = jnp.zeros((P_HEADS, tm), F32)
    for r in range(P_TOPK):
        live = min(r + 1, P_TOPK)
        mx = front[0]
        for k in range(1, live):
            mx = jnp.maximum(mx, front[k])
        pick = jnp.full((P_HEADS, tm), float(P_TOPK), F32)
        for k in reversed(range(live)):
            pick = jnp.where(front[k] == mx, float(k), pick)
        z = z + jnp.exp(mx - top)
        hits = [pick == float(k) for k in range(live)]
        newcnt = jnp.zeros((P_HEADS, tm), F32)
        for k in range(live):
            cnt[k] = cnt[k] + jnp.where(hits[k], 1.0, 0.0)
            newcnt = jnp.where(hits[k], cnt[k], newcnt)
        nxt = jnp.full((P_HEADS, tm), NEG_INF, F32)
        for j in range(1, min(r + 2, P_TOPK)):
            nxt = jnp.where(newcnt == float(j), sv1[j], nxt)
        for k in range(live):
            front[k] = jnp.where(hits[k], sv0[k] + nxt, front[k])
    zinv = 1.0 / z

    for h in range(P_HEADS):
        rank0 = rk_s[2 * h]
        nn = jnp.zeros((P_NKEYS, tm), F32)
        for k in range(P_TOPK):
            nn = jnp.where(rank0 == float(k), cnt[k][h:h + 1, :], nn)
        nn_ref[h] = nn
        c0_ref[h] = jnp.exp(sc_s[2 * h] - sv0[0][h:h + 1, :]) * zinv[h:h + 1, :]
        r1_ref[h] = pltpu.bitcast(rk_s[2 * h + 1].astype(BF16), jnp.int32)
        e1_ref[h] = pltpu.bitcast(jnp.exp(sc_s[2 * h + 1] - sv1[0][h:h + 1, :]).astype(BF16), jnp.int32)


def _route(xb, wq, keys, li, *, tm=512):
    m, k = xb.shape
    tab = jax.ShapeDtypeStruct((P_HEADS, P_NKEYS, m), F32)
    tab16 = jax.ShapeDtypeStruct((P_HEADS, P_NKEYS // 2, m), jnp.int32)
    tab_spec = pl.BlockSpec((P_HEADS, P_NKEYS, tm), lambda i: (0, 0, i))
    tab16_spec = pl.BlockSpec((P_HEADS, P_NKEYS // 2, tm), lambda i: (0, 0, i))
    return pl.pallas_call(
        _route_kernel,
        grid=(m // tm,),
        in_specs=[
            pl.BlockSpec((tm, k), lambda i: (i, 0)),
            pl.BlockSpec((None, k, P_HEADS * 2 * P_DH), lambda i: (li, 0, 0)),
            pl.BlockSpec((None, 2, P_NKEYS, P_DH), lambda i: (li, 0, 0, 0)),
        ],
        out_specs=[tab16_spec, tab16_spec, tab_spec, tab_spec],
        out_shape=[tab16, tab16, tab, tab],
        scratch_shapes=[
            pltpu.VMEM((tm, P_HEADS * 2 * P_DH), F32),
            pltpu.VMEM((2 * P_HEADS, P_NKEYS, tm), F32),
            pltpu.VMEM((2, P_NKEYS, tm), F32),
            pltpu.VMEM((2 * P_HEADS, P_NKEYS, tm), F32),
            pltpu.VMEM((2, P_TOPK, P_HEADS, tm), F32),
        ],
        compiler_params=_params(("parallel",)),
        name="peer_route",
    )(xb, wq, keys)


def _gelu_tanh(x):
    c = 0.7978845608028654
    inner = x + x * x * x * 0.044715
    return x * 0.5 * (jnp.tanh(inner * c) + 1.0)


def _peer_kernel(xb_ref, u_ref, v_ref, r1_ref, e1_ref, nn_ref, c0_ref, xf_ref, g_ref, b_ref,
                 of_ref, ob_ref, acc_s, st_s, act_s, xt_s, *, te):
    e = pl.program_id(1)
    last = pl.num_programs(1) - 1
    n_i1 = te // P_NKEYS
    tb = st_s.shape[1]

    @pl.when(e == 0)
    def _():
        acc_s[...] = jnp.zeros_like(acc_s)
        xt_s[...] = xb_ref[...].T

    st_s[...] = jnp.dot(u_ref[...], xt_s[...], preferred_element_type=F32)
    base = e * n_i1
    shape3 = (P_NKEYS // BF16_ROWS, BF16_ROWS, LANES)
    zero = jnp.zeros(shape3, BF16)
    def slab(j, carry):
        rows = pl.ds(pl.multiple_of(j * P_NKEYS, P_NKEYS), P_NKEYS)
        for t in range(tb // LANES):
            cols = slice(t * LANES, (t + 1) * LANES)
            gate = zero
            for h in range(P_HEADS):
                nn = nn_ref[h, pl.ds(base + j, 1), :][:, cols]
                c0 = c0_ref[h, pl.ds(base + j, 1), :][:, cols]
                nn = jnp.broadcast_to(nn, (BF16_ROWS, LANES)).astype(BF16)[None]
                c0 = jnp.broadcast_to(c0, (BF16_ROWS, LANES)).astype(BF16)[None]
                r1 = pltpu.bitcast(r1_ref[h, :, cols], BF16).reshape(shape3)
                e1 = pltpu.bitcast(e1_ref[h, :, cols], BF16).reshape(shape3)
                gate = gate + jnp.where(r1 < nn, e1, zero) * c0
            act = _gelu_tanh(st_s[rows, cols].astype(BF16).reshape(shape3)) * gate
            act_s[rows, cols] = act.reshape(P_NKEYS, LANES)
        return carry

    lax.fori_loop(0, n_i1, slab, 0)
    acc_s[...] += lax.dot_general(act_s[...], v_ref[...], (((0,), (0,)), ((), ())),
                                  preferred_element_type=F32)

    @pl.when(e == last)
    def _():
        o = _layer_norm_rows(ALPHA * xf_ref[...] + acc_s[...], g_ref[...], b_ref[...])
        of_ref[...] = o
        ob_ref[...] = o.astype(BF16)


def _peer(xb, xf, u, v, r1, e1, nn, c0, ln_g, ln_b, li, *, tb=PEER_TB, te=PEER_TE):
    m, d = xf.shape
    ne = P_NEXP // te
    once = pl.Buffered(1)
    tab_spec = pl.BlockSpec((P_HEADS, P_NKEYS, tb), lambda i, e: (0, 0, i), pipeline_mode=once)
    tab16_spec = pl.BlockSpec((P_HEADS, P_NKEYS // 2, tb), lambda i, e: (0, 0, i), pipeline_mode=once)
    return pl.pallas_call(
        functools.partial(_peer_kernel, te=te),
        grid=(m // tb, ne),
        in_specs=[
            pl.BlockSpec((tb, d), lambda i, e: (i, 0)),
            pl.BlockSpec((None, te, d), lambda i, e: (li, e, 0)),
            pl.BlockSpec((None, te, d), lambda i, e: (li, e, 0)),
            tab16_spec, tab16_spec, tab_spec, tab_spec,
            pl.BlockSpec((tb, d), lambda i, e: (i, 0)),
            _ln_spec(d, li, 1),
            _ln_spec(d, li, 1),
        ],
        out_specs=[pl.BlockSpec((tb, d), lambda i, e: (i, 0), pipeline_mode=once),
                   pl.BlockSpec((tb, d), lambda i, e: (i, 0), pipeline_mode=once)],
        out_shape=[jax.ShapeDtypeStruct((m, d), F32), jax.ShapeDtypeStruct((m, d), BF16)],
        scratch_shapes=[
            pltpu.VMEM((tb, d), F32),
            pltpu.VMEM((te, tb), F32),
            pltpu.VMEM((te, tb), BF16),
            pltpu.VMEM((d, tb), BF16),
        ],
        compiler_params=_params(("parallel", "arbitrary")),
        name="peer_dense",
    )(xb, u, v, r1, e1, nn, c0, xf, ln_g, ln_b)


def _trunk(x, p, c_in, n_in, m_in, *, batch, seq, wts, want_v):
    m = batch * seq
    xf = x
    xb = x.astype(BF16)
    decode = seq < MLSTM_PAD
    lc = min(seq, S_CHUNK)
    n_a = c_in.shape[0]
    states = (jnp.zeros((n_a, batch, M_HEADS, M_DK, M_DV), F32),
              jnp.zeros((n_a, batch, M_HEADS, 1, M_DK), F32),
              jnp.zeros((n_a, batch, M_HEADS, 1, LANES), F32))
    vs = []
    for i in range(DEPTH):
        j = i // N_MIXERS
        if i % N_MIXERS == 0:
            proj = _mm(xb, wts["a_in"], j, n=A_MAIN, out_dtype=BF16)
            gates = _mm(xb, wts["a_gate"], j)
            if decode:
                seg, valid, nc, bb = MLSTM_PAD, seq, 1, 8
                proj, gates = (
                    jnp.pad(a.reshape(batch, seq, -1), ((0, 0), (0, seg - seq), (0, 0))).reshape(batch * seg, -1)
                    for a in (proj, gates))
            else:
                seg, valid, nc, bb = MLSTM_CHUNK, MLSTM_CHUNK, seq // MLSTM_CHUNK, 1
            hg, states = _mlstm_scan(proj, gates, wts["a_gbias"], wts["a_norm"], c_in, n_in, m_in, j, states,
                                     batch=batch, nc=nc, seg=seg, valid=valid, bb=bb)
            if decode:
                hg = hg.reshape(batch, seg, HV)[:, :seq].reshape(m, HV)
            w_out = wts["a_out"]
        else:
            hdn = _mm(xb, wts["b_in"], j, wts["b_in_bias"], act="gelu", out_dtype=BF16)
            hg, v = _mix(hdn, wts["b_mix"][lc], wts["b_mix_bias"][lc], wts["b_norm"], j, lc=lc,
                         emit_v=want_v)
            vs.append(v)
            w_out = wts["b_out"]
        x1f, x1b = _mm_ln(hg, w_out, j, xf, wts["ln_g"], wts["ln_b"], i)
        r1, e1, nn, c0 = _route(x1b, wts["peer_wq"], wts["peer_keys"], i)
        x2f, x2b = _peer(x1b, x1f, wts["peer_u"], wts["peer_v"], r1, e1, nn, c0, wts["ln_g"], wts["ln_b"], i)
        xf, xb = _ple(x2f, x2b, wts["ple_gate"], p, wts["ple_w"], i)
    c_out, n_out, m_out = states
    return xf, c_out, n_out[:, :, :, 0, :], m_out[:, :, :, 0, 0], vs


def _prep_weights(w_a_in, b_a_gate, a_norm_w, w_a_out, w_b_in, b_b_in, b_norm_w, w_b_s, b_b_s, w_b_out,
                  ln_g, ln_b, peer_wq, peer_keys, peer_u, peer_v, ple_w, ple_gate_w, chunk_lens):
    n_a = w_a_in.shape[0]
    n_b = w_b_in.shape[0]
    gpad = LANES - 2 * M_HEADS
    wts = {
        "a_in": w_a_in[:, :, :A_MAIN].astype(BF16),
        "a_gate": jnp.pad(w_a_in[:, :, A_MAIN:], ((0, 0), (0, 0), (0, gpad))),
        "a_gbias": jnp.pad(b_a_gate, ((0, 0), (0, gpad))).reshape(n_a, 1, LANES).astype(F32),
        "a_norm": a_norm_w.reshape(n_a, 1, HV).astype(F32),
        "a_out": w_a_out.astype(BF16),
        "b_in": w_b_in,
        "b_in_bias": b_b_in.reshape(n_b, 1, -1).astype(F32),
        "b_norm": b_norm_w.reshape(n_b, 1, S_WIDTH).astype(F32),
        "b_out": w_b_out.astype(BF16),
        "b_mix": {},
        "b_mix_bias": {},
        "ln_g": ln_g.reshape(DEPTH, 2, 1, D_MODEL).astype(F32),
        "ln_b": ln_b.reshape(DEPTH, 2, 1, D_MODEL).astype(F32),
        "peer_wq": peer_wq.astype(BF16),
        "peer_keys": peer_keys.astype(BF16),
        "peer_u": peer_u.astype(BF16),
        "peer_v": peer_v.astype(BF16),
        "ple_w": ple_w.astype(BF16),
        "ple_gate": ple_gate_w.astype(BF16),
    }
    for lc in chunk_lens:
        rep = S_CHUNK // lc
        if rep == 1:
            wts["b_mix"][lc] = w_b_s.astype(F32)
        else:
            wts["b_mix"][lc] = jnp.tile(w_b_s[:, :, :lc, :lc], (1, 1, rep, rep)).astype(F32)
        wts["b_mix_bias"][lc] = jnp.swapaxes(jnp.tile(b_b_s[:, :, :lc], (1, 1, rep)), 1, 2).astype(F32)
    return wts


def kernel(x_prompt, x_sample, state_C, state_n, state_m, p_prompt, p_sample, w_a_in, b_a_gate, a_norm_w, w_a_out, w_b_in, b_b_in, b_norm_w, w_b_s, b_b_s, w_b_out, ln_g, ln_b, peer_wq, peer_keys, peer_u, peer_v, ple_w, ple_gate_w):
    bp, sp, d = x_prompt.shape
    bs, ss, _ = x_sample.shape
    n_a = state_C.shape[0]
    chunk_lens = sorted({min(sp, S_CHUNK), min(ss, S_CHUNK)})
    wts = _prep_weights(w_a_in, b_a_gate, a_norm_w, w_a_out, w_b_in, b_b_in, b_norm_w, w_b_s, b_b_s, w_b_out,
                        ln_g, ln_b, peer_wq, peer_keys, peer_u, peer_v, ple_w, ple_gate_w, chunk_lens)

    c0 = jnp.zeros((n_a, bp, M_HEADS, M_DK, M_DV), F32)
    n0 = jnp.zeros((n_a, bp, M_HEADS, M_DK), F32)
    m0 = jnp.zeros((n_a, bp, M_HEADS), F32)
    yp, pc, pn, pm, _ = _trunk(x_prompt.reshape(bp * sp, d), p_prompt.reshape(DEPTH, bp * sp, -1),
                               c0, n0, m0, batch=bp, seq=sp, wts=wts, want_v=False)
    ys, sc, sn, sm, sv = _trunk(x_sample.reshape(bs * ss, d), p_sample.reshape(DEPTH, bs * ss, -1),
                                state_C, state_n, state_m, batch=bs, seq=ss, wts=wts, want_v=True)
    return (yp.reshape(bp, sp, d), ys.reshape(bs, ss, d), pc, pn, pm, sc, sn, sm,
            jnp.stack([v.reshape(bs, ss, S_WIDTH) for v in sv]))
```
